```python
import math
import jax, jax.numpy as jnp
from jax import lax
import numpy as np

D_MODEL = 2048
BATCH = 4
SEQ = 2048
DEPTH = 4
DEC_BATCH = 8
DEC_SEQ = 8
PAST_LEN = 16384
PAGE_SIZE = 128

HEAD_DIM = 128
D_MIX = D_MODEL
D_POOL = D_MIX // 4
POOL_WINDOWS = (2, 4, 8, 16)
N_POOL_GROUPS = len(POOL_WINDOWS)
POOL_GROUP = D_POOL // N_POOL_GROUPS
POOL_HIST = max(POOL_WINDOWS) - 1
D_ATT = (D_MIX - D_POOL) // 2
H_SB = D_ATT // HEAD_DIM
H_MB = D_ATT // HEAD_DIM
D_PROJ = 2 * D_POOL + 4 * D_ATT + 4 * D_ATT
SPLIT_SIZES = (D_POOL, D_POOL, D_ATT, D_ATT, D_ATT, D_ATT, D_ATT, D_ATT, D_ATT, D_ATT)
SB_QBLOCK = 128
MOBA_BLOCK = 256
MOBA_TOPK = 3
MOBA_QCHUNK = 32
REL_BUCKETS = 32
REL_MAX_DIST = 128
EPS = 1e-6

kernel_name = 'hymba_pool_stickbreak_moba_step'

F32 = jnp.float32


def rmsnorm(x, g):
    xf = x.astype(F32)
    y = xf * lax.rsqrt(jnp.mean(xf * xf, axis=-1, keepdims=True) + EPS)
    return (y * g.astype(F32)).astype(x.dtype)


def rel_bucket(n):
    n = jnp.maximum(n, 0)
    max_exact = REL_BUCKETS // 2
    nf = jnp.maximum(n, 1).astype(F32)
    large = max_exact + (jnp.log(nf / max_exact) / math.log(REL_MAX_DIST / max_exact)
                         * (REL_BUCKETS - max_exact)).astype(jnp.int32)
    large = jnp.minimum(large, REL_BUCKETS - 1)
    return jnp.where(n < max_exact, n, large)


def pool_mix(u, hist, pos0, w_pool, scale):
    B, n, _ = u.shape
    ext = jnp.concatenate([hist.astype(u.dtype), u], axis=1)
    cs = jnp.concatenate([jnp.zeros((B, 1, D_POOL), F32),
                          jnp.cumsum(ext.astype(F32), axis=1)], axis=1)
    pos = pos0 + jnp.arange(n, dtype=jnp.int32)
    top = cs[:, POOL_HIST + 1:]
    means = []
    for g, w in enumerate(POOL_WINDOWS):
        sl = slice(g * POOL_GROUP, (g + 1) * POOL_GROUP)
        s = top[:, :, sl] - cs[:, POOL_HIST + 1 - w:POOL_HIST + 1 - w + n, sl]
        cnt = jnp.minimum(w, pos + 1).astype(F32)
        means.append(s / cnt[None, :, None])
    pooled = (jnp.concatenate(means, axis=-1) - u.astype(F32)).reshape(B, n, N_POOL_GROUPS, POOL_GROUP)
    y = jnp.einsum('bngc,gcd->bngd', pooled, w_pool.astype(F32)).reshape(B, n, D_POOL)
    return y * scale.astype(F32), ext[:, -POOL_HIST:]


def sb_mix(q, k, v, q_start):
    B, nq, H, D = q.shape
    T = k.shape[1]
    qb = min(SB_QBLOCK, nq)
    n_blk = -(-nq // qb)
    q32 = jnp.pad(q.astype(F32), ((0, 0), (0, n_blk * qb - nq), (0, 0), (0, 0)))
    qs = q32.reshape(B, n_blk, qb, H, D).transpose(1, 0, 2, 3, 4)
    starts = q_start + qb * jnp.arange(n_blk, dtype=jnp.int32)
    k32 = k.astype(F32)
    v32 = v.astype(F32)
    k_pos = jnp.arange(T, dtype=jnp.int32)
    scale = 1.0 / math.sqrt(D)

    def one_block(args):
        qq, s = args
        q_pos = s + jnp.arange(qb, dtype=jnp.int32)
        z = jnp.einsum('bqhd,bkhd->bhqk', qq, k32) * scale
        mask = (k_pos[None, :] < q_pos[:, None])[None, None]
        log_keep = jnp.where(mask, jax.nn.log_sigmoid(-z), 0.0)
        log_after = lax.cumsum(log_keep, axis=3, reverse=True) - log_keep
        a = jnp.where(mask, jnp.exp(jax.nn.log_sigmoid(z) + log_after), 0.0)
        return jnp.einsum('bhqk,bkhd->bqhd', a, v32)

    out = lax.map(one_block, (qs, starts))
    return out.transpose(1, 0, 2, 3, 4).reshape(B, n_blk * qb, H, D)[:, :nq]


def moba_mix(q, k, v, q_start, rel_bias):
    B, nq, H, D = q.shape
    T = k.shape[1]
    qc = min(MOBA_QCHUNK, nq)
    n_ch = -(-nq // qc)
    nq_pad = n_ch * qc
    t_pad = -(-(q_start + nq_pad) // MOBA_BLOCK) * MOBA_BLOCK
    nb = t_pad // MOBA_BLOCK
    q32 = jnp.pad(q.astype(F32), ((0, 0), (0, nq_pad - nq), (0, 0), (0, 0)))
    k32 = jnp.pad(k.astype(F32), ((0, 0), (0, t_pad - T), (0, 0), (0, 0)))
    v32 = jnp.pad(v.astype(F32), ((0, 0), (0, t_pad - T), (0, 0), (0, 0)))
    kb = k32.reshape(B, nb, MOBA_BLOCK, H, D)
    k_mean = jnp.mean(kb, axis=2)
    kb_t = kb.transpose(0, 3, 1, 2, 4)
    vb_t = v32.reshape(B, nb, MOBA_BLOCK, H, D).transpose(0, 3, 1, 2, 4)
    qs = q32.reshape(B, n_ch, qc, H, D).transpose(1, 0, 2, 3, 4)
    starts = q_start + qc * jnp.arange(n_ch, dtype=jnp.int32)
    bi = jnp.arange(B)[:, None, None, None]
    hi = jnp.arange(H)[None, None, :, None]
    hi5 = hi[..., None]
    blk_pos = jnp.arange(MOBA_BLOCK, dtype=jnp.int32)
    blk_ids = jnp.arange(nb, dtype=jnp.int32)
    bias_tab = rel_bias.astype(F32)
    scale = 1.0 / math.sqrt(D)
    n_sel = MOBA_TOPK * MOBA_BLOCK

    def one_chunk(args):
        qq, s = args
        q_pos = s + jnp.arange(qc, dtype=jnp.int32)
        own = s // MOBA_BLOCK
        gate = jnp.einsum('bqhd,bnhd->bqhn', qq, k_mean)
        gate = jnp.where(blk_ids < own, gate, -jnp.inf)
        if nb < MOBA_TOPK:
            gate = jnp.pad(gate, ((0, 0), (0, 0), (0, 0), (0, MOBA_TOPK - nb)), constant_values=-jnp.inf)
        _, idx = lax.top_k(gate, MOBA_TOPK)
        valid = idx < own
        idx = jnp.minimum(idx, nb - 1)
        k_sel = kb_t[bi, hi, idx]
        v_sel = vb_t[bi, hi, idx]
        k_pos_sel = idx[..., None] * MOBA_BLOCK + blk_pos
        dist = q_pos[None, :, None, None, None] - k_pos_sel
        s_sel = jnp.einsum('bqhd,bqhnkd->bqhnk', qq, k_sel) * scale + bias_tab[rel_bucket(dist), hi5]
        s_sel = jnp.where(valid[..., None], s_sel, -jnp.inf).reshape(B, qc, H, n_sel)
        own_start = own * MOBA_BLOCK
        k_own = lax.dynamic_slice_in_dim(k32, own_start, MOBA_BLOCK, axis=1)
        v_own = lax.dynamic_slice_in_dim(v32, own_start, MOBA_BLOCK, axis=1)
        dist_own = q_pos[:, None] - (own_start + blk_pos)[None, :]
        s_own = (jnp.einsum('bqhd,bkhd->bqhk', qq, k_own) * scale
                 + bias_tab[rel_bucket(dist_own)].transpose(0, 2, 1)[None])
        s_own = jnp.where((dist_own >= 0)[None, :, None, :], s_own, -jnp.inf)
        p = jax.nn.softmax(jnp.concatenate([s_sel, s_own], axis=-1), axis=-1)
        p_sel = p[..., :n_sel].reshape(B, qc, H, MOBA_TOPK, MOBA_BLOCK)
        p_own = p[..., n_sel:]
        return (jnp.einsum('bqhnk,bqhnkd->bqhd', p_sel, v_sel)
                + jnp.einsum('bqhk,bkhd->bqhd', p_own, v_own))

    out = lax.map(one_chunk, (qs, starts))
    return out.transpose(1, 0, 2, 3, 4).reshape(B, nq_pad, H, D)[:, :nq]


def mixer_layer(x, q_start, past, hist, norm_g, w_in, w_pool, pool_scale, q_norm_g, k_norm_g, w_out, rel_bias):
    B, n, _ = x.shape
    h = rmsnorm(x, norm_g)
    p = jnp.einsum('bnd,de->bne', h, w_in)
    cuts = np.cumsum(SPLIT_SIZES)[:-1].tolist()
    u, g_a, q_b, k_b, v_b, g_b, q_c, k_c, v_c, g_c = jnp.split(p, cuts, axis=-1)
    a_out, new_hist = pool_mix(u, hist, q_start, w_pool, pool_scale)
    heads = lambda t: t.reshape(B, n, -1, HEAD_DIM)
    q_b, k_b, v_b = heads(q_b), heads(k_b), heads(v_b)
    q_c = rmsnorm(heads(q_c), q_norm_g)
    k_c = rmsnorm(heads(k_c), k_norm_g)
    v_c = heads(v_c)
    if past is None:
        kb_all, vb_all, kc_all, vc_all = k_b, v_b, k_c, v_c
    else:
        pk_b, pv_b, pk_c, pv_c = past
        kb_all = jnp.concatenate([pk_b.astype(k_b.dtype), k_b], axis=1)
        vb_all = jnp.concatenate([pv_b.astype(v_b.dtype), v_b], axis=1)
        kc_all = jnp.concatenate([pk_c.astype(k_c.dtype), k_c], axis=1)
        vc_all = jnp.concatenate([pv_c.astype(v_c.dtype), v_c], axis=1)
    b_out = sb_mix(q_b, kb_all, vb_all, q_start).reshape(B, n, D_ATT)
    c_out = moba_mix(q_c, kc_all, vc_all, q_start, rel_bias).reshape(B, n, D_ATT)
    silu = lambda t: jax.nn.silu(t.astype(F32))
    mixed = jnp.concatenate([a_out * silu(g_a), b_out * silu(g_b), c_out * silu(g_c)], axis=-1)
    y = x + jnp.einsum('bne,ed->bnd', mixed.astype(x.dtype), w_out)
    return y, k_b, v_b, k_c, v_c, new_hist


def setup_inputs(seed: int = 0) -> dict:
    key = jax.random.key(seed)
    ks = jax.random.split(key, 16)
    n_pages = PAST_LEN // PAGE_SIZE
    n_used = DEC_BATCH * n_pages
    n_phys = n_used + (n_used + 3) // 4
    page_table = jax.random.permutation(ks[0], n_phys)[:n_used].reshape(DEC_BATCH, n_pages).astype(jnp.int32)
    cache_shape_sb = (DEPTH, n_phys, PAGE_SIZE, H_SB, HEAD_DIM)
    cache_shape_mb = (DEPTH, n_phys, PAGE_SIZE, H_MB, HEAD_DIM)
    return {
        'x_prompt': jax.random.normal(ks[1], (BATCH, SEQ, D_MODEL), F32),
        'x_sample': jax.random.normal(ks[2], (DEC_BATCH, DEC_SEQ, D_MODEL), F32),
        'cache_sb_k': jax.random.normal(ks[3], cache_shape_sb, F32),
        'cache_sb_v': jax.random.normal(ks[4], cache_shape_sb, F32),
        'cache_moba_k': jax.random.normal(ks[5], cache_shape_mb, F32),
        'cache_moba_v': jax.random.normal(ks[6], cache_shape_mb, F32),
        'state_pool': jax.random.normal(ks[7], (DEPTH, DEC_BATCH, POOL_HIST, D_POOL), F32),
        'page_table': page_table,
        'norm_g': 1.0 + 0.02 * jax.random.normal(ks[8], (DEPTH, D_MODEL), F32),
        'w_in': jax.random.normal(ks[9], (DEPTH, D_MODEL, D_PROJ), F32) * D_MODEL ** -0.5,
        'w_pool': jax.random.normal(ks[10], (DEPTH, N_POOL_GROUPS, POOL_GROUP, POOL_GROUP), F32) * POOL_GROUP ** -0.5,
        'pool_scale': 1.0 + 0.02 * jax.random.normal(ks[11], (DEPTH, D_POOL), F32),
        'q_norm_g': 1.0 + 0.02 * jax.random.normal(ks[12], (DEPTH, HEAD_DIM), F32),
        'k_norm_g': 1.0 + 0.02 * jax.random.normal(ks[13], (DEPTH, HEAD_DIM), F32),
        'w_out': jax.random.normal(ks[14], (DEPTH, D_MIX, D_MODEL), F32) * D_MIX ** -0.5,
        'rel_bias': 0.2 * jax.random.normal(ks[15], (REL_BUCKETS, H_MB), F32),
    }


def reference(x_prompt, x_sample, cache_sb_k, cache_sb_v, cache_moba_k, cache_moba_v, state_pool,
              page_table, norm_g, w_in, w_pool, pool_scale, q_norm_g, k_norm_g, w_out, rel_bias):
    n_dec, n_pages = page_table.shape
    past_len = n_pages * PAGE_SIZE

    def gather(cache_l):
        g = cache_l[page_table]
        return g.reshape(n_dec, past_len, g.shape[-2], g.shape[-1])

    hist0 = jnp.zeros((x_prompt.shape[0], POOL_HIST, D_POOL), x_prompt.dtype)
    yp, ys = x_prompt, x_sample
    p_sbk, p_sbv, p_mbk, p_mbv, p_pool = [], [], [], [], []
    s_sbk, s_sbv, s_mbk, s_mbv, s_pool = [], [], [], [], []
    for l in range(DEPTH):
        wl = (norm_g[l], w_in[l], w_pool[l], pool_scale[l], q_norm_g[l], k_norm_g[l], w_out[l], rel_bias)
        yp, kb, vb, kc, vc, hp = mixer_layer(yp, 0, None, hist0, *wl)
        p_sbk.append(kb); p_sbv.append(vb); p_mbk.append(kc); p_mbv.append(vc); p_pool.append(hp)
        past = (gather(cache_sb_k[l]), gather(cache_sb_v[l]), gather(cache_moba_k[l]), gather(cache_moba_v[l]))
        ys, kb, vb, kc, vc, hs = mixer_layer(ys, past_len, past, state_pool[l], *wl)
        s_sbk.append(kb); s_sbv.append(vb); s_mbk.append(kc); s_mbv.append(vc); s_pool.append(hs)
    return (yp, ys,
            jnp.stack(p_sbk), jnp.stack(p_sbv), jnp.stack(p_mbk), jnp.stack(p_mbv), jnp.stack(p_pool),
            jnp.stack(s_sbk), jnp.stack(s_sbv), jnp.stack(s_mbk), jnp.stack(s_mbv), jnp.stack(s_pool))
```

```python
import functools
import math

import numpy as np
import jax
import jax.numpy as jnp
from jax import lax
from jax.experimental import pallas as pl
from jax.experimental.pallas import tpu as pltpu

F32 = jnp.float32
BF16 = jnp.bfloat16

HEAD_DIM = 128
POOL_WINDOWS = (2, 4, 8, 16)
POOL_GROUP = 128
POOL_HIST = max(POOL_WINDOWS) - 1
HALO = POOL_HIST + 1
PAGE_SIZE = 128
MOBA_BLOCK = 256
MOBA_TOPK = 3
REL_BUCKETS = 32
REL_MAX_DIST = 128
EPS = 1e-6
LANES = 128
VMEM_LIMIT_BYTES = 48 * 1024 * 1024
PAGES_PER_STEP = 4
NEG_INF = float("-inf")

NT_DIMS = (((1,), (1,)), ((), ()))


def _dot(a, b):
    return jnp.dot(a, b, preferred_element_type=F32)


def _dot_nt(a, b):
    return lax.dot_general(a, b, NT_DIMS, preferred_element_type=F32)


def _split_bf16(x):
    hi = x.astype(BF16)
    lo = (x - hi.astype(F32)).astype(BF16)
    return hi, lo


def _log_sigmoid(z):
    return jnp.minimum(z, 0.0) - jnp.log1p(jnp.exp(-jnp.abs(z)))


def _silu(g):
    return g / (1.0 + jnp.exp(-g))


def _params(semantics):
    return pltpu.CompilerParams(dimension_semantics=semantics, vmem_limit_bytes=VMEM_LIMIT_BYTES)


def _in_proj_kernel(l_ref, x_ref, g_ref, w_ref, qkg_ref, o_ref, h_ref, *, norm_lo, norm_hi, head_major):
    j = pl.program_id(2)

    @pl.when(j == 0)
    def _():
        x = x_ref[...]
        ms = jnp.mean(x * x, axis=-1, keepdims=True)
        h_ref[...] = (x * lax.rsqrt(ms + EPS) * g_ref[...]).astype(BF16)

    acc = _dot(h_ref[...], w_ref[...])
    heads_per_tile = acc.shape[1] // HEAD_DIM
    for s in range(heads_per_tile):
        sl = slice(s * HEAD_DIM, (s + 1) * HEAD_DIM)
        head = j * heads_per_tile + s
        needs_norm = jnp.logical_and(head >= norm_lo, head < norm_hi)

        def store(val, s=s, sl=sl):
            if head_major:
                o_ref[s] = val
            else:
                o_ref[:, sl] = val

        @pl.when(needs_norm)
        def _(sl=sl, store=store):
            blk = acc[:, sl]
            ms = jnp.mean(blk * blk, axis=-1, keepdims=True)
            store(blk * lax.rsqrt(ms + EPS) * qkg_ref[:, sl])

        @pl.when(jnp.logical_not(needs_norm))
        def _(sl=sl, store=store):
            store(acc[:, sl])


def _in_proj(lidx, x2d, norm_g, w_bf, qkg, nb, n, tm, tn, norm_cols, head_major):
    T, d_model = x2d.shape
    cols = w_bf.shape[-1]
    assert T == nb * n and n % tm == 0 and cols % tn == 0 and tn % HEAD_DIM == 0
    nt = n // tm
    kern = functools.partial(_in_proj_kernel, norm_lo=norm_cols[0] // HEAD_DIM, norm_hi=norm_cols[1] // HEAD_DIM,
                             head_major=head_major)
    if head_major:
        out_spec = pl.BlockSpec((None, tn // HEAD_DIM, tm, HEAD_DIM), lambda b, i, j, l: (b, j, i, 0))
        out_shape = jax.ShapeDtypeStruct((nb, cols // HEAD_DIM, n, HEAD_DIM), F32)
    else:
        out_spec = pl.BlockSpec((tm, tn), lambda b, i, j, l: (b * nt + i, j))
        out_shape = jax.ShapeDtypeStruct((T, cols), F32)
    return pl.pallas_call(
        kern,
        grid_spec=pltpu.PrefetchScalarGridSpec(
            num_scalar_prefetch=1,
            grid=(nb, nt, cols // tn),
            in_specs=[
                pl.BlockSpec((tm, d_model), lambda b, i, j, l: (b * nt + i, 0)),
                pl.BlockSpec((None, 1, d_model), lambda b, i, j, l: (l[0], 0, 0)),
                pl.BlockSpec((None, d_model, tn), lambda b, i, j, l: (l[0], 0, j)),
                pl.BlockSpec((None, 1, tn), lambda b, i, j, l: (l[0], 0, j)),
            ],
            out_specs=out_spec,
            scratch_shapes=[pltpu.VMEM((tm, d_model), BF16)],
        ),
        out_shape=out_shape,
        compiler_params=_params(("parallel", "parallel", "arbitrary")),
        name="in_proj_heads" if head_major else "in_proj",
    )(lidx, x2d, norm_g, w_bf, qkg)


def _pool_kernel(l_ref, u_ref, ga_ref, halo_ref, wp_ref, ps_ref, o_ref, ext_ref, *, tm, pos0):
    i = pl.program_id(1)
    ext_ref[0:HALO, :] = halo_ref[...]
    ext_ref[HALO:, :] = u_ref[...]
    pos = pos0 + i * tm + lax.broadcasted_iota(jnp.int32, (tm, 1), 0)
    for g, w in enumerate(POOL_WINDOWS):
        sl = slice(g * POOL_GROUP, (g + 1) * POOL_GROUP)
        s = ext_ref[HALO:HALO + tm, sl]
        for d in range(1, w):
            s = s + ext_ref[HALO - d:HALO - d + tm, sl]
        cnt = jnp.minimum(w, pos + 1).astype(F32)
        pooled = s / cnt - u_ref[:, sl]
        y = _dot(pooled.astype(BF16), wp_ref[g].astype(BF16)) * ps_ref[:, sl]
        o_ref[:, sl] = (y * _silu(ga_ref[:, sl])).astype(o_ref.dtype)


def _pool(lidx, p2d, halo, w_pool, pool_scale, nb, n, tm, pos0, out_dtype):
    d_pool = POOL_GROUP * len(POOL_WINDOWS)
    nt = n // tm
    kern = functools.partial(_pool_kernel, tm=tm, pos0=pos0)
    return pl.pallas_call(
        kern,
        grid_spec=pltpu.PrefetchScalarGridSpec(
            num_scalar_prefetch=1,
            grid=(nb, nt),
            in_specs=[
                pl.BlockSpec((tm, d_pool), lambda b, i, l: (b * nt + i, 0)),
                pl.BlockSpec((tm, d_pool), lambda b, i, l: (b * nt + i, 1)),
                pl.BlockSpec((None, None, HALO, d_pool), lambda b, i, l: (b, i, 0, 0)),
                pl.BlockSpec((None, len(POOL_WINDOWS), POOL_GROUP, POOL_GROUP), lambda b, i, l: (l[0], 0, 0, 0)),
                pl.BlockSpec((None, 1, d_pool), lambda b, i, l: (l[0], 0, 0)),
            ],
            out_specs=pl.BlockSpec((tm, d_pool), lambda b, i, l: (b * nt + i, 0)),
            scratch_shapes=[pltpu.VMEM((HALO + tm, d_pool), F32)],
        ),
        out_shape=jax.ShapeDtypeStruct((nb * n, d_pool), out_dtype),
        compiler_params=_params(("parallel", "arbitrary")),
        name="pool",
    )(lidx, p2d, p2d, halo, w_pool, pool_scale)


def _sb_prefill_kernel(q_ref, k_ref, v_ref, g_ref, o_ref, acc_ref, cs_ref, *, tq, scale):
    i = pl.program_id(2)
    q = q_ref[...].astype(BF16)
    row = lax.broadcasted_iota(jnp.int32, (tq, tq), 0)
    col = lax.broadcasted_iota(jnp.int32, (tq, tq), 1)
    later = jnp.where(row > col, 1.0, 0.0).astype(BF16)
    keep = col < row

    def tile(kj, diagonal):
        off = pl.multiple_of(kj * tq, tq)
        k = k_ref[pl.ds(off, tq), :].astype(BF16)
        v = v_ref[pl.ds(off, tq), :].astype(BF16)
        z = _dot_nt(q, k) * scale
        ls = _log_sigmoid(z)
        lk = ls - z
        if diagonal:
            lk = jnp.where(keep, lk, 0.0)
        hi, lo = _split_bf16(lk)
        la = _dot(hi, later) + _dot(lo, later) + cs_ref[...]
        a = jnp.exp(ls + la)
        if diagonal:
            a = jnp.where(keep, a, 0.0)
        acc_ref[...] += _dot(a.astype(BF16), v)
        cs_ref[...] += jnp.sum(lk, axis=1, keepdims=True)

    acc_ref[...] = jnp.zeros_like(acc_ref)
    cs_ref[...] = jnp.zeros_like(cs_ref)
    tile(i, True)

    def body(t, carry):
        tile(i - 1 - t, False)
        return carry

    lax.fori_loop(0, i, body, 0)
    o_ref[...] = (acc_ref[...] * _silu(g_ref[...])).astype(o_ref.dtype)


def _sb_prefill(p2d, kv, nb, n, nh, q_col, g_col, k_head, v_head, tq):
    nq = n // tq
    kern = functools.partial(_sb_prefill_kernel, tq=tq, scale=1.0 / math.sqrt(HEAD_DIM))
    qc, gc = q_col // HEAD_DIM, g_col // HEAD_DIM
    return pl.pallas_call(
        kern,
        grid=(nb, nh, nq),
        in_specs=[
            pl.BlockSpec((tq, HEAD_DIM), lambda b, h, i: (b * nq + i, qc + h)),
            pl.BlockSpec((None, None, n, HEAD_DIM), lambda b, h, i: (b, k_head + h, 0, 0)),
            pl.BlockSpec((None, None, n, HEAD_DIM), lambda b, h, i: (b, v_head + h, 0, 0)),
            pl.BlockSpec((tq, HEAD_DIM), lambda b, h, i: (b * nq + i, gc + h)),
        ],
        out_specs=pl.BlockSpec((tq, HEAD_DIM), lambda b, h, i: (b * nq + i, h)),
        out_shape=jax.ShapeDtypeStruct((nb * n, nh * HEAD_DIM), BF16),
        scratch_shapes=[pltpu.VMEM((tq, HEAD_DIM), F32), pltpu.VMEM((tq, 1), F32)],
        compiler_params=_params(("parallel", "parallel", "arbitrary")),
        name="sb_prefill",
    )(p2d, kv, kv, p2d)


def _top_k_lanes(gate, k):
    lane = lax.broadcasted_iota(jnp.int32, gate.shape, 1).astype(F32)
    sel = jnp.zeros(gate.shape, F32)
    for _ in range(k):
        m = jnp.max(gate, axis=1, keepdims=True)
        cand = jnp.where(gate == m, lane, float(LANES))
        cand = jnp.where(m > NEG_INF, cand, float(LANES))
        idx = jnp.min(cand, axis=1, keepdims=True)
        pick = lane == idx
        sel = jnp.where(pick, 1.0, sel)
        gate = jnp.where(pick, NEG_INF, gate)
    return sel


def _gate_scores(qf, km):
    q_hi, q_lo = _split_bf16(qf)
    k_hi, k_lo = _split_bf16(km)
    return _dot_nt(q_hi, k_hi) + _dot_nt(q_hi, k_lo) + _dot_nt(q_lo, k_hi)


def _moba_prefill_kernel(q_ref, k_ref, v_ref, g_ref, bias_ref, o_ref,
                         km_ref, m_ref, l_ref, acc_ref, *, nblk, scale):
    i = pl.program_id(2)
    blk = MOBA_BLOCK

    @pl.when(i == 0)
    def _():
        km_ref[...] = jnp.zeros_like(km_ref)
        for b in range(nblk):
            km_ref[b:b + 1, :] = jnp.sum(k_ref[b * blk:(b + 1) * blk, :], axis=0, keepdims=True) * (1.0 / blk)

    qf = q_ref[...]
    q = qf.astype(BF16)
    lane = lax.broadcasted_iota(jnp.int32, (blk, LANES), 1)
    gate = jnp.where(lane < i, _gate_scores(qf, km_ref[...]), NEG_INF)
    sel = _top_k_lanes(gate, MOBA_TOPK)
    row = lax.broadcasted_iota(jnp.int32, (blk, blk), 0)
    col = lax.broadcasted_iota(jnp.int32, (blk, blk), 1)

    m_ref[...] = jnp.full_like(m_ref, NEG_INF)
    l_ref[...] = jnp.zeros_like(l_ref)
    acc_ref[...] = jnp.zeros_like(acc_ref)

    def tile(j, own):
        off = pl.multiple_of(j * blk, blk)
        k = k_ref[pl.ds(off, blk), :].astype(BF16)
        v = v_ref[pl.ds(off, blk), :].astype(BF16)
        s = _dot_nt(q, k) * scale + bias_ref[i - j]
        if own:
            s = jnp.where(col <= row, s, NEG_INF)
        else:
            chosen = jnp.max(jnp.where(lane == j, sel, 0.0), axis=1, keepdims=True) > 0.0
            s = jnp.where(chosen, s, NEG_INF)
        m_prev = m_ref[...]
        m_new = jnp.maximum(m_prev, jnp.max(s, axis=1, keepdims=True))
        alpha = jnp.exp(m_prev - m_new)
        p = jnp.exp(s - m_new)
        l_ref[...] = alpha * l_ref[...] + jnp.sum(p, axis=1, keepdims=True)
        acc_ref[...] = alpha * acc_ref[...] + _dot(p.astype(BF16), v)
        m_ref[...] = m_new

    tile(i, True)

    def body(j, carry):
        tile(j, False)
        return carry

    lax.fori_loop(0, i, body, 0)
    o_ref[...] = (acc_ref[...] / l_ref[...] * _silu(g_ref[...])).astype(o_ref.dtype)


def _moba_prefill(p2d, kv, bias_tiles, nb, n, nh, q_col, g_col, k_head, v_head):
    blk = MOBA_BLOCK
    assert n % blk == 0
    nq = n // blk
    assert nq <= LANES
    kern = functools.partial(_moba_prefill_kernel, nblk=nq, scale=1.0 / math.sqrt(HEAD_DIM))
    qc, gc = q_col // HEAD_DIM, g_col // HEAD_DIM
    return pl.pallas_call(
        kern,
        grid=(nb, nh, nq),
        in_specs=[
            pl.BlockSpec((blk, HEAD_DIM), lambda b, h, i: (b * nq + i, qc + h)),
            pl.BlockSpec((None, None, n, HEAD_DIM), lambda b, h, i: (b, k_head + h, 0, 0)),
            pl.BlockSpec((None, None, n, HEAD_DIM), lambda b, h, i: (b, v_head + h, 0, 0)),
            pl.BlockSpec((blk, HEAD_DIM), lambda b, h, i: (b * nq + i, gc + h)),
            pl.BlockSpec((None, nq, blk, blk), lambda b, h, i: (h, 0, 0, 0)),
        ],
        out_specs=pl.BlockSpec((blk, HEAD_DIM), lambda b, h, i: (b * nq + i, h)),
        out_shape=jax.ShapeDtypeStruct((nb * n, nh * HEAD_DIM), BF16),
        scratch_shapes=[pltpu.VMEM((LANES, HEAD_DIM), F32), pltpu.VMEM((blk, 1), F32),
                        pltpu.VMEM((blk, 1), F32), pltpu.VMEM((blk, HEAD_DIM), F32)],
        compiler_params=_params(("parallel", "parallel", "arbitrary")),
        name="moba_prefill",
    )(p2d, kv, kv, p2d, bias_tiles)


def _head_queries(q_ref, nh):
    return [q_ref[:, h * HEAD_DIM:(h + 1) * HEAD_DIM] for h in range(nh)]


def _stacked_scores(qs, k_ref):
    return jnp.concatenate([_dot_nt(q, k_ref[h].astype(BF16)) for h, q in enumerate(qs)], axis=0)


def _stacked_values(p, v_ref, nh, dn):
    return jnp.concatenate(
        [_dot(p[h * dn:(h + 1) * dn].astype(BF16), v_ref[h].astype(BF16)) for h in range(nh)], axis=0)


def _gated_heads(acc, g, nh, dn):
    return jnp.concatenate([acc[h * dn:(h + 1) * dn] for h in range(nh)], axis=1) * _silu(g)


def _page_index_map(r, n_pages, reverse):
    pps = PAGES_PER_STEP

    def index_map(b, s, l, pt):
        page = s * pps + r
        if reverse:
            page = n_pages - 1 - page
        return (l[0], pt[b * n_pages + page], 0, 0, 0)

    return index_map


def _sb_decode_kernel(l_ref, pt_ref, q_ref, kn_ref, vn_ref, g_ref, *rest, nh, dn, scale):
    pps = PAGES_PER_STEP
    k_refs, v_refs = rest[:pps], rest[pps:2 * pps]
    o_ref, acc_ref, cs_ref = rest[2 * pps:]
    s_idx = pl.program_id(1)
    rows = nh * dn
    qs = [q.astype(BF16) for q in _head_queries(q_ref, nh)]
    row = lax.broadcasted_iota(jnp.int32, (PAGE_SIZE, PAGE_SIZE), 0)
    col = lax.broadcasted_iota(jnp.int32, (PAGE_SIZE, PAGE_SIZE), 1)
    later = jnp.where(row > col, 1.0, 0.0).astype(BF16)

    def page(kp_ref, vp_ref, keep):
        z = _stacked_scores(qs, kp_ref) * scale
        ls = _log_sigmoid(z)
        lk = ls - z
        if keep is not None:
            lk = jnp.where(keep, lk, 0.0)
        hi, lo = _split_bf16(lk)
        la = _dot(hi, later) + _dot(lo, later) + cs_ref[...]
        a = jnp.exp(ls + la)
        if keep is not None:
            a = jnp.where(keep, a, 0.0)
        acc_ref[...] += _stacked_values(a, vp_ref, nh, dn)
        cs_ref[...] += jnp.sum(lk, axis=1, keepdims=True)

    @pl.when(s_idx == 0)
    def _():
        acc_ref[...] = jnp.zeros_like(acc_ref)
        cs_ref[...] = jnp.zeros_like(cs_ref)
        tok = lax.broadcasted_iota(jnp.int32, (rows, PAGE_SIZE), 0) & (dn - 1)
        key = lax.broadcasted_iota(jnp.int32, (rows, PAGE_SIZE), 1)
        page(kn_ref, vn_ref, key < tok)

    for r in range(pps):
        page(k_refs[r], v_refs[r], None)

    @pl.when(s_idx == pl.num_programs(1) - 1)
    def _():
        o_ref[...] = _gated_heads(acc_ref[...], g_ref[...], nh, dn).astype(o_ref.dtype)


def _sb_decode(lidx, pt_flat, q, k_new, v_new, g, cache_k, cache_v, nb, dn, nh, n_pages):
    pps = PAGES_PER_STEP
    assert n_pages % pps == 0 and dn == 8
    width = nh * HEAD_DIM
    kern = functools.partial(_sb_decode_kernel, nh=nh, dn=dn, scale=1.0 / math.sqrt(HEAD_DIM))
    seq = lambda rows: pl.BlockSpec((None, rows, width), lambda b, s, l, pt: (b, 0, 0))
    new = pl.BlockSpec((None, nh, PAGE_SIZE, HEAD_DIM), lambda b, s, l, pt: (b, 0, 0, 0))
    pages = [pl.BlockSpec((None, None, nh, PAGE_SIZE, HEAD_DIM), _page_index_map(r, n_pages, True))
             for r in range(pps)]
    return pl.pallas_call(
        kern,
        grid_spec=pltpu.PrefetchScalarGridSpec(
            num_scalar_prefetch=2,
            grid=(nb, n_pages // pps),
            in_specs=[seq(dn), new, new, seq(dn)] + pages + pages,
            out_specs=seq(dn),
            scratch_shapes=[pltpu.VMEM((nh * dn, HEAD_DIM), F32), pltpu.VMEM((nh * dn, 1), F32)],
        ),
        out_shape=jax.ShapeDtypeStruct((nb, dn, width), F32),
        compiler_params=_params(("parallel", "arbitrary")),
        name="sb_decode",
    )(lidx, pt_flat, q, k_new, v_new, g, *([cache_k] * pps), *([cache_v] * pps))


def _moba_scores_kernel(l_ref, pt_ref, q_ref, *rest, nh):
    pps = PAGES_PER_STEP
    k_refs = rest[:pps]
    s_ref, km_ref = rest[pps:]
    s_idx = pl.program_id(1)
    qs = [q.astype(BF16) for q in _head_queries(q_ref, nh)]
    pages_per_block = MOBA_BLOCK // PAGE_SIZE
    blocks_per_step = pps // pages_per_block

    @pl.when(s_idx == 0)
    def _():
        km_ref[...] = jnp.zeros_like(km_ref)

    for r in range(pps):
        s_ref[:, r * PAGE_SIZE:(r + 1) * PAGE_SIZE] = _stacked_scores(qs, k_refs[r])
    block_row = lax.broadcasted_iota(jnp.int32, (LANES, HEAD_DIM), 0)
    for h in range(nh):
        km = km_ref[h]
        for c in range(blocks_per_step):
            tot = jnp.sum(k_refs[c * pages_per_block][h], axis=0, keepdims=True)
            for r in range(1, pages_per_block):
                tot = tot + jnp.sum(k_refs[c * pages_per_block + r][h], axis=0, keepdims=True)
            km = jnp.where(block_row == s_idx * blocks_per_step + c, tot * (1.0 / MOBA_BLOCK), km)
        km_ref[h] = km


def _moba_scores(lidx, pt_flat, q, cache_k, nb, dn, nh, n_pages):
    pps = PAGES_PER_STEP
    ppb = MOBA_BLOCK // PAGE_SIZE
    assert pps % ppb == 0 and n_pages % pps == 0
    width = nh * HEAD_DIM
    past = n_pages * PAGE_SIZE
    assert past // MOBA_BLOCK <= LANES
    kern = functools.partial(_moba_scores_kernel, nh=nh)
    pages = [pl.BlockSpec((None, None, nh, PAGE_SIZE, HEAD_DIM), _page_index_map(r, n_pages, False))
             for r in range(pps)]
    return pl.pallas_call(
        kern,
        grid_spec=pltpu.PrefetchScalarGridSpec(
            num_scalar_prefetch=2,
            grid=(nb, n_pages // pps),
            in_specs=[pl.BlockSpec((None, dn, width), lambda b, s, l, pt: (b, 0, 0))] + pages,
            out_specs=[
                pl.BlockSpec((None, nh * dn, pps * PAGE_SIZE), lambda b, s, l, pt: (b, 0, s)),
                pl.BlockSpec((None, nh, LANES, HEAD_DIM), lambda b, s, l, pt: (b, 0, 0, 0)),
            ],
        ),
        out_shape=[jax.ShapeDtypeStruct((nb, nh * dn, past), F32),
                   jax.ShapeDtypeStruct((nb, nh, LANES, HEAD_DIM), F32)],
        compiler_params=_params(("parallel", "arbitrary")),
        name="moba_scores",
    )(lidx, pt_flat, q, *([cache_k] * pps))


def _moba_values_kernel(l_ref, pt_ref, q_ref, km_ref, s_ref, bias_ref, kn_ref, vn_ref, bown_ref, g_ref, *rest,
                        nh, dn, n_past_blocks, scale):
    pps = PAGES_PER_STEP
    v_refs = rest[:pps]
    o_ref, p_ref, l_scr, acc_ref = rest[pps:]
    s_idx = pl.program_id(1)
    rows = nh * dn
    blk = MOBA_BLOCK

    @pl.when(s_idx == 0)
    def _():
        qf = _head_queries(q_ref, nh)
        lane = lax.broadcasted_iota(jnp.int32, (rows, LANES), 1)
        gate = jnp.concatenate([_gate_scores(qf[h], km_ref[h]) for h in range(nh)], axis=0)
        gate = jnp.where(lane < n_past_blocks, gate, NEG_INF)
        sel = _top_k_lanes(gate, MOBA_TOPK)
        tok = lax.broadcasted_iota(jnp.int32, (rows, PAGE_SIZE), 0) & (dn - 1)
        key = lax.broadcasted_iota(jnp.int32, (rows, PAGE_SIZE), 1)
        s_own = _stacked_scores([q.astype(BF16) for q in qf], kn_ref) * scale + bown_ref[...]
        s_own = jnp.where(key <= tok, s_own, NEG_INF)
        m0 = jnp.max(s_own, axis=1, keepdims=True)

        def masked_scores(j, m):
            off = pl.multiple_of(j * blk, blk)
            chosen = jnp.max(jnp.where(lane == j, sel, 0.0), axis=1, keepdims=True) > 0.0
            sm = jnp.where(chosen, s_ref[:, pl.ds(off, blk)] * scale + bias_ref[:, pl.ds(off, blk)], NEG_INF)
            p_ref[:, pl.ds(off, blk)] = sm
            return jnp.maximum(m, jnp.max(sm, axis=1, keepdims=True))

        m = lax.fori_loop(0, n_past_blocks, masked_scores, m0)

        def exponentiate(j, l):
            off = pl.multiple_of(j * blk, blk)
            p = jnp.exp(p_ref[:, pl.ds(off, blk)] - m)
            p_ref[:, pl.ds(off, blk)] = p
            return l + jnp.sum(p, axis=1, keepdims=True)

        p_own = jnp.exp(s_own - m)
        l_scr[...] = lax.fori_loop(0, n_past_blocks, exponentiate, jnp.sum(p_own, axis=1, keepdims=True))
        acc_ref[...] = _stacked_values(p_own, vn_ref, nh, dn)

    for r in range(pps):
        off = pl.multiple_of((s_idx * pps + r) * PAGE_SIZE, PAGE_SIZE)
        acc_ref[...] += _stacked_values(p_ref[:, pl.ds(off, PAGE_SIZE)], v_refs[r], nh, dn)

    @pl.when(s_idx == pl.num_programs(1) - 1)
    def _():
        o_ref[...] = _gated_heads(acc_ref[...] / l_scr[...], g_ref[...], nh, dn).astype(o_ref.dtype)


def _moba_values(lidx, pt_flat, q, km, scores, bias_past, k_new, v_new, bias_own, g, cache_v,
                 nb, dn, nh, n_pages):
    pps = PAGES_PER_STEP
    width = nh * HEAD_DIM
    rows = nh * dn
    past = n_pages * PAGE_SIZE
    n_past_blocks = past // MOBA_BLOCK
    assert n_past_blocks <= LANES and km.shape[2] == LANES
    kern = functools.partial(_moba_values_kernel, nh=nh, dn=dn, n_past_blocks=n_past_blocks,
                             scale=1.0 / math.sqrt(HEAD_DIM))
    seq = lambda r, c: pl.BlockSpec((None, r, c), lambda b, s, l, pt: (b, 0, 0))
    per_head = lambda r: pl.BlockSpec((None, nh, r, HEAD_DIM), lambda b, s, l, pt: (b, 0, 0, 0))
    shared = lambda r, c: pl.BlockSpec((r, c), lambda b, s, l, pt: (0, 0))
    pages = [pl.BlockSpec((None, None, nh, PAGE_SIZE, HEAD_DIM), _page_index_map(r, n_pages, False))
             for r in range(pps)]
    return pl.pallas_call(
        kern,
        grid_spec=pltpu.PrefetchScalarGridSpec(
            num_scalar_prefetch=2,
            grid=(nb, n_pages // pps),
            in_specs=[seq(dn, width), per_head(LANES), seq(rows, past), shared(rows, past),
                      per_head(PAGE_SIZE), per_head(PAGE_SIZE), shared(rows, PAGE_SIZE), seq(dn, width)] + pages,
            out_specs=seq(dn, width),
            scratch_shapes=[pltpu.VMEM((rows, past), F32), pltpu.VMEM((rows, 1), F32),
                            pltpu.VMEM((rows, HEAD_DIM), F32)],
        ),
        out_shape=jax.ShapeDtypeStruct((nb, dn, width), F32),
        compiler_params=_params(("parallel", "arbitrary")),
        name="moba_values",
    )(lidx, pt_flat, q, km, scores, bias_past, k_new, v_new, bias_own, g, *([cache_v] * pps))


def _out_proj_kernel(l_ref, x_ref, a_ref, b_ref, c_ref, wa_ref, wb_ref, wc_ref, o_ref):
    acc = _dot(a_ref[...].astype(BF16), wa_ref[...])
    acc += _dot(b_ref[...].astype(BF16), wb_ref[...])
    acc += _dot(c_ref[...].astype(BF16), wc_ref[...])
    o_ref[...] = x_ref[...] + acc


def _out_proj(lidx, x2d, a, b, c, wa, wb, wc, tm):
    T, d_model = x2d.shape
    assert T % tm == 0
    act = lambda arr: pl.BlockSpec((tm, arr.shape[1]), lambda i, l: (i, 0))
    wgt = lambda arr: pl.BlockSpec((None,) + arr.shape[1:], lambda i, l: (l[0], 0, 0))
    return pl.pallas_call(
        _out_proj_kernel,
        grid_spec=pltpu.PrefetchScalarGridSpec(
            num_scalar_prefetch=1,
            grid=(T // tm,),
            in_specs=[act(x2d), act(a), act(b), act(c), wgt(wa), wgt(wb), wgt(wc)],
            out_specs=pl.BlockSpec((tm, d_model), lambda i, l: (i, 0)),
        ),
        out_shape=jax.ShapeDtypeStruct((T, d_model), F32),
        compiler_params=_params(("parallel",)),
        name="out_proj",
    )(lidx, x2d, a, b, c, wa, wb, wc)


def _rel_bucket_np(n):
    n = np.maximum(n, 0)
    max_exact = REL_BUCKETS // 2
    nf = np.maximum(n, 1).astype(np.float64)
    large = max_exact + (np.log(nf / max_exact) / math.log(REL_MAX_DIST / max_exact)
                         * (REL_BUCKETS - max_exact)).astype(np.int32)
    large = np.minimum(large, REL_BUCKETS - 1)
    return np.where(n < max_exact, n, large).astype(np.int32)


def _pool_halo(u3, hist, tm):
    nb, n, d = u3.shape
    nt = n // tm
    first = jnp.pad(hist, ((0, 0), (HALO - POOL_HIST, 0), (0, 0)))[:, None]
    if nt == 1:
        return first
    tails = u3.reshape(nb, nt, tm, d)[:, :-1, tm - HALO:, :]
    return jnp.concatenate([first, tails], axis=1)


def kernel(x_prompt, x_sample, cache_sb_k, cache_sb_v, cache_moba_k, cache_moba_v, state_pool, page_table,
           norm_g, w_in, w_pool, pool_scale, q_norm_g, k_norm_g, w_out, rel_bias):
    nb, n, d_model = x_prompt.shape
    db, dn, _ = x_sample.shape
    depth, n_phys, page_size, nh, hd = cache_sb_k.shape
    n_pages = page_table.shape[1]
    past = n_pages * page_size
    assert page_size == PAGE_SIZE and hd == HEAD_DIM
    d_pool = POOL_GROUP * len(POOL_WINDOWS)
    d_att = nh * HEAD_DIM
    d_proj = w_in.shape[-1]
    assert d_proj == 2 * d_pool + 8 * d_att
    seg = lambda t: slice(2 * d_pool + t * d_att, 2 * d_pool + (t + 1) * d_att)
    w_in_bf = w_in.astype(BF16)
    w_main = jnp.concatenate([w_in_bf[..., :2 * d_pool]] + [w_in_bf[..., seg(t)] for t in (0, 3, 4, 7)], axis=-1)
    w_kv = jnp.concatenate([w_in_bf[..., seg(t)] for t in (1, 2, 5, 6)], axis=-1)
    m_qb, m_gb, m_qc, m_gc = (2 * d_pool + t * d_att for t in range(4))
    kv_kb, kv_vb, kv_kc, kv_vc = (t * nh for t in range(4))
    zeros_att = jnp.zeros((depth, d_att), F32)
    g_main = jnp.concatenate([jnp.zeros((depth, m_qc), F32), jnp.tile(q_norm_g, (1, nh)), zeros_att], axis=-1)[:, None]
    g_kv = jnp.concatenate([zeros_att, zeros_att, jnp.tile(k_norm_g, (1, nh)), zeros_att], axis=-1)[:, None]
    norm_main = (m_qc, m_qc + d_att)
    norm_kv = (kv_kc * HEAD_DIM, (kv_kc + nh) * HEAD_DIM)

    w_out_bf = w_out.astype(BF16)
    wa, wb, wc = w_out_bf[:, :d_pool], w_out_bf[:, d_pool:d_pool + d_att], w_out_bf[:, d_pool + d_att:]
    norm_g3 = norm_g[:, None, :]
    pool_scale3 = pool_scale[:, None, :]
    caches = [jnp.transpose(c, (0, 1, 3, 2, 4)) for c in (cache_sb_k, cache_sb_v, cache_moba_k, cache_moba_v)]
    pt_flat = page_table.reshape(-1).astype(jnp.int32)

    bias_by_dist = rel_bias.astype(F32).T[:, _rel_bucket_np(np.arange(past + dn + 1))]
    nq = n // MOBA_BLOCK
    r = np.arange(MOBA_BLOCK)
    dist_tiles = np.maximum((np.arange(nq) * MOBA_BLOCK)[:, None, None] + r[None, :, None] - r[None, None, :], 0)
    bias_tiles = bias_by_dist[:, dist_tiles]
    tok = np.arange(dn)
    dist_past = past + tok[:, None] - np.arange(past)[None, :]
    bias_past = bias_by_dist[:, dist_past].reshape(nh * dn, past)
    dist_own = np.maximum(tok[:, None] - np.arange(PAGE_SIZE)[None, :], 0)
    bias_own = bias_by_dist[:, dist_own].reshape(nh * dn, PAGE_SIZE)

    tm_p = 512
    hist0 = jnp.zeros((nb, POOL_HIST, d_pool), F32)

    def layer(carry, l):
        yp, ys = carry
        lidx = jnp.reshape(l, (1,)).astype(jnp.int32)

        p = _in_proj(lidx, yp, norm_g3, w_main, g_main, nb, n, tm_p, 512, norm_main, False)
        kv = _in_proj(lidx, yp, norm_g3, w_kv, g_kv, nb, n, tm_p, 512, norm_kv, True)
        u3 = p.reshape(nb, n, -1)[:, :, :d_pool]
        a = _pool(lidx, p, _pool_halo(u3, hist0, tm_p), w_pool, pool_scale3, nb, n, tm_p, 0, BF16)
        bmix = _sb_prefill(p, kv, nb, n, nh, m_qb, m_gb, kv_kb, kv_vb, 256)
        cmix = _moba_prefill(p, kv, bias_tiles, nb, n, nh, m_qc, m_gc, kv_kc, kv_vc)
        yp_new = _out_proj(lidx, yp, a, bmix, cmix, wa, wb, wc, tm_p)
        outs_p = tuple(kv[:, h0:h0 + nh] for h0 in (kv_kb, kv_vb, kv_kc, kv_vc)) + (u3[:, n - POOL_HIST:],)

        ps = _in_proj(lidx, ys, norm_g3, w_main, g_main, 1, db * dn, db * dn, 512, norm_main, False)
        kvs = _in_proj(lidx, ys, norm_g3, w_kv, g_kv, 1, db * dn, db * dn, 512, norm_kv, False)
        ps3 = ps.reshape(db, dn, -1)
        kvs5 = jnp.transpose(kvs.reshape(db, dn, 4, nh, HEAD_DIM), (2, 0, 3, 1, 4))
        hist_s = lax.dynamic_index_in_dim(state_pool, l, axis=0, keepdims=False)
        us3 = ps3[:, :, :d_pool]
        a_s = _pool(lidx, ps, _pool_halo(us3, hist_s, dn), w_pool, pool_scale3, db, dn, dn, past, F32)
        col = lambda c: ps3[:, :, c:c + d_att]
        new_page = lambda t: jnp.pad(kvs5[t], ((0, 0), (0, 0), (0, PAGE_SIZE - dn), (0, 0)))
        b_s = _sb_decode(lidx, pt_flat, col(m_qb), new_page(0), new_page(1), col(m_gb),
                         caches[0], caches[1], db, dn, nh, n_pages)
        scores, km = _moba_scores(lidx, pt_flat, col(m_qc), caches[2], db, dn, nh, n_pages)
        c_s = _moba_values(lidx, pt_flat, col(m_qc), km, scores, bias_past, new_page(2), new_page(3),
                           bias_own, col(m_gc), caches[3], db, dn, nh, n_pages)
        ys_new = _out_proj(lidx, ys, a_s, b_s.reshape(db * dn, d_att), c_s.reshape(db * dn, d_att),
                           wa, wb, wc, db * dn)
        hist_new = jnp.concatenate([hist_s, us3], axis=1)[:, -POOL_HIST:]
        outs_s = tuple(kvs5[t] for t in range(4)) + (hist_new,)
        return (yp_new, ys_new), (outs_p, outs_s)

    (yp, ys), (outs_p, outs_s) = lax.scan(
        layer, (x_prompt.reshape(nb * n, d_model), x_sample.reshape(db * dn, d_model)), jnp.arange(depth))
    rows = lambda t: jnp.transpose(t, (0, 1, 3, 2, 4))
    outs_p = tuple(rows(t) for t in outs_p[:4]) + (outs_p[4],)
    outs_s = tuple(rows(t) for t in outs_s[:4]) + (outs_s[4],)
    return (yp.reshape(nb, n, d_model), ys.reshape(db, dn, d_model)) + outs_p + outs_s
```

```python
import functools
import math

import numpy as np
import jax
import jax.numpy as jnp
from jax import lax
from jax.experimental import pallas as pl
from jax.experimental.pallas import tpu as pltpu

F32 = jnp.float32
BF16 = jnp.bfloat16

HEAD_DIM = 128
POOL_WINDOWS = (2, 4, 8, 16)
POOL_GROUP = 128
POOL_HIST = max(POOL_WINDOWS) - 1
HALO = POOL_HIST + 1
PAGE_SIZE = 128
MOBA_BLOCK = 256
MOBA_TOPK = 3
REL_BUCKETS = 32
REL_MAX_DIST = 128
EPS = 1e-6
LANES = 128
SUBLANES = 8
VMEM_LIMIT_BYTES = 48 * 1024 * 1024
HEADS_PER_STEP = 2
SB_PAGES_PER_STEP = 8
MOBA_PAGES_PER_STEP = 16
NEG_INF = float("-inf")
NEG_BIG = -1e30
UNDERFLOW_LOG = -104.0
LOG2E = 1.4426950408889634
LN2 = 0.6931471805599453

NT_DIMS = (((1,), (1,)), ((), ()))


def _dot(a, b):
    return jnp.dot(a, b, preferred_element_type=F32)


def _dot_nt(a, b):
    return lax.dot_general(a, b, NT_DIMS, preferred_element_type=F32)


def _split_bf16(x):
    hi = x.astype(BF16)
    lo = (x - hi.astype(F32)).astype(BF16)
    return hi, lo


def _log_sigmoid(z):
    return jnp.minimum(z, 0.0) - jnp.log(1.0 + jnp.exp(-jnp.abs(z)))


def _silu(g):
    return g / (1.0 + jnp.exp(-g))


def _rms_norm(x, gain):
    ms = jnp.mean(x * x, axis=-1, keepdims=True)
    return x * lax.rsqrt(ms + EPS) * gain


def _lanes(x, width):
    return x if width == LANES else jnp.concatenate([x] * (width // LANES), axis=1)


def _params(semantics):
    return pltpu.CompilerParams(dimension_semantics=semantics, vmem_limit_bytes=VMEM_LIMIT_BYTES)


def _in_proj_kernel(l_ref, x_ref, g_ref, w_ref, *rest, norm_heads, head_major):
    if norm_heads is None:
        o_ref, h_ref = rest
    else:
        qkg_ref, o_ref, h_ref = rest
    j = pl.program_id(2)

    @pl.when(j == 0)
    def _():
        h_ref[...] = _rms_norm(x_ref[...], g_ref[...]).astype(BF16)

    acc = _dot(h_ref[...], w_ref[...])
    heads_per_tile = acc.shape[1] // HEAD_DIM

    def store(s, val):
        if head_major:
            o_ref[s] = val
        else:
            o_ref[:, s * HEAD_DIM:(s + 1) * HEAD_DIM] = val

    if norm_heads is None:
        if head_major:
            for s in range(heads_per_tile):
                store(s, acc[:, s * HEAD_DIM:(s + 1) * HEAD_DIM])
        else:
            o_ref[...] = acc
        return

    for s in range(heads_per_tile):
        sl = slice(s * HEAD_DIM, (s + 1) * HEAD_DIM)
        head = j * heads_per_tile + s
        needs_norm = jnp.logical_and(head >= norm_heads[0], head < norm_heads[1])

        @pl.when(needs_norm)
        def _(s=s, sl=sl):
            store(s, _rms_norm(acc[:, sl], qkg_ref[:, sl]))

        @pl.when(jnp.logical_not(needs_norm))
        def _(s=s, sl=sl):
            store(s, acc[:, sl])


def _in_proj(lidx, x2d, norm_g, w_bf, nb, n, tm, tn, head_major, qkg=None, norm_cols=None):
    T, d_model = x2d.shape
    cols = w_bf.shape[-1]
    assert T == nb * n and n % tm == 0 and cols % tn == 0 and tn % HEAD_DIM == 0
    nt = n // tm
    norm_heads = None if qkg is None else (norm_cols[0] // HEAD_DIM, norm_cols[1] // HEAD_DIM)
    kern = functools.partial(_in_proj_kernel, norm_heads=norm_heads, head_major=head_major)
    if head_major:
        out_spec = pl.BlockSpec((None, tn // HEAD_DIM, tm, HEAD_DIM), lambda b, i, j, l: (b, j, i, 0))
        out_shape = jax.ShapeDtypeStruct((nb, cols // HEAD_DIM, n, HEAD_DIM), F32)
    else:
        out_spec = pl.BlockSpec((tm, tn), lambda b, i, j, l: (b * nt + i, j))
        out_shape = jax.ShapeDtypeStruct((T, cols), F32)
    in_specs = [
        pl.BlockSpec((tm, d_model), lambda b, i, j, l: (b * nt + i, 0)),
        pl.BlockSpec((None, 1, d_model), lambda b, i, j, l: (l[0], 0, 0)),
        pl.BlockSpec((None, d_model, tn), lambda b, i, j, l: (l[0], 0, j)),
    ]
    args = [lidx, x2d, norm_g, w_bf]
    if qkg is not None:
        in_specs.append(pl.BlockSpec((None, 1, tn), lambda b, i, j, l: (l[0], 0, j)))
        args.append(qkg)
    return pl.pallas_call(
        kern,
        grid_spec=pltpu.PrefetchScalarGridSpec(
            num_scalar_prefetch=1,
            grid=(nb, nt, cols // tn),
            in_specs=in_specs,
            out_specs=out_spec,
            scratch_shapes=[pltpu.VMEM((tm, d_model), BF16)],
        ),
        out_shape=out_shape,
        compiler_params=_params(("parallel", "parallel", "arbitrary")),
        name="in_proj_heads" if head_major else "in_proj",
    )(*args)


def _pool_kernel(l_ref, u_ref, ga_ref, halo_ref, wp_ref, ps_ref, o_ref, ext_ref, *, tm, pos0):
    i = pl.program_id(1)
    ext_ref[0:HALO, :] = halo_ref[...]
    ext_ref[HALO:, :] = u_ref[...]
    pos = pos0 + i * tm + lax.broadcasted_iota(jnp.int32, (tm, 1), 0)
    for g, w in enumerate(POOL_WINDOWS):
        sl = slice(g * POOL_GROUP, (g + 1) * POOL_GROUP)
        s = ext_ref[HALO:HALO + tm, sl]
        for d in range(1, w):
            s = s + ext_ref[HALO - d:HALO - d + tm, sl]
        cnt = jnp.minimum(w, pos + 1).astype(F32)
        pooled = s / cnt - u_ref[:, sl]
        y = _dot(pooled.astype(BF16), wp_ref[g].astype(BF16)) * ps_ref[:, sl]
        o_ref[:, sl] = (y * _silu(ga_ref[:, sl])).astype(o_ref.dtype)


def _pool(lidx, p2d, halo, w_pool, pool_scale, nb, n, tm, pos0, out_dtype):
    d_pool = POOL_GROUP * len(POOL_WINDOWS)
    nt = n // tm
    kern = functools.partial(_pool_kernel, tm=tm, pos0=pos0)
    return pl.pallas_call(
        kern,
        grid_spec=pltpu.PrefetchScalarGridSpec(
            num_scalar_prefetch=1,
            grid=(nb, nt),
            in_specs=[
                pl.BlockSpec((tm, d_pool), lambda b, i, l: (b * nt + i, 0)),
                pl.BlockSpec((tm, d_pool), lambda b, i, l: (b * nt + i, 1)),
                pl.BlockSpec((None, None, HALO, d_pool), lambda b, i, l: (b, i, 0, 0)),
                pl.BlockSpec((None, len(POOL_WINDOWS), POOL_GROUP, POOL_GROUP), lambda b, i, l: (l[0], 0, 0, 0)),
                pl.BlockSpec((None, 1, d_pool), lambda b, i, l: (l[0], 0, 0)),
            ],
            out_specs=pl.BlockSpec((tm, d_pool), lambda b, i, l: (b * nt + i, 0)),
            scratch_shapes=[pltpu.VMEM((HALO + tm, d_pool), F32)],
        ),
        out_shape=jax.ShapeDtypeStruct((nb * n, d_pool), out_dtype),
        compiler_params=_params(("parallel", "arbitrary")),
        name="pool",
    )(lidx, p2d, p2d, halo, w_pool, pool_scale)


def _sb_prefill_kernel(q_ref, k_ref, v_ref, g_ref, o_ref, acc_ref, cs_ref, *, tq, scale):
    i = pl.program_id(2)
    hps = HEADS_PER_STEP
    qs = [q_ref[:, h * HEAD_DIM:(h + 1) * HEAD_DIM].astype(BF16) for h in range(hps)]
    row = lax.broadcasted_iota(jnp.int32, (tq, tq), 0)
    col = lax.broadcasted_iota(jnp.int32, (tq, tq), 1)
    later = jnp.where(row > col, 1.0, 0.0).astype(BF16)

    def tiles(kj, diagonal):
        off = pl.multiple_of(kj * tq, tq)
        z = jnp.concatenate([_dot_nt(qs[h], k_ref[h, pl.ds(off, tq), :].astype(BF16)) for h in range(hps)],
                            axis=0) * scale
        ls = _log_sigmoid(z)
        lk = ls - z
        if diagonal:
            keep = jnp.concatenate([col < row] * hps, axis=0)
            lk = jnp.where(keep, lk, 0.0)
        hi, lo = _split_bf16(lk)
        la = _dot(hi, later) + _dot(lo, later) + _lanes(cs_ref[...], tq)
        a = jnp.exp(ls + la)
        if diagonal:
            a = jnp.where(keep, a, 0.0)
        for h in range(hps):
            v = v_ref[h, pl.ds(off, tq), :].astype(BF16)
            acc_ref[h] += _dot(a[h * tq:(h + 1) * tq].astype(BF16), v)
        cs = cs_ref[...] + jnp.sum(lk, axis=1, keepdims=True)
        cs_ref[...] = cs
        return jnp.max(cs)

    acc_ref[...] = jnp.zeros_like(acc_ref)
    cs_ref[...] = jnp.zeros_like(cs_ref)
    cs_max = tiles(i, True)

    def more(state):
        t, cs_max = state
        return jnp.logical_and(t < i, cs_max > UNDERFLOW_LOG)

    def body(state):
        t, _ = state
        return t + 1, tiles(i - 1 - t, False)

    lax.while_loop(more, body, (0, cs_max))
    gate = _silu(g_ref[...])
    for h in range(hps):
        sl = slice(h * HEAD_DIM, (h + 1) * HEAD_DIM)
        o_ref[:, sl] = (acc_ref[h] * gate[:, sl]).astype(o_ref.dtype)


def _sb_prefill(p2d, kv, nb, n, nh, q_col, g_col, k_head, v_head, tq):
    hps = HEADS_PER_STEP
    width = hps * HEAD_DIM
    nq = n // tq
    assert nh % hps == 0 and q_col % width == 0 and g_col % width == 0 and k_head % hps == 0 and v_head % hps == 0
    kern = functools.partial(_sb_prefill_kernel, tq=tq, scale=1.0 / math.sqrt(HEAD_DIM))
    qc, gc, kh, vh = q_col // width, g_col // width, k_head // hps, v_head // hps
    return pl.pallas_call(
        kern,
        grid=(nb, nh // hps, nq),
        in_specs=[
            pl.BlockSpec((tq, width), lambda b, h, i: (b * nq + i, qc + h)),
            pl.BlockSpec((None, hps, n, HEAD_DIM), lambda b, h, i: (b, kh + h, 0, 0)),
            pl.BlockSpec((None, hps, n, HEAD_DIM), lambda b, h, i: (b, vh + h, 0, 0)),
            pl.BlockSpec((tq, width), lambda b, h, i: (b * nq + i, gc + h)),
        ],
        out_specs=pl.BlockSpec((tq, width), lambda b, h, i: (b * nq + i, h)),
        out_shape=jax.ShapeDtypeStruct((nb * n, nh * HEAD_DIM), BF16),
        scratch_shapes=[pltpu.VMEM((hps, tq, HEAD_DIM), F32), pltpu.VMEM((hps * tq, LANES), F32)],
        compiler_params=_params(("parallel", "parallel", "arbitrary")),
        name="sb_prefill",
    )(p2d, kv, kv, p2d)


def _top_k_lanes(gate, k):
    lane = lax.broadcasted_iota(jnp.int32, gate.shape, 1).astype(F32)
    sel = jnp.zeros(gate.shape, F32)
    for _ in range(k):
        m = jnp.max(gate, axis=1, keepdims=True)
        cand = jnp.where(gate == m, lane, float(LANES))
        cand = jnp.where(m > NEG_INF, cand, float(LANES))
        idx = jnp.min(cand, axis=1, keepdims=True)
        pick = lane == idx
        sel = jnp.where(pick, 1.0, sel)
        gate = jnp.where(pick, NEG_INF, gate)
    return sel


def _gate_scores(qf, km):
    q_hi, q_lo = _split_bf16(qf)
    k_hi, k_lo = _split_bf16(km)
    return _dot_nt(q_hi, k_hi) + _dot_nt(q_hi, k_lo) + _dot_nt(q_lo, k_hi)


def _moba_prefill_kernel(l_ref, q_ref, k_ref, v_ref, g_ref, bias_ref, qg_ref, kg_ref, o_ref, kn_ref,
                         km_ref, m_ref, l_scr, acc_ref, *, nblk, scale):
    i = pl.program_id(2)
    hps = HEADS_PER_STEP
    blk = MOBA_BLOCK

    @pl.when(i == 0)
    def _():
        km_ref[...] = jnp.zeros_like(km_ref)
        for h in range(hps):
            for b in range(nblk):
                kn = _rms_norm(k_ref[h, b * blk:(b + 1) * blk, :], kg_ref[...])
                kn_ref[h, b * blk:(b + 1) * blk, :] = kn
                km_ref[h, b:b + 1, :] = jnp.sum(kn, axis=0, keepdims=True) * (1.0 / blk)

    lane = lax.broadcasted_iota(jnp.int32, (blk, LANES), 1)
    row = lax.broadcasted_iota(jnp.int32, (blk, blk), 0)
    col = lax.broadcasted_iota(jnp.int32, (blk, blk), 1)
    block_id = lax.broadcasted_iota(jnp.int32, (LANES, blk), 0)
    qs, masks = [], []
    for h in range(hps):
        qf = _rms_norm(q_ref[:, h * HEAD_DIM:(h + 1) * HEAD_DIM], qg_ref[...])
        gate = jnp.where(lane < i, _gate_scores(qf, km_ref[h]), NEG_INF)
        masks.append(jnp.where(_top_k_lanes(gate, MOBA_TOPK) > 0.0, 0.0, NEG_BIG))
        qs.append(qf.astype(BF16))
    not_chosen = jnp.concatenate(masks, axis=0).astype(BF16)

    m_ref[...] = jnp.full_like(m_ref, NEG_INF)
    l_scr[...] = jnp.zeros_like(l_scr)
    acc_ref[...] = jnp.zeros_like(acc_ref)

    def tiles(j, own):
        off = pl.multiple_of(j * blk, blk)
        s = jnp.concatenate([_dot_nt(qs[h], kn_ref[h, pl.ds(off, blk), :].astype(BF16)) * scale + bias_ref[h, i - j]
                             for h in range(hps)], axis=0)
        if own:
            s = jnp.where(jnp.concatenate([col <= row] * hps, axis=0), s, NEG_INF)
        else:
            s = s + _dot(not_chosen, jnp.where(block_id == j, 1.0, 0.0).astype(BF16))
        m_prev = m_ref[...]
        m_new = jnp.maximum(m_prev, jnp.max(s, axis=1, keepdims=True))
        alpha = jnp.exp(m_prev - m_new)
        p = jnp.exp(s - _lanes(m_new, blk))
        l_scr[...] = alpha * l_scr[...] + jnp.sum(p, axis=1, keepdims=True)
        for h in range(hps):
            rows = slice(h * blk, (h + 1) * blk)
            v = v_ref[h, pl.ds(off, blk), :].astype(BF16)
            acc_ref[h] = alpha[rows] * acc_ref[h] + _dot(p[rows].astype(BF16), v)
        m_ref[...] = m_new

    tiles(i, True)

    def body(j, carry):
        tiles(j, False)
        return carry

    lax.fori_loop(0, i, body, 0)
    gate = _silu(g_ref[...])
    inv_l = 1.0 / l_scr[...]
    for h in range(hps):
        sl = slice(h * HEAD_DIM, (h + 1) * HEAD_DIM)
        o_ref[:, sl] = (acc_ref[h] * inv_l[h * blk:(h + 1) * blk] * gate[:, sl]).astype(o_ref.dtype)


def _moba_prefill(lidx, p2d, kv, bias_tiles, q_gain, k_gain, nb, n, nh, q_col, g_col, k_head, v_head):
    hps = HEADS_PER_STEP
    width = hps * HEAD_DIM
    blk = MOBA_BLOCK
    assert n % blk == 0
    nq = n // blk
    assert nq <= LANES
    assert nh % hps == 0 and q_col % width == 0 and g_col % width == 0 and k_head % hps == 0 and v_head % hps == 0
    kern = functools.partial(_moba_prefill_kernel, nblk=nq, scale=1.0 / math.sqrt(HEAD_DIM))
    qc, gc, kh, vh = q_col // width, g_col // width, k_head // hps, v_head // hps
    gain = pl.BlockSpec((None, 1, HEAD_DIM), lambda b, h, i, l: (l[0], 0, 0))
    return pl.pallas_call(
        kern,
        grid_spec=pltpu.PrefetchScalarGridSpec(
            num_scalar_prefetch=1,
            grid=(nb, nh // hps, nq),
            in_specs=[
                pl.BlockSpec((blk, width), lambda b, h, i, l: (b * nq + i, qc + h)),
                pl.BlockSpec((None, hps, n, HEAD_DIM), lambda b, h, i, l: (b, kh + h, 0, 0)),
                pl.BlockSpec((None, hps, n, HEAD_DIM), lambda b, h, i, l: (b, vh + h, 0, 0)),
                pl.BlockSpec((blk, width), lambda b, h, i, l: (b * nq + i, gc + h)),
                pl.BlockSpec((hps, nq, blk, blk), lambda b, h, i, l: (h, 0, 0, 0)),
                gain, gain,
            ],
            out_specs=[
                pl.BlockSpec((blk, width), lambda b, h, i, l: (b * nq + i, h)),
                pl.BlockSpec((None, hps, n, HEAD_DIM), lambda b, h, i, l: (b, h, 0, 0)),
            ],
            scratch_shapes=[pltpu.VMEM((hps, LANES, HEAD_DIM), F32), pltpu.VMEM((hps * blk, LANES), F32),
                            pltpu.VMEM((hps * blk, LANES), F32), pltpu.VMEM((hps, blk, HEAD_DIM), F32)],
        ),
        out_shape=[jax.ShapeDtypeStruct((nb * n, nh * HEAD_DIM), BF16),
                   jax.ShapeDtypeStruct((nb, nh, n, HEAD_DIM), F32)],
        compiler_params=_params(("parallel", "parallel", "arbitrary")),
        name="moba_prefill",
    )(lidx, p2d, kv, kv, p2d, bias_tiles, q_gain, k_gain)


def _head_queries(q_ref, nh):
    return [q_ref[:, h * HEAD_DIM:(h + 1) * HEAD_DIM] for h in range(nh)]


def _stacked_scores(qs, k_ref):
    return jnp.concatenate([_dot_nt(q, k_ref[h].astype(BF16)) for h, q in enumerate(qs)], axis=0)


def _stacked_values(p, v_ref, nh, dn):
    return jnp.concatenate(
        [_dot(p[h * dn:(h + 1) * dn].astype(BF16), v_ref[h].astype(BF16)) for h in range(nh)], axis=0)


def _gated_heads(acc, g, nh, dn):
    return jnp.concatenate([acc[h * dn:(h + 1) * dn] for h in range(nh)], axis=1) * _silu(g)


def _page_index_map(r, pps, n_pages, reverse):
    def index_map(b, s, l, pt):
        page = s * pps + r
        if reverse:
            page = n_pages - 1 - page
        return (l[0], pt[b * n_pages + page], 0, 0, 0)

    return index_map


def _page_specs(pps, nh, n_pages, reverse):
    return [pl.BlockSpec((None, None, nh, PAGE_SIZE, HEAD_DIM), _page_index_map(r, pps, n_pages, reverse))
            for r in range(pps)]


def _sb_decode_kernel(l_ref, pt_ref, q_ref, kn_ref, vn_ref, g_ref, *rest, nh, dn, scale):
    pps = SB_PAGES_PER_STEP
    k_refs, v_refs = rest[:pps], rest[pps:2 * pps]
    o_ref, acc_ref, cs_ref = rest[2 * pps:]
    s_idx = pl.program_id(1)
    rows = nh * dn
    qs = [q.astype(BF16) for q in _head_queries(q_ref, nh)]
    row = lax.broadcasted_iota(jnp.int32, (PAGE_SIZE, PAGE_SIZE), 0)
    col = lax.broadcasted_iota(jnp.int32, (PAGE_SIZE, PAGE_SIZE), 1)
    later = jnp.where(row > col, 1.0, 0.0).astype(BF16)

    def pages(kp_refs, vp_refs, keep):
        z = jnp.concatenate([_stacked_scores(qs, kp) for kp in kp_refs], axis=0) * scale
        ls = _log_sigmoid(z)
        lk = ls - z
        if keep is not None:
            lk = jnp.where(keep, lk, 0.0)
        hi, lo = _split_bf16(lk)
        within = _dot(hi, later) + _dot(lo, later)
        tot = jnp.sum(lk, axis=1, keepdims=True)
        carry = cs_ref[...]
        carries = []
        for r in range(len(kp_refs)):
            carries.append(carry)
            carry = carry + tot[r * rows:(r + 1) * rows]
        cs_ref[...] = carry
        a = jnp.exp(ls + within + jnp.concatenate(carries, axis=0))
        if keep is not None:
            a = jnp.where(keep, a, 0.0)
        upd = _stacked_values(a[0:rows], vp_refs[0], nh, dn)
        for r in range(1, len(kp_refs)):
            upd = upd + _stacked_values(a[r * rows:(r + 1) * rows], vp_refs[r], nh, dn)
        acc_ref[...] += upd

    @pl.when(s_idx == 0)
    def _():
        acc_ref[...] = jnp.zeros_like(acc_ref)
        cs_ref[...] = jnp.zeros_like(cs_ref)
        tok = lax.broadcasted_iota(jnp.int32, (rows, PAGE_SIZE), 0) & (dn - 1)
        key = lax.broadcasted_iota(jnp.int32, (rows, PAGE_SIZE), 1)
        pages([kn_ref], [vn_ref], key < tok)

    @pl.when(jnp.max(cs_ref[...]) > UNDERFLOW_LOG)
    def _():
        pages(k_refs, v_refs, None)

    @pl.when(s_idx == pl.num_programs(1) - 1)
    def _():
        o_ref[...] = _gated_heads(acc_ref[...], g_ref[...], nh, dn).astype(o_ref.dtype)


def _sb_decode(lidx, pt_flat, q, k_new, v_new, g, cache_k, cache_v, nb, dn, nh, n_pages):
    pps = SB_PAGES_PER_STEP
    assert n_pages % pps == 0 and dn == SUBLANES
    width = nh * HEAD_DIM
    kern = functools.partial(_sb_decode_kernel, nh=nh, dn=dn, scale=1.0 / math.sqrt(HEAD_DIM))
    seq = lambda rows: pl.BlockSpec((None, rows, width), lambda b, s, l, pt: (b, 0, 0))
    new = pl.BlockSpec((None, nh, PAGE_SIZE, HEAD_DIM), lambda b, s, l, pt: (b, 0, 0, 0))
    pages = _page_specs(pps, nh, n_pages, True)
    return pl.pallas_call(
        kern,
        grid_spec=pltpu.PrefetchScalarGridSpec(
            num_scalar_prefetch=2,
            grid=(nb, n_pages // pps),
            in_specs=[seq(dn), new, new, seq(dn)] + pages + pages,
            out_specs=seq(dn),
            scratch_shapes=[pltpu.VMEM((nh * dn, HEAD_DIM), F32), pltpu.VMEM((nh * dn, LANES), F32)],
        ),
        out_shape=jax.ShapeDtypeStruct((nb, dn, width), F32),
        compiler_params=_params(("parallel", "arbitrary")),
        name="sb_decode",
    )(lidx, pt_flat, q, k_new, v_new, g, *([cache_k] * pps), *([cache_v] * pps))


def _moba_scores_kernel(l_ref, pt_ref, q_ref, *rest, nh, dn):
    pps = MOBA_PAGES_PER_STEP
    k_refs = rest[:pps]
    s_ref, km_ref = rest[pps:]
    s_idx = pl.program_id(1)
    qs = [q.astype(BF16) for q in _head_queries(q_ref, nh)]
    pages_per_block = MOBA_BLOCK // PAGE_SIZE
    blocks_per_step = pps // pages_per_block

    @pl.when(s_idx == 0)
    def _():
        km_ref[...] = jnp.zeros_like(km_ref)

    sub = lax.broadcasted_iota(jnp.int32, (blocks_per_step, HEAD_DIM), 0)
    for h in range(nh):
        means = jnp.zeros((blocks_per_step, HEAD_DIM), F32)
        for c in range(blocks_per_step):
            tot = jnp.zeros((1, HEAD_DIM), F32)
            for r in range(c * pages_per_block, (c + 1) * pages_per_block):
                kp = k_refs[r][h]
                s_ref[h * dn:(h + 1) * dn, r * PAGE_SIZE:(r + 1) * PAGE_SIZE] = _dot_nt(qs[h], kp.astype(BF16))
                tot = tot + jnp.sum(kp, axis=0, keepdims=True)
            means = jnp.where(sub == c, tot * (1.0 / MOBA_BLOCK), means)
        km_ref[h, pl.ds(pl.multiple_of(s_idx * blocks_per_step, blocks_per_step), blocks_per_step), :] = means


def _moba_scores(lidx, pt_flat, q, cache_k, nb, dn, nh, n_pages):
    pps = MOBA_PAGES_PER_STEP
    assert pps // (MOBA_BLOCK // PAGE_SIZE) == SUBLANES and n_pages % pps == 0
    width = nh * HEAD_DIM
    past = n_pages * PAGE_SIZE
    assert past // MOBA_BLOCK <= LANES
    kern = functools.partial(_moba_scores_kernel, nh=nh, dn=dn)
    return pl.pallas_call(
        kern,
        grid_spec=pltpu.PrefetchScalarGridSpec(
            num_scalar_prefetch=2,
            grid=(nb, n_pages // pps),
            in_specs=[pl.BlockSpec((None, dn, width), lambda b, s, l, pt: (b, 0, 0))]
            + _page_specs(pps, nh, n_pages, False),
            out_specs=[
                pl.BlockSpec((None, nh * dn, pps * PAGE_SIZE), lambda b, s, l, pt: (b, 0, s)),
                pl.BlockSpec((None, nh, LANES, HEAD_DIM), lambda b, s, l, pt: (b, 0, 0, 0)),
            ],
        ),
        out_shape=[jax.ShapeDtypeStruct((nb, nh * dn, past), F32),
                   jax.ShapeDtypeStruct((nb, nh, LANES, HEAD_DIM), F32)],
        compiler_params=_params(("parallel", "arbitrary")),
        name="moba_scores",
    )(lidx, pt_flat, q, *([cache_k] * pps))


def _moba_values_kernel(l_ref, pt_ref, q_ref, km_ref, s_ref, bias_ref, kn_ref, vn_ref, bown_ref, g_ref, *rest,
                        nh, dn, n_past_blocks, scale):
    pps = MOBA_PAGES_PER_STEP
    v_refs = rest[:pps]
    o_ref, sel_ref, m_ref, l_scr, acc_ref = rest[pps:]
    s_idx = pl.program_id(1)
    rows = nh * dn
    width = pps * PAGE_SIZE
    blocks_per_step = width // MOBA_BLOCK

    @pl.when(s_idx == 0)
    def _():
        qf = _head_queries(q_ref, nh)
        lane = lax.broadcasted_iota(jnp.int32, (rows, LANES), 1)
        gate = jnp.concatenate([_gate_scores(qf[h], km_ref[h]) for h in range(nh)], axis=0)
        gate = jnp.where(lane < n_past_blocks, gate, NEG_INF)
        sel_ref[...] = _top_k_lanes(gate, MOBA_TOPK)
        tok = lax.broadcasted_iota(jnp.int32, (rows, PAGE_SIZE), 0) & (dn - 1)
        key = lax.broadcasted_iota(jnp.int32, (rows, PAGE_SIZE), 1)
        s_own = _stacked_scores([q.astype(BF16) for q in qf], kn_ref) * scale + bown_ref[...]
        s_own = jnp.where(key <= tok, s_own, NEG_INF)
        m0 = jnp.max(s_own, axis=1, keepdims=True)
        p_own = jnp.exp(s_own - m0)
        m_ref[...] = jnp.broadcast_to(m0, m_ref.shape)
        l_scr[...] = jnp.broadcast_to(jnp.sum(p_own, axis=1, keepdims=True), l_scr.shape)
        acc_ref[...] = _stacked_values(p_own, vn_ref, nh, dn)

    key_in_step = lax.broadcasted_iota(jnp.int32, (LANES, width), 1)
    blk_of_key = lax.shift_right_logical(key_in_step, int(math.log2(MOBA_BLOCK))) + s_idx * blocks_per_step
    indicator = jnp.where(lax.broadcasted_iota(jnp.int32, (LANES, width), 0) == blk_of_key, 1.0, 0.0).astype(BF16)
    chosen = _dot(sel_ref[...].astype(BF16), indicator) > 0.5
    sm = jnp.where(chosen, s_ref[...] * scale + bias_ref[...], NEG_INF)
    m_prev = m_ref[...]
    m_new = jnp.maximum(m_prev, jnp.max(sm, axis=1, keepdims=True))
    alpha = jnp.exp(m_prev - m_new)
    p = jnp.exp(sm - _lanes(m_new, width))
    l_scr[...] = alpha * l_scr[...] + jnp.sum(p, axis=1, keepdims=True)
    upd = _stacked_values(p[:, 0:PAGE_SIZE], v_refs[0], nh, dn)
    for r in range(1, pps):
        upd = upd + _stacked_values(p[:, r * PAGE_SIZE:(r + 1) * PAGE_SIZE], v_refs[r], nh, dn)
    acc_ref[...] = alpha * acc_ref[...] + upd
    m_ref[...] = m_new

    @pl.when(s_idx == pl.num_programs(1) - 1)
    def _():
        o_ref[...] = _gated_heads(acc_ref[...] / l_scr[...], g_ref[...], nh, dn).astype(o_ref.dtype)


def _moba_values(lidx, pt_flat, q, km, scores, bias_past, k_new, v_new, bias_own, g, cache_v,
                 nb, dn, nh, n_pages):
    pps = MOBA_PAGES_PER_STEP
    width = nh * HEAD_DIM
    rows = nh * dn
    past = n_pages * PAGE_SIZE
    n_past_blocks = past // MOBA_BLOCK
    assert n_past_blocks <= LANES and km.shape[2] == LANES and n_pages % pps == 0
    assert (pps * PAGE_SIZE) % MOBA_BLOCK == 0
    kern = functools.partial(_moba_values_kernel, nh=nh, dn=dn, n_past_blocks=n_past_blocks,
                             scale=1.0 / math.sqrt(HEAD_DIM))
    seq = lambda r, c: pl.BlockSpec((None, r, c), lambda b, s, l, pt: (b, 0, 0))
    per_head = lambda r: pl.BlockSpec((None, nh, r, HEAD_DIM), lambda b, s, l, pt: (b, 0, 0, 0))
    return pl.pallas_call(
        kern,
        grid_spec=pltpu.PrefetchScalarGridSpec(
            num_scalar_prefetch=2,
            grid=(nb, n_pages // pps),
            in_specs=[seq(dn, width), per_head(LANES),
                      pl.BlockSpec((None, rows, pps * PAGE_SIZE), lambda b, s, l, pt: (b, 0, s)),
                      pl.BlockSpec((rows, pps * PAGE_SIZE), lambda b, s, l, pt: (0, s)),
                      per_head(PAGE_SIZE), per_head(PAGE_SIZE),
                      pl.BlockSpec((rows, PAGE_SIZE), lambda b, s, l, pt: (0, 0)),
                      seq(dn, width)] + _page_specs(pps, nh, n_pages, False),
            out_specs=seq(dn, width),
            scratch_shapes=[pltpu.VMEM((rows, LANES), F32), pltpu.VMEM((rows, LANES), F32),
                            pltpu.VMEM((rows, LANES), F32), pltpu.VMEM((rows, HEAD_DIM), F32)],
        ),
        out_shape=jax.ShapeDtypeStruct((nb, dn, width), F32),
        compiler_params=_params(("parallel", "arbitrary")),
        name="moba_values",
    )(lidx, pt_flat, q, km, scores, bias_past, k_new, v_new, bias_own, g, *([cache_v] * pps))


def _out_proj_kernel(l_ref, x_ref, a_ref, b_ref, c_ref, wa_ref, wb_ref, wc_ref, o_ref):
    acc = _dot(a_ref[...].astype(BF16), wa_ref[...])
    acc += _dot(b_ref[...].astype(BF16), wb_ref[...])
    acc += _dot(c_ref[...].astype(BF16), wc_ref[...])
    o_ref[...] = x_ref[...] + acc


def _out_proj(lidx, x2d, a, b, c, wa, wb, wc, tm):
    T, d_model = x2d.shape
    assert T % tm == 0
    act = lambda arr: pl.BlockSpec((tm, arr.shape[1]), lambda i, l: (i, 0))
    wgt = lambda arr: pl.BlockSpec((None,) + arr.shape[1:], lambda i, l: (l[0], 0, 0))
    return pl.pallas_call(
        _out_proj_kernel,
        grid_spec=pltpu.PrefetchScalarGridSpec(
            num_scalar_prefetch=1,
            grid=(T // tm,),
            in_specs=[act(x2d), act(a), act(b), act(c), wgt(wa), wgt(wb), wgt(wc)],
            out_specs=pl.BlockSpec((tm, d_model), lambda i, l: (i, 0)),
        ),
        out_shape=jax.ShapeDtypeStruct((T, d_model), F32),
        input_output_aliases={1: 0},
        compiler_params=_params(("parallel",)),
        name="out_proj",
    )(lidx, x2d, a, b, c, wa, wb, wc)


def _rel_bucket_np(n):
    n = np.maximum(n, 0)
    max_exact = REL_BUCKETS // 2
    nf = np.maximum(n, 1).astype(np.float64)
    large = max_exact + (np.log(nf / max_exact) / math.log(REL_MAX_DIST / max_exact)
                         * (REL_BUCKETS - max_exact)).astype(np.int32)
    large = np.minimum(large, REL_BUCKETS - 1)
    return np.where(n < max_exact, n, large).astype(np.int32)


def _toeplitz_bias_tiles(bias_by_dist, nq):
    blk = MOBA_BLOCK
    nh = bias_by_dist.shape[0]
    m = np.arange(2 * blk)
    d = np.arange(nq)[:, None] * blk
    gen_idx = np.where(m[None, :] <= blk, np.maximum(d - m[None, :], 0), d + 2 * blk - m[None, :])
    gen = bias_by_dist[:, gen_idx]
    wide = jnp.broadcast_to(gen[:, :, None, :], (nh, nq, blk, 2 * blk)).reshape(nh, nq, blk * 2 * blk)
    return wide[:, :, :blk * (2 * blk - 1)].reshape(nh, nq, blk, 2 * blk - 1)[:, :, :, :blk]


def _pool_halo(u3, hist, tm):
    nb, n, d = u3.shape
    nt = n // tm
    first = jnp.pad(hist, ((0, 0), (HALO - POOL_HIST, 0), (0, 0)))[:, None]
    if nt == 1:
        return first
    tails = u3.reshape(nb, nt, tm, d)[:, :-1, tm - HALO:, :]
    return jnp.concatenate([first, tails], axis=1)


def kernel(x_prompt, x_sample, cache_sb_k, cache_sb_v, cache_moba_k, cache_moba_v, state_pool, page_table,
           norm_g, w_in, w_pool, pool_scale, q_norm_g, k_norm_g, w_out, rel_bias):
    nb, n, d_model = x_prompt.shape
    db, dn, _ = x_sample.shape
    depth, n_phys, page_size, nh, hd = cache_sb_k.shape
    n_pages = page_table.shape[1]
    past = n_pages * page_size
    assert page_size == PAGE_SIZE and hd == HEAD_DIM
    d_pool = POOL_GROUP * len(POOL_WINDOWS)
    d_att = nh * HEAD_DIM
    d_proj = w_in.shape[-1]
    assert d_proj == 2 * d_pool + 8 * d_att
    seg = lambda t: slice(2 * d_pool + t * d_att, 2 * d_pool + (t + 1) * d_att)
    w_in_bf = w_in.astype(BF16)
    w_main = jnp.concatenate([w_in_bf[..., :2 * d_pool]] + [w_in_bf[..., seg(t)] for t in (0, 3, 4, 7)], axis=-1)
    w_kv = jnp.concatenate([w_in_bf[..., seg(t)] for t in (1, 2, 5, 6)], axis=-1)
    m_qb, m_gb, m_qc, m_gc = (2 * d_pool + t * d_att for t in range(4))
    kv_kb, kv_vb, kv_kc, kv_vc = (t * nh for t in range(4))
    zeros_att = jnp.zeros((depth, d_att), F32)
    g_main = jnp.concatenate([jnp.zeros((depth, m_qc), F32), jnp.tile(q_norm_g, (1, nh)), zeros_att], axis=-1)[:, None]
    g_kv = jnp.concatenate([zeros_att, zeros_att, jnp.tile(k_norm_g, (1, nh)), zeros_att], axis=-1)[:, None]
    norm_main = (m_qc, m_qc + d_att)
    norm_kv = (kv_kc * HEAD_DIM, (kv_kc + nh) * HEAD_DIM)
    q_gain, k_gain = q_norm_g[:, None, :], k_norm_g[:, None, :]

    w_out_bf = w_out.astype(BF16)
    wa, wb, wc = w_out_bf[:, :d_pool], w_out_bf[:, d_pool:d_pool + d_att], w_out_bf[:, d_pool + d_att:]
    norm_g3 = norm_g[:, None, :]
    pool_scale3 = pool_scale[:, None, :]
    caches = [jnp.transpose(c, (0, 1, 3, 2, 4)) for c in (cache_sb_k, cache_sb_v, cache_moba_k, cache_moba_v)]
    pt_flat = page_table.reshape(-1).astype(jnp.int32)

    n_dist = past + dn + 1
    bias_by_dist = rel_bias.astype(F32).T[:, _rel_bucket_np(np.arange(n_dist))]
    bias_tiles = _toeplitz_bias_tiles(bias_by_dist, n // MOBA_BLOCK)
    rev = bias_by_dist[:, ::-1]
    bias_past = jnp.stack([rev[:, n_dist - 1 - past - t:n_dist - 1 - t] for t in range(dn)], axis=1)
    bias_past = bias_past.reshape(nh * dn, past)
    dist_own = np.maximum(np.arange(dn)[:, None] - np.arange(PAGE_SIZE)[None, :], 0)
    bias_own = bias_by_dist[:, dist_own].reshape(nh * dn, PAGE_SIZE)

    tm_p = 512
    hist0 = jnp.zeros((nb, POOL_HIST, d_pool), F32)

    def layer(carry, l):
        yp, ys = carry
        lidx = jnp.reshape(l, (1,)).astype(jnp.int32)

        p = _in_proj(lidx, yp, norm_g3, w_main, nb, n, tm_p, 1024, False)
        kv = _in_proj(lidx, yp, norm_g3, w_kv, nb, n, tm_p, d_att, True)
        u3 = p.reshape(nb, n, -1)[:, :, :d_pool]
        a = _pool(lidx, p, _pool_halo(u3, hist0, tm_p), w_pool, pool_scale3, nb, n, tm_p, 0, BF16)
        bmix = _sb_prefill(p, kv, nb, n, nh, m_qb, m_gb, kv_kb, kv_vb, 256)
        cmix, kc_norm = _moba_prefill(lidx, p, kv, bias_tiles, q_gain, k_gain, nb, n, nh, m_qc, m_gc, kv_kc, kv_vc)
        yp_new = _out_proj(lidx, yp, a, bmix, cmix, wa, wb, wc, tm_p)
        outs_p = (kv[:, kv_kb:kv_kb + nh], kv[:, kv_vb:kv_vb + nh], kc_norm, kv[:, kv_vc:kv_vc + nh],
                  u3[:, n - POOL_HIST:])

        ts = db * dn
        ps = _in_proj(lidx, ys, norm_g3, w_main, 1, ts, ts, 1024, False, g_main, norm_main)
        kvs = _in_proj(lidx, ys, norm_g3, w_kv, 1, ts, ts, d_att, False, g_kv, norm_kv)
        ps3 = ps.reshape(db, dn, -1)
        kvs5 = jnp.transpose(kvs.reshape(db, dn, 4, nh, HEAD_DIM), (2, 0, 3, 1, 4))
        hist_s = lax.dynamic_index_in_dim(state_pool, l, axis=0, keepdims=False)
        us3 = ps3[:, :, :d_pool]
        a_s = _pool(lidx, ps, _pool_halo(us3, hist_s, dn), w_pool, pool_scale3, db, dn, dn, past, F32)
        col = lambda c: ps3[:, :, c:c + d_att]
        new_page = lambda t: jnp.pad(kvs5[t], ((0, 0), (0, 0), (0, PAGE_SIZE - dn), (0, 0)))
        b_s = _sb_decode(lidx, pt_flat, col(m_qb), new_page(0), new_page(1), col(m_gb),
                         caches[0], caches[1], db, dn, nh, n_pages)
        scores, km = _moba_scores(lidx, pt_flat, col(m_qc), caches[2], db, dn, nh, n_pages)
        c_s = _moba_values(lidx, pt_flat, col(m_qc), km, scores, bias_past, new_page(2), new_page(3),
                           bias_own, col(m_gc), caches[3], db, dn, nh, n_pages)
        ys_new = _out_proj(lidx, ys, a_s, b_s.reshape(ts, d_att), c_s.reshape(ts, d_att), wa, wb, wc, ts)
        hist_new = jnp.concatenate([hist_s, us3], axis=1)[:, -POOL_HIST:]
        outs_s = tuple(kvs5[t] for t in range(4)) + (hist_new,)
        return (yp_new, ys_new), (outs_p, outs_s)

    (yp, ys), (outs_p, outs_s) = lax.scan(
        layer, (x_prompt.reshape(nb * n, d_model), x_sample.reshape(db * dn, d_model)), jnp.arange(depth))
    rows = lambda t: jnp.transpose(t, (0, 1, 3, 2, 4))
    outs_p = tuple(rows(t) for t in outs_p[:4]) + (outs_p[4],)
    outs_s = tuple(rows(t) for t in outs_s[:4]) + (outs_s[4],)
    return (yp.reshape(nb, n, d_model), ys.reshape(db, dn, d_model)) + outs_p + outs_s
```

```python
import functools
import math

import numpy as np
import jax
import jax.numpy as jnp
from jax import lax
from jax.experimental import pallas as pl
from jax.experimental.pallas import tpu as pltpu

F32 = jnp.float32
BF16 = jnp.bfloat16

HEAD_DIM = 128
POOL_WINDOWS = (2, 4, 8, 16)
POOL_GROUP = 128
POOL_HIST = max(POOL_WINDOWS) - 1
HALO = POOL_HIST + 1
PAGE_SIZE = 128
MOBA_BLOCK = 256
MOBA_TOPK = 3
REL_BUCKETS = 32
REL_MAX_DIST = 128
EPS = 1e-6
LANES = 128
SUBLANES = 8
VMEM_LIMIT_BYTES = 48 * 1024 * 1024
IN_PROJ_ROWS = 1024
IN_PROJ_COLS = 1024
ROW_TILE = 512
SB_TILE = 256
HEADS_PER_STEP = 2
SB_PAGES_PER_CHUNK = 4
MOBA_PAGES_PER_STEP = 16
NEG_INF = float("-inf")
NEG_BIG = -1e30
UNDERFLOW_LOG = -104.0
LOG2E = 1.4426950408889634
LN2 = 0.6931471805599453

NT_DIMS = (((1,), (1,)), ((), ()))


def _dot(a, b):
    return jnp.dot(a, b, preferred_element_type=F32)


def _dot_nt(a, b):
    return lax.dot_general(a, b, NT_DIMS, preferred_element_type=F32)


def _split_bf16(x):
    hi = x.astype(BF16)
    lo = (x - hi.astype(F32)).astype(BF16)
    return hi, lo


def _log_sigmoid(z):
    return jnp.minimum(z, 0.0) - jnp.log(1.0 + jnp.exp(-jnp.abs(z)))


def _silu(g):
    return g / (1.0 + jnp.exp(-g))


def _rms_norm(x, gain):
    ms = jnp.mean(x * x, axis=-1, keepdims=True)
    return x * lax.rsqrt(ms + EPS) * gain


def _lanes(x, width):
    return x if width == LANES else jnp.concatenate([x] * (width // LANES), axis=1)


def _params(semantics):
    return pltpu.CompilerParams(dimension_semantics=semantics, vmem_limit_bytes=VMEM_LIMIT_BYTES)


def _in_proj_kernel(l_ref, x_ref, g_ref, w_ref, *rest, norm_heads, head_major):
    if norm_heads is None:
        o_ref, h_ref = rest
    else:
        qkg_ref, o_ref, h_ref = rest
    j = pl.program_id(2)

    @pl.when(j == 0)
    def _():
        h_ref[...] = _rms_norm(x_ref[...], g_ref[...]).astype(BF16)

    acc = _dot(h_ref[...], w_ref[...])
    heads_per_tile = acc.shape[1] // HEAD_DIM

    def store(s, val):
        if head_major:
            o_ref[s] = val
        else:
            o_ref[:, s * HEAD_DIM:(s + 1) * HEAD_DIM] = val

    if norm_heads is None:
        if head_major:
            for s in range(heads_per_tile):
                store(s, acc[:, s * HEAD_DIM:(s + 1) * HEAD_DIM])
        else:
            o_ref[...] = acc
        return

    for s in range(heads_per_tile):
        sl = slice(s * HEAD_DIM, (s + 1) * HEAD_DIM)
        head = j * heads_per_tile + s
        needs_norm = jnp.logical_and(head >= norm_heads[0], head < norm_heads[1])

        @pl.when(needs_norm)
        def _(s=s, sl=sl):
            store(s, _rms_norm(acc[:, sl], qkg_ref[:, sl]))

        @pl.when(jnp.logical_not(needs_norm))
        def _(s=s, sl=sl):
            store(s, acc[:, sl])


def _in_proj(lidx, x2d, norm_g, w_bf, nb, n, tm, tn, head_major, qkg=None, norm_cols=None):
    T, d_model = x2d.shape
    cols = w_bf.shape[-1]
    assert T == nb * n and n % tm == 0 and cols % tn == 0 and tn % HEAD_DIM == 0
    nt = n // tm
    norm_heads = None if qkg is None else (norm_cols[0] // HEAD_DIM, norm_cols[1] // HEAD_DIM)
    kern = functools.partial(_in_proj_kernel, norm_heads=norm_heads, head_major=head_major)
    if head_major:
        out_spec = pl.BlockSpec((None, tn // HEAD_DIM, tm, HEAD_DIM), lambda b, i, j, l: (b, j, i, 0))
        out_shape = jax.ShapeDtypeStruct((nb, cols // HEAD_DIM, n, HEAD_DIM), F32)
    else:
        out_spec = pl.BlockSpec((tm, tn), lambda b, i, j, l: (b * nt + i, j))
        out_shape = jax.ShapeDtypeStruct((T, cols), F32)
    in_specs = [
        pl.BlockSpec((tm, d_model), lambda b, i, j, l: (b * nt + i, 0)),
        pl.BlockSpec((None, 1, d_model), lambda b, i, j, l: (l[0], 0, 0)),
        pl.BlockSpec((None, d_model, tn), lambda b, i, j, l: (l[0], 0, j)),
    ]
    args = [lidx, x2d, norm_g, w_bf]
    if qkg is not None:
        in_specs.append(pl.BlockSpec((None, 1, tn), lambda b, i, j, l: (l[0], 0, j)))
        args.append(qkg)
    return pl.pallas_call(
        kern,
        grid_spec=pltpu.PrefetchScalarGridSpec(
            num_scalar_prefetch=1,
            grid=(nb, nt, cols // tn),
            in_specs=in_specs,
            out_specs=out_spec,
            scratch_shapes=[pltpu.VMEM((tm, d_model), BF16)],
        ),
        out_shape=out_shape,
        compiler_params=_params(("parallel", "parallel", "arbitrary")),
        name="in_proj_heads" if head_major else "in_proj",
    )(*args)


def _pool_kernel(l_ref, u_ref, ga_ref, halo_ref, wp_ref, ps_ref, o_ref, ext_ref, *, tm, pos0):
    i = pl.program_id(1)
    ext_ref[0:HALO, :] = halo_ref[...]
    ext_ref[HALO:, :] = u_ref[...]
    pos = pos0 + i * tm + lax.broadcasted_iota(jnp.int32, (tm, 1), 0)
    for g, w in enumerate(POOL_WINDOWS):
        sl = slice(g * POOL_GROUP, (g + 1) * POOL_GROUP)
        s = ext_ref[HALO:HALO + tm, sl]
        for d in range(1, w):
            s = s + ext_ref[HALO - d:HALO - d + tm, sl]
        cnt = jnp.minimum(w, pos + 1).astype(F32)
        pooled = s / cnt - u_ref[:, sl]
        y = _dot(pooled.astype(BF16), wp_ref[g].astype(BF16)) * ps_ref[:, sl]
        o_ref[:, sl] = (y * _silu(ga_ref[:, sl])).astype(o_ref.dtype)


def _pool(lidx, p2d, halo, w_pool, pool_scale, nb, n, tm, pos0, out_dtype):
    d_pool = POOL_GROUP * len(POOL_WINDOWS)
    nt = n // tm
    kern = functools.partial(_pool_kernel, tm=tm, pos0=pos0)
    return pl.pallas_call(
        kern,
        grid_spec=pltpu.PrefetchScalarGridSpec(
            num_scalar_prefetch=1,
            grid=(nb, nt),
            in_specs=[
                pl.BlockSpec((tm, d_pool), lambda b, i, l: (b * nt + i, 0)),
                pl.BlockSpec((tm, d_pool), lambda b, i, l: (b * nt + i, 1)),
                pl.BlockSpec((None, None, HALO, d_pool), lambda b, i, l: (b, i, 0, 0)),
                pl.BlockSpec((None, len(POOL_WINDOWS), POOL_GROUP, POOL_GROUP), lambda b, i, l: (l[0], 0, 0, 0)),
                pl.BlockSpec((None, 1, d_pool), lambda b, i, l: (l[0], 0, 0)),
            ],
            out_specs=pl.BlockSpec((tm, d_pool), lambda b, i, l: (b * nt + i, 0)),
            scratch_shapes=[pltpu.VMEM((HALO + tm, d_pool), F32)],
        ),
        out_shape=jax.ShapeDtypeStruct((nb * n, d_pool), out_dtype),
        compiler_params=_params(("parallel", "arbitrary")),
        name="pool",
    )(lidx, p2d, p2d, halo, w_pool, pool_scale)


def _sb_prefill_kernel(q_ref, k_ref, v_ref, g_ref, o_ref, acc_ref, cs_ref, *, tq, scale):
    i = pl.program_id(2)
    hps = HEADS_PER_STEP
    qs = [q_ref[:, h * HEAD_DIM:(h + 1) * HEAD_DIM].astype(BF16) for h in range(hps)]
    row = lax.broadcasted_iota(jnp.int32, (tq, tq), 0)
    col = lax.broadcasted_iota(jnp.int32, (tq, tq), 1)
    later = jnp.where(row > col, 1.0, 0.0).astype(BF16)

    def tiles(kj, diagonal):
        off = pl.multiple_of(kj * tq, tq)
        z = jnp.concatenate([_dot_nt(qs[h], k_ref[h, pl.ds(off, tq), :].astype(BF16)) for h in range(hps)],
                            axis=0) * scale
        ls = _log_sigmoid(z)
        lk = ls - z
        if diagonal:
            keep = jnp.concatenate([col < row] * hps, axis=0)
            lk = jnp.where(keep, lk, 0.0)
        hi, lo = _split_bf16(lk)
        la = _dot(hi, later) + _dot(lo, later) + _lanes(cs_ref[...], tq)
        a = jnp.exp(ls + la)
        if diagonal:
            a = jnp.where(keep, a, 0.0)
        for h in range(hps):
            v = v_ref[h, pl.ds(off, tq), :].astype(BF16)
            acc_ref[h] += _dot(a[h * tq:(h + 1) * tq].astype(BF16), v)
        cs = cs_ref[...] + jnp.sum(lk, axis=1, keepdims=True)
        cs_ref[...] = cs
        return jnp.max(cs)

    acc_ref[...] = jnp.zeros_like(acc_ref)
    cs_ref[...] = jnp.zeros_like(cs_ref)
    cs_max = tiles(i, True)

    def more(state):
        t, cs_max = state
        return jnp.logical_and(t < i, cs_max > UNDERFLOW_LOG)

    def body(state):
        t, _ = state
        return t + 1, tiles(i - 1 - t, False)

    lax.while_loop(more, body, (0, cs_max))
    gate = _silu(g_ref[...])
    for h in range(hps):
        sl = slice(h * HEAD_DIM, (h + 1) * HEAD_DIM)
        o_ref[:, sl] = (acc_ref[h] * gate[:, sl]).astype(o_ref.dtype)


def _sb_prefill(p2d, kv, nb, n, nh, q_col, g_col, k_head, v_head, tq):
    hps = HEADS_PER_STEP
    width = hps * HEAD_DIM
    nq = n // tq
    assert nh % hps == 0 and q_col % width == 0 and g_col % width == 0 and k_head % hps == 0 and v_head % hps == 0
    kern = functools.partial(_sb_prefill_kernel, tq=tq, scale=1.0 / math.sqrt(HEAD_DIM))
    qc, gc, kh, vh = q_col // width, g_col // width, k_head // hps, v_head // hps
    return pl.pallas_call(
        kern,
        grid=(nb, nh // hps, nq),
        in_specs=[
            pl.BlockSpec((tq, width), lambda b, h, i: (b * nq + i, qc + h)),
            pl.BlockSpec((None, hps, n, HEAD_DIM), lambda b, h, i: (b, kh + h, 0, 0)),
            pl.BlockSpec((None, hps, n, HEAD_DIM), lambda b, h, i: (b, vh + h, 0, 0)),
            pl.BlockSpec((tq, width), lambda b, h, i: (b * nq + i, gc + h)),
        ],
        out_specs=pl.BlockSpec((tq, width), lambda b, h, i: (b * nq + i, h)),
        out_shape=jax.ShapeDtypeStruct((nb * n, nh * HEAD_DIM), BF16),
        scratch_shapes=[pltpu.VMEM((hps, tq, HEAD_DIM), F32), pltpu.VMEM((hps * tq, LANES), F32)],
        compiler_params=_params(("parallel", "parallel", "arbitrary")),
        name="sb_prefill",
    )(p2d, kv, kv, p2d)


def _top_k_lanes(gate, k):
    lane = lax.broadcasted_iota(jnp.int32, gate.shape, 1).astype(F32)
    sel = jnp.zeros(gate.shape, F32)
    for _ in range(k):
        m = jnp.max(gate, axis=1, keepdims=True)
        cand = jnp.where(gate == m, lane, float(LANES))
        cand = jnp.where(m > NEG_INF, cand, float(LANES))
        idx = jnp.min(cand, axis=1, keepdims=True)
        pick = lane == idx
        sel = jnp.where(pick, 1.0, sel)
        gate = jnp.where(pick, NEG_INF, gate)
    return sel


def _gate_scores(qf, km):
    q_hi, q_lo = _split_bf16(qf)
    k_hi, k_lo = _split_bf16(km)
    return _dot_nt(q_hi, k_hi) + _dot_nt(q_hi, k_lo) + _dot_nt(q_lo, k_hi)


def _moba_prefill_kernel(l_ref, q_ref, k_ref, v_ref, g_ref, bias_ref, qg_ref, kg_ref, o_ref, kn_ref,
                         km_ref, m_ref, l_scr, acc_ref, *, nblk, scale):
    i = pl.program_id(2)
    hps = HEADS_PER_STEP
    blk = MOBA_BLOCK

    @pl.when(i == 0)
    def _():
        km_ref[...] = jnp.zeros_like(km_ref)
        for h in range(hps):
            for b in range(nblk):
                kn = _rms_norm(k_ref[h, b * blk:(b + 1) * blk, :], kg_ref[...])
                kn_ref[h, b * blk:(b + 1) * blk, :] = kn
                km_ref[h, b:b + 1, :] = jnp.sum(kn, axis=0, keepdims=True) * (1.0 / blk)

    lane = lax.broadcasted_iota(jnp.int32, (blk, LANES), 1)
    row = lax.broadcasted_iota(jnp.int32, (blk, blk), 0)
    col = lax.broadcasted_iota(jnp.int32, (blk, blk), 1)
    block_id = lax.broadcasted_iota(jnp.int32, (LANES, blk), 0)
    qs, masks = [], []
    for h in range(hps):
        qf = _rms_norm(q_ref[:, h * HEAD_DIM:(h + 1) * HEAD_DIM], qg_ref[...])
        gate = jnp.where(lane < i, _gate_scores(qf, km_ref[h]), NEG_INF)
        masks.append(jnp.where(_top_k_lanes(gate, MOBA_TOPK) > 0.0, 0.0, NEG_BIG))
        qs.append(qf.astype(BF16))
    not_chosen = jnp.concatenate(masks, axis=0).astype(BF16)

    m_ref[...] = jnp.full_like(m_ref, NEG_INF)
    l_scr[...] = jnp.zeros_like(l_scr)
    acc_ref[...] = jnp.zeros_like(acc_ref)

    def scores(j, own):
        off = pl.multiple_of(j * blk, blk)
        s = jnp.concatenate([_dot_nt(qs[h], kn_ref[h, pl.ds(off, blk), :].astype(BF16)) * scale + bias_ref[h, i - j]
                             for h in range(hps)], axis=0)
        if own:
            return jnp.where(jnp.concatenate([col <= row] * hps, axis=0), s, NEG_INF)
        return s + _dot(not_chosen, jnp.where(block_id == j, 1.0, 0.0).astype(BF16))

    def accumulate(s, j):
        off = pl.multiple_of(j * blk, blk)
        m_prev = m_ref[...]
        m_new = jnp.maximum(m_prev, jnp.max(s, axis=1, keepdims=True))
        alpha = jnp.exp(m_prev - m_new)
        p = jnp.exp(s - _lanes(m_new, blk))
        l_scr[...] = alpha * l_scr[...] + jnp.sum(p, axis=1, keepdims=True)
        for h in range(hps):
            rows = slice(h * blk, (h + 1) * blk)
            v = v_ref[h, pl.ds(off, blk), :].astype(BF16)
            acc_ref[h] = alpha[rows] * acc_ref[h] + _dot(p[rows].astype(BF16), v)
        m_ref[...] = m_new

    accumulate(scores(i, True), i)

    def body(j, s):
        s_next = scores(jnp.minimum(j + 1, jnp.maximum(i - 1, 0)), False)
        accumulate(s, j)
        return s_next

    lax.fori_loop(0, i, body, scores(0, False))
    gate = _silu(g_ref[...])
    inv_l = 1.0 / l_scr[...]
    for h in range(hps):
        sl = slice(h * HEAD_DIM, (h + 1) * HEAD_DIM)
        o_ref[:, sl] = (acc_ref[h] * inv_l[h * blk:(h + 1) * blk] * gate[:, sl]).astype(o_ref.dtype)


def _moba_prefill(lidx, p2d, kv, bias_tiles, q_gain, k_gain, nb, n, nh, q_col, g_col, k_head, v_head):
    hps = HEADS_PER_STEP
    width = hps * HEAD_DIM
    blk = MOBA_BLOCK
    assert n % blk == 0
    nq = n // blk
    assert nq <= LANES
    assert nh % hps == 0 and q_col % width == 0 and g_col % width == 0 and k_head % hps == 0 and v_head % hps == 0
    kern = functools.partial(_moba_prefill_kernel, nblk=nq, scale=1.0 / math.sqrt(HEAD_DIM))
    qc, gc, kh, vh = q_col // width, g_col // width, k_head // hps, v_head // hps
    gain = pl.BlockSpec((None, 1, HEAD_DIM), lambda b, h, i, l: (l[0], 0, 0))
    return pl.pallas_call(
        kern,
        grid_spec=pltpu.PrefetchScalarGridSpec(
            num_scalar_prefetch=1,
            grid=(nb, nh // hps, nq),
            in_specs=[
                pl.BlockSpec((blk, width), lambda b, h, i, l: (b * nq + i, qc + h)),
                pl.BlockSpec((None, hps, n, HEAD_DIM), lambda b, h, i, l: (b, kh + h, 0, 0)),
                pl.BlockSpec((None, hps, n, HEAD_DIM), lambda b, h, i, l: (b, vh + h, 0, 0)),
                pl.BlockSpec((blk, width), lambda b, h, i, l: (b * nq + i, gc + h)),
                pl.BlockSpec((hps, nq, blk, blk), lambda b, h, i, l: (h, 0, 0, 0)),
                gain, gain,
            ],
            out_specs=[
                pl.BlockSpec((blk, width), lambda b, h, i, l: (b * nq + i, h)),
                pl.BlockSpec((None, hps, n, HEAD_DIM), lambda b, h, i, l: (b, h, 0, 0)),
            ],
            scratch_shapes=[pltpu.VMEM((hps, LANES, HEAD_DIM), F32), pltpu.VMEM((hps * blk, LANES), F32),
                            pltpu.VMEM((hps * blk, LANES), F32), pltpu.VMEM((hps, blk, HEAD_DIM), F32)],
        ),
        out_shape=[jax.ShapeDtypeStruct((nb * n, nh * HEAD_DIM), BF16),
                   jax.ShapeDtypeStruct((nb, nh, n, HEAD_DIM), F32)],
        compiler_params=_params(("parallel", "parallel", "arbitrary")),
        name="moba_prefill",
    )(lidx, p2d, kv, kv, p2d, bias_tiles, q_gain, k_gain)


def _head_queries(q_ref, nh):
    return [q_ref[:, h * HEAD_DIM:(h + 1) * HEAD_DIM] for h in range(nh)]


def _stacked_scores(qs, k_ref):
    return jnp.concatenate([_dot_nt(q, k_ref[h].astype(BF16)) for h, q in enumerate(qs)], axis=0)


def _stacked_values(p, v_ref, nh, dn):
    return jnp.concatenate(
        [_dot(p[h * dn:(h + 1) * dn].astype(BF16), v_ref[h].astype(BF16)) for h in range(nh)], axis=0)


def _gated_heads(acc, g, nh, dn):
    return jnp.concatenate([acc[h * dn:(h + 1) * dn] for h in range(nh)], axis=1) * _silu(g)


def _page_index_map(r, pps, n_pages, reverse):
    def index_map(b, s, l, pt):
        page = s * pps + r
        if reverse:
            page = n_pages - 1 - page
        return (l[0], pt[b * n_pages + page], 0, 0, 0)

    return index_map


def _page_specs(pps, nh, n_pages, reverse):
    return [pl.BlockSpec((None, None, nh, PAGE_SIZE, HEAD_DIM), _page_index_map(r, pps, n_pages, reverse))
            for r in range(pps)]


def _sb_decode_kernel(l_ref, pt_ref, q_ref, kn_ref, vn_ref, g_ref, ck_hbm, cv_hbm, o_ref,
                      kbuf, vbuf, sems, acc_ref, cs_ref, *, nh, dn, n_pages, scale):
    ppc = SB_PAGES_PER_CHUNK
    n_chunks = n_pages // ppc
    b = pl.program_id(0)
    rows = nh * dn

    def page_copies(c, slot):
        copies = []
        for r in range(ppc):
            page = pt_ref[b * n_pages + (n_pages - 1 - (c * ppc + r))]
            copies.append(pltpu.make_async_copy(ck_hbm.at[l_ref[0], page], kbuf.at[slot, r], sems.at[0, slot]))
            copies.append(pltpu.make_async_copy(cv_hbm.at[l_ref[0], page], vbuf.at[slot, r], sems.at[1, slot]))
        return copies

    qs = [q.astype(BF16) for q in _head_queries(q_ref, nh)]
    row = lax.broadcasted_iota(jnp.int32, (PAGE_SIZE, PAGE_SIZE), 0)
    col = lax.broadcasted_iota(jnp.int32, (PAGE_SIZE, PAGE_SIZE), 1)
    later = jnp.where(row > col, 1.0, 0.0).astype(BF16)

    def pages(kp_refs, vp_refs, keep):
        z = jnp.concatenate([_stacked_scores(qs, kp) for kp in kp_refs], axis=0) * scale
        ls = _log_sigmoid(z)
        lk = ls - z
        if keep is not None:
            lk = jnp.where(keep, lk, 0.0)
        hi, lo = _split_bf16(lk)
        within = _dot(hi, later) + _dot(lo, later)
        tot = jnp.sum(lk, axis=1, keepdims=True)
        carry = cs_ref[...]
        carries = []
        for r in range(len(kp_refs)):
            carries.append(carry)
            carry = carry + tot[r * rows:(r + 1) * rows]
        cs_ref[...] = carry
        a = jnp.exp(ls + within + jnp.concatenate(carries, axis=0))
        if keep is not None:
            a = jnp.where(keep, a, 0.0)
        upd = _stacked_values(a[0:rows], vp_refs[0], nh, dn)
        for r in range(1, len(kp_refs)):
            upd = upd + _stacked_values(a[r * rows:(r + 1) * rows], vp_refs[r], nh, dn)
        acc_ref[...] += upd
        return jnp.max(carry)

    for cp in page_copies(0, 0):
        cp.start()
    acc_ref[...] = jnp.zeros_like(acc_ref)
    cs_ref[...] = jnp.zeros_like(cs_ref)
    tok = lax.broadcasted_iota(jnp.int32, (rows, PAGE_SIZE), 0) & (dn - 1)
    key = lax.broadcasted_iota(jnp.int32, (rows, PAGE_SIZE), 1)
    cs_max = pages([kn_ref], [vn_ref], key < tok)

    def more(state):
        c, cs_max = state
        return jnp.logical_and(c < n_chunks, cs_max > UNDERFLOW_LOG)

    def trip(state):
        c, _ = state
        slot = c & 1

        @pl.when(c + 1 < n_chunks)
        def _():
            for cp in page_copies(c + 1, 1 - slot):
                cp.start()

        for cp in page_copies(c, slot):
            cp.wait()
        cs_max = pages([kbuf.at[slot, r] for r in range(ppc)], [vbuf.at[slot, r] for r in range(ppc)], None)
        return c + 1, cs_max

    c_end, _ = lax.while_loop(more, trip, (0, cs_max))

    @pl.when(c_end < n_chunks)
    def _():
        for cp in page_copies(c_end, c_end & 1):
            cp.wait()

    o_ref[...] = _gated_heads(acc_ref[...], g_ref[...], nh, dn).astype(o_ref.dtype)


def _sb_decode(lidx, pt_flat, q, k_new, v_new, g, cache_k, cache_v, nb, dn, nh, n_pages):
    ppc = SB_PAGES_PER_CHUNK
    assert n_pages % ppc == 0 and dn == SUBLANES
    width = nh * HEAD_DIM
    kern = functools.partial(_sb_decode_kernel, nh=nh, dn=dn, n_pages=n_pages, scale=1.0 / math.sqrt(HEAD_DIM))
    seq = lambda rows: pl.BlockSpec((None, rows, width), lambda b, l, pt: (b, 0, 0))
    new = pl.BlockSpec((None, nh, PAGE_SIZE, HEAD_DIM), lambda b, l, pt: (b, 0, 0, 0))
    hbm = pl.BlockSpec(memory_space=pl.ANY)
    chunk = (2, ppc, nh, PAGE_SIZE, HEAD_DIM)
    return pl.pallas_call(
        kern,
        grid_spec=pltpu.PrefetchScalarGridSpec(
            num_scalar_prefetch=2,
            grid=(nb,),
            in_specs=[seq(dn), new, new, seq(dn), hbm, hbm],
            out_specs=seq(dn),
            scratch_shapes=[pltpu.VMEM(chunk, F32), pltpu.VMEM(chunk, F32), pltpu.SemaphoreType.DMA((2, 2)),
                            pltpu.VMEM((nh * dn, HEAD_DIM), F32), pltpu.VMEM((nh * dn, LANES), F32)],
        ),
        out_shape=jax.ShapeDtypeStruct((nb, dn, width), F32),
        compiler_params=_params(("arbitrary",)),
        name="sb_decode",
    )(lidx, pt_flat, q, k_new, v_new, g, cache_k, cache_v)


def _moba_scores_kernel(l_ref, pt_ref, q_ref, *rest, nh, dn):
    pps = MOBA_PAGES_PER_STEP
    k_refs = rest[:pps]
    s_ref, km_ref = rest[pps:]
    s_idx = pl.program_id(1)
    qs = [q.astype(BF16) for q in _head_queries(q_ref, nh)]
    pages_per_block = MOBA_BLOCK // PAGE_SIZE
    blocks_per_step = pps // pages_per_block

    @pl.when(s_idx == 0)
    def _():
        km_ref[...] = jnp.zeros_like(km_ref)

    sub = lax.broadcasted_iota(jnp.int32, (blocks_per_step, HEAD_DIM), 0)
    for h in range(nh):
        means = jnp.zeros((blocks_per_step, HEAD_DIM), F32)
        for c in range(blocks_per_step):
            tot = jnp.zeros((1, HEAD_DIM), F32)
            for r in range(c * pages_per_block, (c + 1) * pages_per_block):
                kp = k_refs[r][h]
                s_ref[h * dn:(h + 1) * dn, r * PAGE_SIZE:(r + 1) * PAGE_SIZE] = _dot_nt(qs[h], kp.astype(BF16))
                tot = tot + jnp.sum(kp, axis=0, keepdims=True)
            means = jnp.where(sub == c, tot * (1.0 / MOBA_BLOCK), means)
        km_ref[h, pl.ds(pl.multiple_of(s_idx * blocks_per_step, blocks_per_step), blocks_per_step), :] = means


def _moba_scores(lidx, pt_flat, q, cache_k, nb, dn, nh, n_pages):
    pps = MOBA_PAGES_PER_STEP
    assert pps // (MOBA_BLOCK // PAGE_SIZE) == SUBLANES and n_pages % pps == 0
    width = nh * HEAD_DIM
    past = n_pages * PAGE_SIZE
    assert past // MOBA_BLOCK <= LANES
    kern = functools.partial(_moba_scores_kernel, nh=nh, dn=dn)
    return pl.pallas_call(
        kern,
        grid_spec=pltpu.PrefetchScalarGridSpec(
            num_scalar_prefetch=2,
            grid=(nb, n_pages // pps),
            in_specs=[pl.BlockSpec((None, dn, width), lambda b, s, l, pt: (b, 0, 0))]
            + _page_specs(pps, nh, n_pages, False),
            out_specs=[
                pl.BlockSpec((None, nh * dn, pps * PAGE_SIZE), lambda b, s, l, pt: (b, 0, s)),
                pl.BlockSpec((None, nh, LANES, HEAD_DIM), lambda b, s, l, pt: (b, 0, 0, 0)),
            ],
        ),
        out_shape=[jax.ShapeDtypeStruct((nb, nh * dn, past), F32),
                   jax.ShapeDtypeStruct((nb, nh, LANES, HEAD_DIM), F32)],
        compiler_params=_params(("parallel", "arbitrary")),
        name="moba_scores",
    )(lidx, pt_flat, q, *([cache_k] * pps))


def _moba_values_kernel(l_ref, pt_ref, q_ref, km_ref, s_ref, bias_ref, kn_ref, vn_ref, bown_ref, g_ref, *rest,
                        nh, dn, n_past_blocks, scale):
    pps = MOBA_PAGES_PER_STEP
    v_refs = rest[:pps]
    o_ref, sel_ref, m_ref, l_scr, acc_ref = rest[pps:]
    s_idx = pl.program_id(1)
    rows = nh * dn
    width = pps * PAGE_SIZE
    blocks_per_step = width // MOBA_BLOCK

    @pl.when(s_idx == 0)
    def _():
        qf = _head_queries(q_ref, nh)
        lane = lax.broadcasted_iota(jnp.int32, (rows, LANES), 1)
        gate = jnp.concatenate([_gate_scores(qf[h], km_ref[h]) for h in range(nh)], axis=0)
        gate = jnp.where(lane < n_past_blocks, gate, NEG_INF)
        sel_ref[...] = _top_k_lanes(gate, MOBA_TOPK)
        tok = lax.broadcasted_iota(jnp.int32, (rows, PAGE_SIZE), 0) & (dn - 1)
        key = lax.broadcasted_iota(jnp.int32, (rows, PAGE_SIZE), 1)
        s_own = _stacked_scores([q.astype(BF16) for q in qf], kn_ref) * scale + bown_ref[...]
        s_own = jnp.where(key <= tok, s_own, NEG_INF)
        m0 = jnp.max(s_own, axis=1, keepdims=True)
        p_own = jnp.exp(s_own - m0)
        m_ref[...] = jnp.broadcast_to(m0, m_ref.shape)
        l_scr[...] = jnp.broadcast_to(jnp.sum(p_own, axis=1, keepdims=True), l_scr.shape)
        acc_ref[...] = _stacked_values(p_own, vn_ref, nh, dn)

    key_in_step = lax.broadcasted_iota(jnp.int32, (LANES, width), 1)
    blk_of_key = lax.shift_right_logical(key_in_step, int(math.log2(MOBA_BLOCK))) + s_idx * blocks_per_step
    indicator = jnp.where(lax.broadcasted_iota(jnp.int32, (LANES, width), 0) == blk_of_key, 1.0, 0.0).astype(BF16)
    chosen = _dot(sel_ref[...].astype(BF16), indicator) > 0.5
    sm = jnp.where(chosen, s_ref[...] * scale + bias_ref[...], NEG_INF)
    m_prev = m_ref[...]
    m_new = jnp.maximum(m_prev, jnp.max(sm, axis=1, keepdims=True))
    alpha = jnp.exp(m_prev - m_new)
    p = jnp.exp(sm - _lanes(m_new, width))
    l_scr[...] = alpha * l_scr[...] + jnp.sum(p, axis=1, keepdims=True)
    upd = _stacked_values(p[:, 0:PAGE_SIZE], v_refs[0], nh, dn)
    for r in range(1, pps):
        upd = upd + _stacked_values(p[:, r * PAGE_SIZE:(r + 1) * PAGE_SIZE], v_refs[r], nh, dn)
    acc_ref[...] = alpha * acc_ref[...] + upd
    m_ref[...] = m_new

    @pl.when(s_idx == pl.num_programs(1) - 1)
    def _():
        o_ref[...] = _gated_heads(acc_ref[...] / l_scr[...], g_ref[...], nh, dn).astype(o_ref.dtype)


def _moba_values(lidx, pt_flat, q, km, scores, bias_past, k_new, v_new, bias_own, g, cache_v,
                 nb, dn, nh, n_pages):
    pps = MOBA_PAGES_PER_STEP
    width = nh * HEAD_DIM
    rows = nh * dn
    past = n_pages * PAGE_SIZE
    n_past_blocks = past // MOBA_BLOCK
    assert n_past_blocks <= LANES and km.shape[2] == LANES and n_pages % pps == 0
    assert (pps * PAGE_SIZE) % MOBA_BLOCK == 0
    kern = functools.partial(_moba_values_kernel, nh=nh, dn=dn, n_past_blocks=n_past_blocks,
                             scale=1.0 / math.sqrt(HEAD_DIM))
    seq = lambda r, c: pl.BlockSpec((None, r, c), lambda b, s, l, pt: (b, 0, 0))
    per_head = lambda r: pl.BlockSpec((None, nh, r, HEAD_DIM), lambda b, s, l, pt: (b, 0, 0, 0))
    return pl.pallas_call(
        kern,
        grid_spec=pltpu.PrefetchScalarGridSpec(
            num_scalar_prefetch=2,
            grid=(nb, n_pages // pps),
            in_specs=[seq(dn, width), per_head(LANES),
                      pl.BlockSpec((None, rows, pps * PAGE_SIZE), lambda b, s, l, pt: (b, 0, s)),
                      pl.BlockSpec((rows, pps * PAGE_SIZE), lambda b, s, l, pt: (0, s)),
                      per_head(PAGE_SIZE), per_head(PAGE_SIZE),
                      pl.BlockSpec((rows, PAGE_SIZE), lambda b, s, l, pt: (0, 0)),
                      seq(dn, width)] + _page_specs(pps, nh, n_pages, False),
            out_specs=seq(dn, width),
            scratch_shapes=[pltpu.VMEM((rows, LANES), F32), pltpu.VMEM((rows, LANES), F32),
                            pltpu.VMEM((rows, LANES), F32), pltpu.VMEM((rows, HEAD_DIM), F32)],
        ),
        out_shape=jax.ShapeDtypeStruct((nb, dn, width), F32),
        compiler_params=_params(("parallel", "arbitrary")),
        name="moba_values",
    )(lidx, pt_flat, q, km, scores, bias_past, k_new, v_new, bias_own, g, *([cache_v] * pps))


def _out_proj_kernel(l_ref, x_ref, a_ref, b_ref, c_ref, wa_ref, wb_ref, wc_ref, o_ref):
    acc = _dot(a_ref[...].astype(BF16), wa_ref[...])
    acc += _dot(b_ref[...].astype(BF16), wb_ref[...])
    acc += _dot(c_ref[...].astype(BF16), wc_ref[...])
    o_ref[...] = x_ref[...] + acc


def _out_proj(lidx, x2d, a, b, c, wa, wb, wc, tm):
    T, d_model = x2d.shape
    assert T % tm == 0
    act = lambda arr: pl.BlockSpec((tm, arr.shape[1]), lambda i, l: (i, 0))
    wgt = lambda arr: pl.BlockSpec((None,) + arr.shape[1:], lambda i, l: (l[0], 0, 0))
    return pl.pallas_call(
        _out_proj_kernel,
        grid_spec=pltpu.PrefetchScalarGridSpec(
            num_scalar_prefetch=1,
            grid=(T // tm,),
            in_specs=[act(x2d), act(a), act(b), act(c), wgt(wa), wgt(wb), wgt(wc)],
            out_specs=pl.BlockSpec((tm, d_model), lambda i, l: (i, 0)),
        ),
        out_shape=jax.ShapeDtypeStruct((T, d_model), F32),
        input_output_aliases={1: 0},
        compiler_params=_params(("parallel",)),
        name="out_proj",
    )(lidx, x2d, a, b, c, wa, wb, wc)


def _rel_bucket_np(n):
    n = np.maximum(n, 0)
    max_exact = REL_BUCKETS // 2
    nf = np.maximum(n, 1).astype(np.float64)
    large = max_exact + (np.log(nf / max_exact) / math.log(REL_MAX_DIST / max_exact)
                         * (REL_BUCKETS - max_exact)).astype(np.int32)
    large = np.minimum(large, REL_BUCKETS - 1)
    return np.where(n < max_exact, n, large).astype(np.int32)


def _toeplitz_bias_tiles(bias_by_dist, nq):
    blk = MOBA_BLOCK
    nh = bias_by_dist.shape[0]
    m = np.arange(2 * blk)
    d = np.arange(nq)[:, None] * blk
    gen_idx = np.where(m[None, :] <= blk, np.maximum(d - m[None, :], 0), d + 2 * blk - m[None, :])
    gen = bias_by_dist[:, gen_idx]
    wide = jnp.broadcast_to(gen[:, :, None, :], (nh, nq, blk, 2 * blk)).reshape(nh, nq, blk * 2 * blk)
    return wide[:, :, :blk * (2 * blk - 1)].reshape(nh, nq, blk, 2 * blk - 1)[:, :, :, :blk]


def _pool_halo(u3, hist, tm):
    nb, n, d = u3.shape
    nt = n // tm
    first = jnp.pad(hist, ((0, 0), (HALO - POOL_HIST, 0), (0, 0)))[:, None]
    if nt == 1:
        return first
    tails = u3.reshape(nb, nt, tm, d)[:, :-1, tm - HALO:, :]
    return jnp.concatenate([first, tails], axis=1)


def kernel(x_prompt, x_sample, cache_sb_k, cache_sb_v, cache_moba_k, cache_moba_v, state_pool, page_table,
           norm_g, w_in, w_pool, pool_scale, q_norm_g, k_norm_g, w_out, rel_bias):
    nb, n, d_model = x_prompt.shape
    db, dn, _ = x_sample.shape
    depth, n_phys, page_size, nh, hd = cache_sb_k.shape
    n_pages = page_table.shape[1]
    past = n_pages * page_size
    assert page_size == PAGE_SIZE and hd == HEAD_DIM
    d_pool = POOL_GROUP * len(POOL_WINDOWS)
    d_att = nh * HEAD_DIM
    d_proj = w_in.shape[-1]
    assert d_proj == 2 * d_pool + 8 * d_att
    seg = lambda t: slice(2 * d_pool + t * d_att, 2 * d_pool + (t + 1) * d_att)
    w_in_bf = w_in.astype(BF16)
    w_main = jnp.concatenate([w_in_bf[..., :2 * d_pool]] + [w_in_bf[..., seg(t)] for t in (0, 3, 4, 7)], axis=-1)
    w_kv = jnp.concatenate([w_in_bf[..., seg(t)] for t in (1, 2, 5, 6)], axis=-1)
    m_qb, m_gb, m_qc, m_gc = (2 * d_pool + t * d_att for t in range(4))
    kv_kb, kv_vb, kv_kc, kv_vc = (t * nh for t in range(4))
    zeros_att = jnp.zeros((depth, d_att), F32)
    g_main = jnp.concatenate([jnp.zeros((depth, m_qc), F32), jnp.tile(q_norm_g, (1, nh)), zeros_att], axis=-1)[:, None]
    g_kv = jnp.concatenate([zeros_att, zeros_att, jnp.tile(k_norm_g, (1, nh)), zeros_att], axis=-1)[:, None]
    norm_main = (m_qc, m_qc + d_att)
    norm_kv = (kv_kc * HEAD_DIM, (kv_kc + nh) * HEAD_DIM)
    q_gain, k_gain = q_norm_g[:, None, :], k_norm_g[:, None, :]

    w_out_bf = w_out.astype(BF16)
    wa, wb, wc = w_out_bf[:, :d_pool], w_out_bf[:, d_pool:d_pool + d_att], w_out_bf[:, d_pool + d_att:]
    norm_g3 = norm_g[:, None, :]
    pool_scale3 = pool_scale[:, None, :]
    caches = [jnp.transpose(c, (0, 1, 3, 2, 4)) for c in (cache_sb_k, cache_sb_v, cache_moba_k, cache_moba_v)]
    pt_flat = page_table.reshape(-1).astype(jnp.int32)

    n_dist = past + dn + 1
    buckets = _rel_bucket_np(np.arange(n_dist))
    bias_by_dist = rel_bias.astype(F32).T[:, buckets]
    bias_tiles = _toeplitz_bias_tiles(bias_by_dist, n // MOBA_BLOCK)
    rev = rel_bias.astype(F32).T[:, buckets[::-1].copy()]
    bias_past = jnp.stack([rev[:, n_dist - 1 - past - t:n_dist - 1 - t] for t in range(dn)], axis=1)
    bias_past = bias_past.reshape(nh * dn, past)
    dist_own = np.maximum(np.arange(dn)[:, None] - np.arange(PAGE_SIZE)[None, :], 0)
    bias_own = bias_by_dist[:, dist_own].reshape(nh * dn, PAGE_SIZE)

    tm_proj = min(n, IN_PROJ_ROWS)
    tm_p = min(n, ROW_TILE)
    hist0 = jnp.zeros((nb, POOL_HIST, d_pool), F32)

    def layer(carry, l):
        yp, ys = carry
        lidx = jnp.reshape(l, (1,)).astype(jnp.int32)

        p = _in_proj(lidx, yp, norm_g3, w_main, nb, n, tm_proj, IN_PROJ_COLS, False)
        kv = _in_proj(lidx, yp, norm_g3, w_kv, nb, n, tm_proj, d_att, True)
        u3 = p.reshape(nb, n, -1)[:, :, :d_pool]
        a = _pool(lidx, p, _pool_halo(u3, hist0, tm_p), w_pool, pool_scale3, nb, n, tm_p, 0, BF16)
        bmix = _sb_prefill(p, kv, nb, n, nh, m_qb, m_gb, kv_kb, kv_vb, SB_TILE)
        cmix, kc_norm = _moba_prefill(lidx, p, kv, bias_tiles, q_gain, k_gain, nb, n, nh, m_qc, m_gc, kv_kc, kv_vc)
        yp_new = _out_proj(lidx, yp, a, bmix, cmix, wa, wb, wc, tm_p)
        outs_p = (kv[:, kv_kb:kv_kb + nh], kv[:, kv_vb:kv_vb + nh], kc_norm, kv[:, kv_vc:kv_vc + nh],
                  u3[:, n - POOL_HIST:])

        ts = db * dn
        ps = _in_proj(lidx, ys, norm_g3, w_main, 1, ts, ts, IN_PROJ_COLS, False, g_main, norm_main)
        kvs = _in_proj(lidx, ys, norm_g3, w_kv, 1, ts, ts, d_att, False, g_kv, norm_kv)
        ps3 = ps.reshape(db, dn, -1)
        kvs5 = jnp.transpose(kvs.reshape(db, dn, 4, nh, HEAD_DIM), (2, 0, 3, 1, 4))
        hist_s = lax.dynamic_index_in_dim(state_pool, l, axis=0, keepdims=False)
        us3 = ps3[:, :, :d_pool]
        a_s = _pool(lidx, ps, _pool_halo(us3, hist_s, dn), w_pool, pool_scale3, db, dn, dn, past, F32)
        col = lambda c: ps3[:, :, c:c + d_att]
        new_page = lambda t: jnp.pad(kvs5[t], ((0, 0), (0, 0), (0, PAGE_SIZE - dn), (0, 0)))
        b_s = _sb_decode(lidx, pt_flat, col(m_qb), new_page(0), new_page(1), col(m_gb),
                         caches[0], caches[1], db, dn, nh, n_pages)
        scores, km = _moba_scores(lidx, pt_flat, col(m_qc), caches[2], db, dn, nh, n_pages)
        c_s = _moba_values(lidx, pt_flat, col(m_qc), km, scores, bias_past, new_page(2), new_page(3),
                           bias_own, col(m_gc), caches[3], db, dn, nh, n_pages)
        ys_new = _out_proj(lidx, ys, a_s, b_s.reshape(ts, d_att), c_s.reshape(ts, d_att), wa, wb, wc, ts)
        hist_new = jnp.concatenate([hist_s, us3], axis=1)[:, -POOL_HIST:]
        outs_s = tuple(kvs5[t] for t in range(4)) + (hist_new,)
        return (yp_new, ys_new), (outs_p, outs_s)

    (yp, ys), (outs_p, outs_s) = lax.scan(
        layer, (x_prompt.reshape(nb * n, d_model), x_sample.reshape(db * dn, d_model)), jnp.arange(depth))
    rows = lambda t: jnp.transpose(t, (0, 1, 3, 2, 4))
    outs_p = tuple(rows(t) for t in outs_p[:4]) + (outs_p[4],)
    outs_s = tuple(rows(t) for t in outs_s[:4]) + (outs_s[4],)
    return (yp.reshape(nb, n, d_model), ys.reshape(db, dn, d_model)) + outs_p + outs_s
```

```python
import functools
import math

import numpy as np
import jax
import jax.numpy as jnp
from jax import lax
from jax.experimental import pallas as pl
from jax.experimental.pallas import tpu as pltpu

F32 = jnp.float32
BF16 = jnp.bfloat16

HEAD_DIM = 128
POOL_WINDOWS = (2, 4, 8, 16)
POOL_GROUP = 128
POOL_HIST = max(POOL_WINDOWS) - 1
HALO = POOL_HIST + 1
PAGE_SIZE = 128
MOBA_BLOCK = 256
MOBA_TOPK = 3
REL_BUCKETS = 32
REL_MAX_DIST = 128
EPS = 1e-6
LANES = 128
SUBLANES = 8
VMEM_LIMIT_BYTES = 56 * 1024 * 1024
IN_PROJ_ROWS = 1024
IN_PROJ_COLS = 1024
ROW_TILE = 512
SB_TILE = 256
SB_HEADS_PER_STEP = 3
MOBA_HEADS_PER_STEP = 2
SB_PAGES_PER_CHUNK = 4
MOBA_PAGES_PER_STEP = 16
NEG_INF = float("-inf")
NEG_BIG = -1e30
UNDERFLOW_LOG = -104.0
LOG2E = 1.4426950408889634
LN2 = 0.6931471805599453

NT_DIMS = (((1,), (1,)), ((), ()))


def _dot(a, b):
    return jnp.dot(a, b, preferred_element_type=F32)


def _dot_nt(a, b):
    return lax.dot_general(a, b, NT_DIMS, preferred_element_type=F32)


def _split_bf16(x):
    hi = x.astype(BF16)
    lo = (x - hi.astype(F32)).astype(BF16)
    return hi, lo


def _log_sigmoid(z):
    return jnp.minimum(z, 0.0) - jnp.log(1.0 + jnp.exp(-jnp.abs(z)))


def _silu(g):
    return g / (1.0 + jnp.exp(-g))


def _rms_norm(x, gain):
    ms = jnp.mean(x * x, axis=-1, keepdims=True)
    return x * lax.rsqrt(ms + EPS) * gain


def _lanes(x, width):
    return x if width == LANES else jnp.concatenate([x] * (width // LANES), axis=1)


def _params(semantics):
    return pltpu.CompilerParams(dimension_semantics=semantics, vmem_limit_bytes=VMEM_LIMIT_BYTES)


def _in_proj_kernel(l_ref, x_ref, g_ref, w_ref, *rest, norm_heads):
    if norm_heads is None:
        o_ref, h_ref = rest
    else:
        qkg_ref, o_ref, h_ref = rest
    j = pl.program_id(2)

    @pl.when(j == 0)
    def _():
        h_ref[...] = _rms_norm(x_ref[...], g_ref[...]).astype(BF16)

    acc = _dot(h_ref[...], w_ref[...])
    if norm_heads is None:
        o_ref[...] = acc
        return

    heads_per_tile = acc.shape[1] // HEAD_DIM
    for s in range(heads_per_tile):
        sl = slice(s * HEAD_DIM, (s + 1) * HEAD_DIM)
        head = j * heads_per_tile + s
        needs_norm = jnp.logical_and(head >= norm_heads[0], head < norm_heads[1])

        @pl.when(needs_norm)
        def _(sl=sl):
            o_ref[:, sl] = _rms_norm(acc[:, sl], qkg_ref[:, sl])

        @pl.when(jnp.logical_not(needs_norm))
        def _(sl=sl):
            o_ref[:, sl] = acc[:, sl]


def _in_proj(lidx, x2d, norm_g, w_bf, nb, n, tm, tn, qkg=None, norm_cols=None):
    T, d_model = x2d.shape
    cols = w_bf.shape[-1]
    assert T == nb * n and n % tm == 0 and cols % tn == 0 and tn % HEAD_DIM == 0
    nt = n // tm
    norm_heads = None if qkg is None else (norm_cols[0] // HEAD_DIM, norm_cols[1] // HEAD_DIM)
    in_specs = [
        pl.BlockSpec((tm, d_model), lambda b, i, j, l: (b * nt + i, 0)),
        pl.BlockSpec((None, 1, d_model), lambda b, i, j, l: (l[0], 0, 0)),
        pl.BlockSpec((None, d_model, tn), lambda b, i, j, l: (l[0], 0, j)),
    ]
    args = [lidx, x2d, norm_g, w_bf]
    if qkg is not None:
        in_specs.append(pl.BlockSpec((None, 1, tn), lambda b, i, j, l: (l[0], 0, j)))
        args.append(qkg)
    return pl.pallas_call(
        functools.partial(_in_proj_kernel, norm_heads=norm_heads),
        grid_spec=pltpu.PrefetchScalarGridSpec(
            num_scalar_prefetch=1,
            grid=(nb, nt, cols // tn),
            in_specs=in_specs,
            out_specs=pl.BlockSpec((tm, tn), lambda b, i, j, l: (b * nt + i, j)),
            scratch_shapes=[pltpu.VMEM((tm, d_model), BF16)],
        ),
        out_shape=jax.ShapeDtypeStruct((T, cols), F32),
        compiler_params=_params(("parallel", "parallel", "arbitrary")),
        name="in_proj",
    )(*args)


def _kv_proj_kernel(l_ref, x_ref, g_ref, w_ref, *rest, n_stacks, nh):
    outs, h_ref = rest[n_stacks:-1], rest[-1]
    j = pl.program_id(2)

    @pl.when(j == 0)
    def _():
        h_ref[...] = _rms_norm(x_ref[...], g_ref[...]).astype(BF16)

    for kind, o_ref in enumerate(outs):
        @pl.when(j == kind)
        def _(o_ref=o_ref):
            acc = _dot(h_ref[...], w_ref[...])
            for s in range(nh):
                o_ref[s] = acc[:, s * HEAD_DIM:(s + 1) * HEAD_DIM]


def _kv_proj(lidx, x2d, norm_g, w_bf, stacks, depth, nb, n, nh, tm, stacked):
    T, d_model = x2d.shape
    tn = nh * HEAD_DIM
    assert T == nb * n and n % tm == 0 and w_bf.shape[-1] == len(stacked) * tn
    nt = n // tm
    out_specs, out_shapes = [], []
    for is_stacked in stacked:
        if is_stacked:
            out_specs.append(pl.BlockSpec((None, None, nh, tm, HEAD_DIM), lambda b, i, j, l: (l[0], b, 0, i, 0)))
            out_shapes.append(jax.ShapeDtypeStruct((depth, nb, nh, n, HEAD_DIM), F32))
        else:
            out_specs.append(pl.BlockSpec((None, nh, tm, HEAD_DIM), lambda b, i, j, l: (b, 0, i, 0)))
            out_shapes.append(jax.ShapeDtypeStruct((nb, nh, n, HEAD_DIM), F32))
    stacks = list(stacks)
    stacked_outs = [k for k, is_stacked in enumerate(stacked) if is_stacked]
    assert len(stacks) == len(stacked_outs)
    n_fixed = 4
    return pl.pallas_call(
        functools.partial(_kv_proj_kernel, n_stacks=len(stacks), nh=nh),
        grid_spec=pltpu.PrefetchScalarGridSpec(
            num_scalar_prefetch=1,
            grid=(nb, nt, len(stacked)),
            in_specs=[
                pl.BlockSpec((tm, d_model), lambda b, i, j, l: (b * nt + i, 0)),
                pl.BlockSpec((None, 1, d_model), lambda b, i, j, l: (l[0], 0, 0)),
                pl.BlockSpec((None, d_model, tn), lambda b, i, j, l: (l[0], 0, j)),
            ] + [pl.BlockSpec(memory_space=pl.ANY)] * len(stacks),
            out_specs=out_specs,
            scratch_shapes=[pltpu.VMEM((tm, d_model), BF16)],
        ),
        out_shape=out_shapes,
        input_output_aliases={n_fixed + s: stacked_outs[s] for s in range(len(stacks))},
        compiler_params=_params(("parallel", "parallel", "arbitrary")),
        name="kv_proj",
    )(lidx, x2d, norm_g, w_bf, *stacks)


def _pool_kernel(l_ref, u_ref, ga_ref, halo_ref, wp_ref, ps_ref, o_ref, ext_ref, *, tm, pos0):
    i = pl.program_id(1)
    ext_ref[0:HALO, :] = halo_ref[...]
    ext_ref[HALO:, :] = u_ref[...]
    pos = pos0 + i * tm + lax.broadcasted_iota(jnp.int32, (tm, 1), 0)
    for g, w in enumerate(POOL_WINDOWS):
        sl = slice(g * POOL_GROUP, (g + 1) * POOL_GROUP)
        s = ext_ref[HALO:HALO + tm, sl]
        for d in range(1, w):
            s = s + ext_ref[HALO - d:HALO - d + tm, sl]
        cnt = jnp.minimum(w, pos + 1).astype(F32)
        pooled = s / cnt - u_ref[:, sl]
        y = _dot(pooled.astype(BF16), wp_ref[g].astype(BF16)) * ps_ref[:, sl]
        o_ref[:, sl] = (y * _silu(ga_ref[:, sl])).astype(o_ref.dtype)


def _pool(lidx, p2d, halo, w_pool, pool_scale, nb, n, tm, pos0, u_col, g_col, out_dtype):
    d_pool = POOL_GROUP * len(POOL_WINDOWS)
    nt = n // tm
    assert u_col % d_pool == 0 and g_col % d_pool == 0
    uc, gc = u_col // d_pool, g_col // d_pool
    kern = functools.partial(_pool_kernel, tm=tm, pos0=pos0)
    return pl.pallas_call(
        kern,
        grid_spec=pltpu.PrefetchScalarGridSpec(
            num_scalar_prefetch=1,
            grid=(nb, nt),
            in_specs=[
                pl.BlockSpec((tm, d_pool), lambda b, i, l: (b * nt + i, uc)),
                pl.BlockSpec((tm, d_pool), lambda b, i, l: (b * nt + i, gc)),
                pl.BlockSpec((None, None, HALO, d_pool), lambda b, i, l: (b, i, 0, 0)),
                pl.BlockSpec((None, len(POOL_WINDOWS), POOL_GROUP, POOL_GROUP), lambda b, i, l: (l[0], 0, 0, 0)),
                pl.BlockSpec((None, 1, d_pool), lambda b, i, l: (l[0], 0, 0)),
            ],
            out_specs=pl.BlockSpec((tm, d_pool), lambda b, i, l: (b * nt + i, 0)),
            scratch_shapes=[pltpu.VMEM((HALO + tm, d_pool), F32)],
        ),
        out_shape=jax.ShapeDtypeStruct((nb * n, d_pool), out_dtype),
        compiler_params=_params(("parallel", "arbitrary")),
        name="pool",
    )(lidx, p2d, p2d, halo, w_pool, pool_scale)


def _sb_prefill_kernel(l_ref, q_ref, k_ref, v_ref, g_ref, o_ref, acc_ref, cs_ref, *, tq, scale):
    i = pl.program_id(2)
    hps = SB_HEADS_PER_STEP
    qs = [q_ref[:, h * HEAD_DIM:(h + 1) * HEAD_DIM].astype(BF16) for h in range(hps)]
    row = lax.broadcasted_iota(jnp.int32, (tq, tq), 0)
    col = lax.broadcasted_iota(jnp.int32, (tq, tq), 1)
    later = jnp.where(row > col, 1.0, 0.0).astype(BF16)

    def logits(kj):
        off = pl.multiple_of(kj * tq, tq)
        return jnp.concatenate([_dot_nt(qs[h], k_ref[h, pl.ds(off, tq), :].astype(BF16)) for h in range(hps)],
                               axis=0) * scale

    def tiles(z, kj, diagonal):
        off = pl.multiple_of(kj * tq, tq)
        ls = _log_sigmoid(z)
        lk = ls - z
        if diagonal:
            keep = jnp.concatenate([col < row] * hps, axis=0)
            lk = jnp.where(keep, lk, 0.0)
        hi, lo = _split_bf16(lk)
        la = _dot(hi, later) + _dot(lo, later) + _lanes(cs_ref[...], tq)
        a = jnp.exp(ls + la)
        if diagonal:
            a = jnp.where(keep, a, 0.0)
        for h in range(hps):
            v = v_ref[h, pl.ds(off, tq), :].astype(BF16)
            acc_ref[h] += _dot(a[h * tq:(h + 1) * tq].astype(BF16), v)
        cs = cs_ref[...] + jnp.sum(lk, axis=1, keepdims=True)
        cs_ref[...] = cs
        return jnp.max(cs)

    acc_ref[...] = jnp.zeros_like(acc_ref)
    cs_ref[...] = jnp.zeros_like(cs_ref)
    z_next = logits(jnp.maximum(i - 1, 0))
    cs_max = tiles(logits(i), i, True)

    def more(state):
        t, cs_max, _ = state
        return jnp.logical_and(t < i, cs_max > UNDERFLOW_LOG)

    def body(state):
        t, _, z = state
        z_after = logits(jnp.maximum(i - 2 - t, 0))
        return t + 1, tiles(z, i - 1 - t, False), z_after

    lax.while_loop(more, body, (0, cs_max, z_next))
    gate = _silu(g_ref[...])
    for h in range(hps):
        sl = slice(h * HEAD_DIM, (h + 1) * HEAD_DIM)
        o_ref[:, sl] = (acc_ref[h] * gate[:, sl]).astype(o_ref.dtype)


def _sb_prefill(lidx, p2d, k_stack, v_stack, nb, n, nh, q_col, g_col, tq):
    hps = SB_HEADS_PER_STEP
    width = hps * HEAD_DIM
    nq = n // tq
    assert nh % hps == 0 and q_col % width == 0 and g_col % width == 0
    kern = functools.partial(_sb_prefill_kernel, tq=tq, scale=1.0 / math.sqrt(HEAD_DIM))
    qc, gc = q_col // width, g_col // width
    heads = pl.BlockSpec((None, None, hps, n, HEAD_DIM), lambda b, h, i, l: (l[0], b, h, 0, 0))
    return pl.pallas_call(
        kern,
        grid_spec=pltpu.PrefetchScalarGridSpec(
            num_scalar_prefetch=1,
            grid=(nb, nh // hps, nq),
            in_specs=[
                pl.BlockSpec((tq, width), lambda b, h, i, l: (b * nq + i, qc + h)),
                heads, heads,
                pl.BlockSpec((tq, width), lambda b, h, i, l: (b * nq + i, gc + h)),
            ],
            out_specs=pl.BlockSpec((tq, width), lambda b, h, i, l: (b * nq + i, h)),
            scratch_shapes=[pltpu.VMEM((hps, tq, HEAD_DIM), F32), pltpu.VMEM((hps * tq, LANES), F32)],
        ),
        out_shape=jax.ShapeDtypeStruct((nb * n, nh * HEAD_DIM), BF16),
        compiler_params=_params(("parallel", "parallel", "arbitrary")),
        name="sb_prefill",
    )(lidx, p2d, k_stack, v_stack, p2d)


def _top_k_lanes(gate, k):
    lane = lax.broadcasted_iota(jnp.int32, gate.shape, 1).astype(F32)
    sel = jnp.zeros(gate.shape, F32)
    for _ in range(k):
        m = jnp.max(gate, axis=1, keepdims=True)
        cand = jnp.where(gate == m, lane, float(LANES))
        cand = jnp.where(m > NEG_INF, cand, float(LANES))
        idx = jnp.min(cand, axis=1, keepdims=True)
        pick = lane == idx
        sel = jnp.where(pick, 1.0, sel)
        gate = jnp.where(pick, NEG_INF, gate)
    return sel


def _top_k_sublanes(gate_t, k, n_blocks):
    block = lax.broadcasted_iota(jnp.int32, gate_t.shape, 0)
    rank = jnp.zeros(gate_t.shape, F32)
    for b2 in range(n_blocks):
        other = jnp.broadcast_to(gate_t[b2:b2 + 1, :], gate_t.shape)
        wins_tie = jnp.where(block > b2, 1.0, 0.0)
        rank = rank + jnp.where(other > gate_t, 1.0, jnp.where(other == gate_t, wins_tie, 0.0))
    return jnp.where(jnp.logical_and(rank < k, gate_t > NEG_INF), 1.0, 0.0)


def _gate_scores(a, b):
    a_hi, a_lo = _split_bf16(a)
    b_hi, b_lo = _split_bf16(b)
    return _dot_nt(a_hi, b_hi) + _dot_nt(a_hi, b_lo) + _dot_nt(a_lo, b_hi)


def _moba_prefill_kernel(l_ref, q_ref, k_ref, v_ref, g_ref, bias_ref, qg_ref, kg_ref, *rest, n_stacks, nblk, scale):
    o_ref, kn_ref, km_ref, m_ref, l_scr, acc_ref = rest[n_stacks:]
    i = pl.program_id(2)
    hps = MOBA_HEADS_PER_STEP
    blk = MOBA_BLOCK

    @pl.when(i == 0)
    def _():
        km_ref[...] = jnp.zeros_like(km_ref)
        for h in range(hps):
            for b in range(nblk):
                kn = _rms_norm(k_ref[h, b * blk:(b + 1) * blk, :], kg_ref[...])
                kn_ref[h, b * blk:(b + 1) * blk, :] = kn
                km_ref[h, b:b + 1, :] = jnp.sum(kn, axis=0, keepdims=True) * (1.0 / blk)

    row = lax.broadcasted_iota(jnp.int32, (blk, blk), 0)
    col = lax.broadcasted_iota(jnp.int32, (blk, blk), 1)
    block_id = lax.broadcasted_iota(jnp.int32, (LANES, blk), 0)
    eye = jnp.where(row == col, 1.0, 0.0).astype(BF16)
    gate_rows = -(-nblk // SUBLANES) * SUBLANES
    gate_block = lax.broadcasted_iota(jnp.int32, (gate_rows, blk), 0)
    qfs = [_rms_norm(q_ref[:, h * HEAD_DIM:(h + 1) * HEAD_DIM], qg_ref[...]) for h in range(hps)]
    qs = [qf.astype(BF16) for qf in qfs]

    m_ref[...] = jnp.full_like(m_ref, NEG_INF)
    l_scr[...] = jnp.zeros_like(l_scr)
    acc_ref[...] = jnp.zeros_like(acc_ref)

    def raw_scores(j):
        off = pl.multiple_of(j * blk, blk)
        return jnp.concatenate([_dot_nt(qs[h], kn_ref[h, pl.ds(off, blk), :].astype(BF16)) * scale + bias_ref[h, i - j]
                                for h in range(hps)], axis=0)

    s_own = jnp.where(jnp.concatenate([col <= row] * hps, axis=0), raw_scores(i), NEG_INF)
    raw_first = raw_scores(0)

    masks = []
    for h in range(hps):
        gate_t = jnp.where(gate_block < i, _gate_scores(km_ref[h, 0:gate_rows, :], qfs[h]), NEG_INF)
        mask_t = jnp.where(_top_k_sublanes(gate_t, MOBA_TOPK, nblk) > 0.0, 0.0, NEG_BIG)
        if gate_rows < LANES:
            mask_t = jnp.concatenate([mask_t, jnp.zeros((LANES - gate_rows, blk), F32)], axis=0)
        masks.append(_dot_nt(eye, mask_t.astype(BF16)))
    not_chosen = jnp.concatenate(masks, axis=0).astype(BF16)

    def masked(raw, j):
        return raw + _dot(not_chosen, jnp.where(block_id == j, 1.0, 0.0).astype(BF16))

    def scores(j):
        return masked(raw_scores(j), j)

    def accumulate(s, j):
        off = pl.multiple_of(j * blk, blk)
        m_prev = m_ref[...]
        m_new = jnp.maximum(m_prev, jnp.max(s, axis=1, keepdims=True))
        alpha = jnp.exp(m_prev - m_new)
        p = jnp.exp(s - _lanes(m_new, blk))
        l_scr[...] = alpha * l_scr[...] + jnp.sum(p, axis=1, keepdims=True)
        for h in range(hps):
            rows = slice(h * blk, (h + 1) * blk)
            v = v_ref[h, pl.ds(off, blk), :].astype(BF16)
            acc_ref[h] = alpha[rows] * acc_ref[h] + _dot(p[rows].astype(BF16), v)
        m_ref[...] = m_new

    s_first = masked(raw_first, 0)
    accumulate(s_own, i)

    def body(j, s):
        s_next = scores(jnp.minimum(j + 1, jnp.maximum(i - 1, 0)))
        accumulate(s, j)
        return s_next

    lax.fori_loop(0, i, body, s_first)
    gate = _silu(g_ref[...])
    inv_l = 1.0 / l_scr[...]
    for h in range(hps):
        sl = slice(h * HEAD_DIM, (h + 1) * HEAD_DIM)
        o_ref[:, sl] = (acc_ref[h] * inv_l[h * blk:(h + 1) * blk] * gate[:, sl]).astype(o_ref.dtype)


def _moba_prefill(lidx, p2d, k_raw, v_stack, k_stack, bias_tiles, q_gain, k_gain, depth, nb, n, nh, q_col, g_col):
    hps = MOBA_HEADS_PER_STEP
    width = hps * HEAD_DIM
    blk = MOBA_BLOCK
    assert n % blk == 0
    nq = n // blk
    assert nq <= LANES
    assert nh % hps == 0 and q_col % width == 0 and g_col % width == 0
    extend = [k_stack]
    kern = functools.partial(_moba_prefill_kernel, n_stacks=len(extend), nblk=nq, scale=1.0 / math.sqrt(HEAD_DIM))
    qc, gc = q_col // width, g_col // width
    gain = pl.BlockSpec((None, 1, HEAD_DIM), lambda b, h, i, l: (l[0], 0, 0))
    layer_heads = pl.BlockSpec((None, None, hps, n, HEAD_DIM), lambda b, h, i, l: (l[0], b, h, 0, 0))
    n_fixed = 8
    return pl.pallas_call(
        kern,
        grid_spec=pltpu.PrefetchScalarGridSpec(
            num_scalar_prefetch=1,
            grid=(nb, nh // hps, nq),
            in_specs=[
                pl.BlockSpec((blk, width), lambda b, h, i, l: (b * nq + i, qc + h)),
                pl.BlockSpec((None, hps, n, HEAD_DIM), lambda b, h, i, l: (b, h, 0, 0)),
                layer_heads,
                pl.BlockSpec((blk, width), lambda b, h, i, l: (b * nq + i, gc + h)),
                pl.BlockSpec((hps, nq, blk, blk), lambda b, h, i, l: (h, 0, 0, 0)),
                gain, gain,
            ] + [pl.BlockSpec(memory_space=pl.ANY)] * len(extend),
            out_specs=[
                pl.BlockSpec((blk, width), lambda b, h, i, l: (b * nq + i, h)),
                layer_heads,
            ],
            scratch_shapes=[pltpu.VMEM((hps, LANES, HEAD_DIM), F32), pltpu.VMEM((hps * blk, LANES), F32),
                            pltpu.VMEM((hps * blk, LANES), F32), pltpu.VMEM((hps, blk, HEAD_DIM), F32)],
        ),
        out_shape=[jax.ShapeDtypeStruct((nb * n, nh * HEAD_DIM), BF16),
                   jax.ShapeDtypeStruct((depth, nb, nh, n, HEAD_DIM), F32)],
        input_output_aliases={n_fixed: 1},
        compiler_params=_params(("parallel", "parallel", "arbitrary")),
        name="moba_prefill",
    )(lidx, p2d, k_raw, v_stack, p2d, bias_tiles, q_gain, k_gain, *extend)


def _head_queries(q_ref, nh):
    return [q_ref[:, h * HEAD_DIM:(h + 1) * HEAD_DIM] for h in range(nh)]


def _stacked_scores(qs, k_ref):
    return jnp.concatenate([_dot_nt(q, k_ref[h].astype(BF16)) for h, q in enumerate(qs)], axis=0)


def _stacked_values(p, v_ref, nh, dn):
    return jnp.concatenate(
        [_dot(p[h * dn:(h + 1) * dn].astype(BF16), v_ref[h].astype(BF16)) for h in range(nh)], axis=0)


def _gated_heads(acc, g, nh, dn):
    return jnp.concatenate([acc[h * dn:(h + 1) * dn] for h in range(nh)], axis=1) * _silu(g)


def _page_index_map(r, pps, n_pages, reverse):
    def index_map(b, s, l, pt):
        page = s * pps + r
        if reverse:
            page = n_pages - 1 - page
        return (l[0], pt[b * n_pages + page], 0, 0, 0)

    return index_map


def _page_specs(pps, nh, n_pages, reverse):
    return [pl.BlockSpec((None, None, nh, PAGE_SIZE, HEAD_DIM), _page_index_map(r, pps, n_pages, reverse))
            for r in range(pps)]


def _sb_decode_kernel(l_ref, pt_ref, q_ref, kn_ref, vn_ref, g_ref, ck_hbm, cv_hbm, o_ref,
                      kbuf, vbuf, sems, acc_ref, cs_ref, *, nh, dn, n_pages, scale):
    ppc = SB_PAGES_PER_CHUNK
    n_chunks = n_pages // ppc
    b = pl.program_id(0)
    rows = nh * dn

    def page_copies(c, slot):
        copies = []
        for r in range(ppc):
            page = pt_ref[b * n_pages + (n_pages - 1 - (c * ppc + r))]
            copies.append(pltpu.make_async_copy(ck_hbm.at[l_ref[0], page], kbuf.at[slot, r], sems.at[0, slot]))
            copies.append(pltpu.make_async_copy(cv_hbm.at[l_ref[0], page], vbuf.at[slot, r], sems.at[1, slot]))
        return copies

    qs = [q.astype(BF16) for q in _head_queries(q_ref, nh)]
    row = lax.broadcasted_iota(jnp.int32, (PAGE_SIZE, PAGE_SIZE), 0)
    col = lax.broadcasted_iota(jnp.int32, (PAGE_SIZE, PAGE_SIZE), 1)
    later = jnp.where(row > col, 1.0, 0.0).astype(BF16)

    def pages(kp_refs, vp_refs, keep):
        z = jnp.concatenate([_stacked_scores(qs, kp) for kp in kp_refs], axis=0) * scale
        ls = _log_sigmoid(z)
        lk = ls - z
        if keep is not None:
            lk = jnp.where(keep, lk, 0.0)
        hi, lo = _split_bf16(lk)
        within = _dot(hi, later) + _dot(lo, later)
        tot = jnp.sum(lk, axis=1, keepdims=True)
        carry = cs_ref[...]
        carries = []
        for r in range(len(kp_refs)):
            carries.append(carry)
            carry = carry + tot[r * rows:(r + 1) * rows]
        cs_ref[...] = carry
        a = jnp.exp(ls + within + jnp.concatenate(carries, axis=0))
        if keep is not None:
            a = jnp.where(keep, a, 0.0)
        upd = _stacked_values(a[0:rows], vp_refs[0], nh, dn)
        for r in range(1, len(kp_refs)):
            upd = upd + _stacked_values(a[r * rows:(r + 1) * rows], vp_refs[r], nh, dn)
        acc_ref[...] += upd
        return jnp.max(carry)

    for cp in page_copies(0, 0):
        cp.start()
    acc_ref[...] = jnp.zeros_like(acc_ref)
    cs_ref[...] = jnp.zeros_like(cs_ref)
    tok = lax.broadcasted_iota(jnp.int32, (rows, PAGE_SIZE), 0) & (dn - 1)
    key = lax.broadcasted_iota(jnp.int32, (rows, PAGE_SIZE), 1)
    cs_max = pages([kn_ref], [vn_ref], key < tok)

    def more(state):
        c, cs_max = state
        return jnp.logical_and(c < n_chunks, cs_max > UNDERFLOW_LOG)

    def trip(state):
        c, _ = state
        slot = c & 1

        @pl.when(c + 1 < n_chunks)
        def _():
            for cp in page_copies(c + 1, 1 - slot):
                cp.start()

        for cp in page_copies(c, slot):
            cp.wait()
        cs_max = pages([kbuf.at[slot, r] for r in range(ppc)], [vbuf.at[slot, r] for r in range(ppc)], None)
        return c + 1, cs_max

    c_end, _ = lax.while_loop(more, trip, (0, cs_max))

    @pl.when(c_end < n_chunks)
    def _():
        for cp in page_copies(c_end, c_end & 1):
            cp.wait()

    o_ref[...] = _gated_heads(acc_ref[...], g_ref[...], nh, dn).astype(o_ref.dtype)


def _sb_decode(lidx, pt_flat, q, k_new, v_new, g, cache_k, cache_v, nb, dn, nh, n_pages):
    ppc = SB_PAGES_PER_CHUNK
    assert n_pages % ppc == 0 and dn == SUBLANES
    width = nh * HEAD_DIM
    kern = functools.partial(_sb_decode_kernel, nh=nh, dn=dn, n_pages=n_pages, scale=1.0 / math.sqrt(HEAD_DIM))
    seq = lambda rows: pl.BlockSpec((None, rows, width), lambda b, l, pt: (b, 0, 0))
    new = pl.BlockSpec((None, nh, PAGE_SIZE, HEAD_DIM), lambda b, l, pt: (b, 0, 0, 0))
    hbm = pl.BlockSpec(memory_space=pl.ANY)
    chunk = (2, ppc, nh, PAGE_SIZE, HEAD_DIM)
    return pl.pallas_call(
        kern,
        grid_spec=pltpu.PrefetchScalarGridSpec(
            num_scalar_prefetch=2,
            grid=(nb,),
            in_specs=[seq(dn), new, new, seq(dn), hbm, hbm],
            out_specs=seq(dn),
            scratch_shapes=[pltpu.VMEM(chunk, F32), pltpu.VMEM(chunk, F32), pltpu.SemaphoreType.DMA((2, 2)),
                            pltpu.VMEM((nh * dn, HEAD_DIM), F32), pltpu.VMEM((nh * dn, LANES), F32)],
        ),
        out_shape=jax.ShapeDtypeStruct((nb, dn, width), F32),
        compiler_params=_params(("arbitrary",)),
        name="sb_decode",
    )(lidx, pt_flat, q, k_new, v_new, g, cache_k, cache_v)


def _moba_scores_kernel(l_ref, pt_ref, q_ref, *rest, nh, dn):
    pps = MOBA_PAGES_PER_STEP
    k_refs = rest[:pps]
    s_ref, km_ref = rest[pps:]
    s_idx = pl.program_id(1)
    qs = [q.astype(BF16) for q in _head_queries(q_ref, nh)]
    pages_per_block = MOBA_BLOCK // PAGE_SIZE
    blocks_per_step = pps // pages_per_block

    @pl.when(s_idx == 0)
    def _():
        km_ref[...] = jnp.zeros_like(km_ref)

    sub = lax.broadcasted_iota(jnp.int32, (blocks_per_step, HEAD_DIM), 0)
    for h in range(nh):
        means = jnp.zeros((blocks_per_step, HEAD_DIM), F32)
        for c in range(blocks_per_step):
            tot = jnp.zeros((1, HEAD_DIM), F32)
            for r in range(c * pages_per_block, (c + 1) * pages_per_block):
                kp = k_refs[r][h]
                s_ref[h * dn:(h + 1) * dn, r * PAGE_SIZE:(r + 1) * PAGE_SIZE] = _dot_nt(qs[h], kp.astype(BF16))
                tot = tot + jnp.sum(kp, axis=0, keepdims=True)
            means = jnp.where(sub == c, tot * (1.0 / MOBA_BLOCK), means)
        km_ref[h, pl.ds(pl.multiple_of(s_idx * blocks_per_step, blocks_per_step), blocks_per_step), :] = means


def _moba_scores(lidx, pt_flat, q, cache_k, nb, dn, nh, n_pages):
    pps = MOBA_PAGES_PER_STEP
    assert pps // (MOBA_BLOCK // PAGE_SIZE) == SUBLANES and n_pages % pps == 0
    width = nh * HEAD_DIM
    past = n_pages * PAGE_SIZE
    assert past // MOBA_BLOCK <= LANES
    kern = functools.partial(_moba_scores_kernel, nh=nh, dn=dn)
    return pl.pallas_call(
        kern,
        grid_spec=pltpu.PrefetchScalarGridSpec(
            num_scalar_prefetch=2,
            grid=(nb, n_pages // pps),
            in_specs=[pl.BlockSpec((None, dn, width), lambda b, s, l, pt: (b, 0, 0))]
            + _page_specs(pps, nh, n_pages, False),
            out_specs=[
                pl.BlockSpec((None, nh * dn, pps * PAGE_SIZE), lambda b, s, l, pt: (b, 0, s)),
                pl.BlockSpec((None, nh, LANES, HEAD_DIM), lambda b, s, l, pt: (b, 0, 0, 0)),
            ],
        ),
        out_shape=[jax.ShapeDtypeStruct((nb, nh * dn, past), F32),
                   jax.ShapeDtypeStruct((nb, nh, LANES, HEAD_DIM), F32)],
        compiler_params=_params(("parallel", "arbitrary")),
        name="moba_scores",
    )(lidx, pt_flat, q, *([cache_k] * pps))


def _moba_values_kernel(l_ref, pt_ref, q_ref, km_ref, s_ref, bias_ref, kn_ref, vn_ref, bown_ref, g_ref, *rest,
                        nh, dn, n_past_blocks, scale):
    pps = MOBA_PAGES_PER_STEP
    v_refs = rest[:pps]
    o_ref, sel_ref, m_ref, l_scr, acc_ref = rest[pps:]
    s_idx = pl.program_id(1)
    rows = nh * dn
    width = pps * PAGE_SIZE
    blocks_per_step = width // MOBA_BLOCK

    @pl.when(s_idx == 0)
    def _():
        qf = _head_queries(q_ref, nh)
        lane = lax.broadcasted_iota(jnp.int32, (rows, LANES), 1)
        gate = jnp.concatenate([_gate_scores(qf[h], km_ref[h]) for h in range(nh)], axis=0)
        gate = jnp.where(lane < n_past_blocks, gate, NEG_INF)
        sel_ref[...] = _top_k_lanes(gate, MOBA_TOPK)
        tok = lax.broadcasted_iota(jnp.int32, (rows, PAGE_SIZE), 0) & (dn - 1)
        key = lax.broadcasted_iota(jnp.int32, (rows, PAGE_SIZE), 1)
        s_own = _stacked_scores([q.astype(BF16) for q in qf], kn_ref) * scale + bown_ref[...]
        s_own = jnp.where(key <= tok, s_own, NEG_INF)
        m0 = jnp.max(s_own, axis=1, keepdims=True)
        p_own = jnp.exp(s_own - m0)
        m_ref[...] = jnp.broadcast_to(m0, m_ref.shape)
        l_scr[...] = jnp.broadcast_to(jnp.sum(p_own, axis=1, keepdims=True), l_scr.shape)
        acc_ref[...] = _stacked_values(p_own, vn_ref, nh, dn)

    key_in_step = lax.broadcasted_iota(jnp.int32, (LANES, width), 1)
    blk_of_key = lax.shift_right_logical(key_in_step, int(math.log2(MOBA_BLOCK))) + s_idx * blocks_per_step
    indicator = jnp.where(lax.broadcasted_iota(jnp.int32, (LANES, width), 0) == blk_of_key, 1.0, 0.0).astype(BF16)
    chosen = _dot(sel_ref[...].astype(BF16), indicator) > 0.5
    sm = jnp.where(chosen, s_ref[...] * scale + bias_ref[...], NEG_INF)
    m_prev = m_ref[...]
    m_new = jnp.maximum(m_prev, jnp.max(sm, axis=1, keepdims=True))
    alpha = jnp.exp(m_prev - m_new)
    p = jnp.exp(sm - _lanes(m_new, width))
    l_scr[...] = alpha * l_scr[...] + jnp.sum(p, axis=1, keepdims=True)
    upd = _stacked_values(p[:, 0:PAGE_SIZE], v_refs[0], nh, dn)
    for r in range(1, pps):
        upd = upd + _stacked_values(p[:, r * PAGE_SIZE:(r + 1) * PAGE_SIZE], v_refs[r], nh, dn)
    acc_ref[...] = alpha * acc_ref[...] + upd
    m_ref[...] = m_new

    @pl.when(s_idx == pl.num_programs(1) - 1)
    def _():
        o_ref[...] = _gated_heads(acc_ref[...] / l_scr[...], g_ref[...], nh, dn).astype(o_ref.dtype)


def _moba_values(lidx, pt_flat, q, km, scores, bias_past, k_new, v_new, bias_own, g, cache_v,
                 nb, dn, nh, n_pages):
    pps = MOBA_PAGES_PER_STEP
    width = nh * HEAD_DIM
    rows = nh * dn
    past = n_pages * PAGE_SIZE
    n_past_blocks = past // MOBA_BLOCK
    assert n_past_blocks <= LANES and km.shape[2] == LANES and n_pages % pps == 0
    assert (pps * PAGE_SIZE) % MOBA_BLOCK == 0
    kern = functools.partial(_moba_values_kernel, nh=nh, dn=dn, n_past_blocks=n_past_blocks,
                             scale=1.0 / math.sqrt(HEAD_DIM))
    seq = lambda r, c: pl.BlockSpec((None, r, c), lambda b, s, l, pt: (b, 0, 0))
    per_head = lambda r: pl.BlockSpec((None, nh, r, HEAD_DIM), lambda b, s, l, pt: (b, 0, 0, 0))
    return pl.pallas_call(
        kern,
        grid_spec=pltpu.PrefetchScalarGridSpec(
            num_scalar_prefetch=2,
            grid=(nb, n_pages // pps),
            in_specs=[seq(dn, width), per_head(LANES),
                      pl.BlockSpec((None, rows, pps * PAGE_SIZE), lambda b, s, l, pt: (b, 0, s)),
                      pl.BlockSpec((rows, pps * PAGE_SIZE), lambda b, s, l, pt: (0, s)),
                      per_head(PAGE_SIZE), per_head(PAGE_SIZE),
                      pl.BlockSpec((rows, PAGE_SIZE), lambda b, s, l, pt: (0, 0)),
                      seq(dn, width)] + _page_specs(pps, nh, n_pages, False),
            out_specs=seq(dn, width),
            scratch_shapes=[pltpu.VMEM((rows, LANES), F32), pltpu.VMEM((rows, LANES), F32),
                            pltpu.VMEM((rows, LANES), F32), pltpu.VMEM((rows, HEAD_DIM), F32)],
        ),
        out_shape=jax.ShapeDtypeStruct((nb, dn, width), F32),
        compiler_params=_params(("parallel", "arbitrary")),
        name="moba_values",
    )(lidx, pt_flat, q, km, scores, bias_past, k_new, v_new, bias_own, g, *([cache_v] * pps))


def _out_proj_kernel(l_ref, x_ref, a_ref, b_ref, c_ref, wa_ref, wb_ref, wc_ref, o_ref):
    acc = _dot(a_ref[...].astype(BF16), wa_ref[...])
    acc += _dot(b_ref[...].astype(BF16), wb_ref[...])
    acc += _dot(c_ref[...].astype(BF16), wc_ref[...])
    o_ref[...] = x_ref[...] + acc


def _out_proj(lidx, x2d, a, b, c, wa, wb, wc, tm):
    T, d_model = x2d.shape
    assert T % tm == 0
    act = lambda arr: pl.BlockSpec((tm, arr.shape[1]), lambda i, l: (i, 0))
    wgt = lambda arr: pl.BlockSpec((None,) + arr.shape[1:], lambda i, l: (l[0], 0, 0))
    return pl.pallas_call(
        _out_proj_kernel,
        grid_spec=pltpu.PrefetchScalarGridSpec(
            num_scalar_prefetch=1,
            grid=(T // tm,),
            in_specs=[act(x2d), act(a), act(b), act(c), wgt(wa), wgt(wb), wgt(wc)],
            out_specs=pl.BlockSpec((tm, d_model), lambda i, l: (i, 0)),
        ),
        out_shape=jax.ShapeDtypeStruct((T, d_model), F32),
        input_output_aliases={1: 0},
        compiler_params=_params(("parallel",)),
        name="out_proj",
    )(lidx, x2d, a, b, c, wa, wb, wc)


def _rel_bucket_np(n):
    n = np.maximum(n, 0)
    max_exact = REL_BUCKETS // 2
    nf = np.maximum(n, 1).astype(np.float64)
    large = max_exact + (np.log(nf / max_exact) / math.log(REL_MAX_DIST / max_exact)
                         * (REL_BUCKETS - max_exact)).astype(np.int32)
    large = np.minimum(large, REL_BUCKETS - 1)
    return np.where(n < max_exact, n, large).astype(np.int32)


def _toeplitz_bias_tiles(bias_by_dist, nq):
    blk = MOBA_BLOCK
    nh = bias_by_dist.shape[0]
    m = np.arange(2 * blk)
    d = np.arange(nq)[:, None] * blk
    gen_idx = np.where(m[None, :] <= blk, np.maximum(d - m[None, :], 0), d + 2 * blk - m[None, :])
    gen = bias_by_dist[:, gen_idx]
    wide = jnp.broadcast_to(gen[:, :, None, :], (nh, nq, blk, 2 * blk)).reshape(nh, nq, blk * 2 * blk)
    return wide[:, :, :blk * (2 * blk - 1)].reshape(nh, nq, blk, 2 * blk - 1)[:, :, :, :blk]


def _pool_halo(u3, hist, tm):
    nb, n, d = u3.shape
    nt = n // tm
    first = jnp.pad(hist, ((0, 0), (HALO - POOL_HIST, 0), (0, 0)))[:, None]
    if nt == 1:
        return first
    tails = u3.reshape(nb, nt, tm, d)[:, :-1, tm - HALO:, :]
    return jnp.concatenate([first, tails], axis=1)


def kernel(x_prompt, x_sample, cache_sb_k, cache_sb_v, cache_moba_k, cache_moba_v, state_pool, page_table,
           norm_g, w_in, w_pool, pool_scale, q_norm_g, k_norm_g, w_out, rel_bias):
    nb, n, d_model = x_prompt.shape
    db, dn, _ = x_sample.shape
    depth, n_phys, page_size, nh, hd = cache_sb_k.shape
    n_pages = page_table.shape[1]
    past = n_pages * page_size
    assert page_size == PAGE_SIZE and hd == HEAD_DIM
    d_pool = POOL_GROUP * len(POOL_WINDOWS)
    d_att = nh * HEAD_DIM
    d_proj = w_in.shape[-1]
    assert d_proj == 2 * d_pool + 8 * d_att
    seg = lambda t: slice(2 * d_pool + t * d_att, 2 * d_pool + (t + 1) * d_att)
    w_main = jnp.concatenate([w_in[..., seg(t)] for t in (0, 3, 4, 7)] + [w_in[..., :2 * d_pool]], axis=-1).astype(BF16)
    w_kv = jnp.concatenate([w_in[..., seg(t)] for t in (1, 2, 5, 6)], axis=-1).astype(BF16)
    m_qb, m_gb, m_qc, m_gc = (t * d_att for t in range(4))
    m_u, m_ga = 4 * d_att, 4 * d_att + d_pool
    zeros_att = jnp.zeros((depth, d_att), F32)
    g_main = jnp.concatenate([jnp.zeros((depth, m_qc), F32), jnp.tile(q_norm_g, (1, nh)),
                              jnp.zeros((depth, d_att + 2 * d_pool), F32)], axis=-1)[:, None]
    g_kv = jnp.concatenate([zeros_att, zeros_att, jnp.tile(k_norm_g, (1, nh)), zeros_att], axis=-1)[:, None]
    norm_main = (m_qc, m_qc + d_att)
    norm_kv = (2 * d_att, 3 * d_att)
    q_gain, k_gain = q_norm_g[:, None, :], k_norm_g[:, None, :]

    w_out_bf = w_out.astype(BF16)
    wa, wb, wc = w_out_bf[:, :d_pool], w_out_bf[:, d_pool:d_pool + d_att], w_out_bf[:, d_pool + d_att:]
    norm_g3 = norm_g[:, None, :]
    pool_scale3 = pool_scale[:, None, :]
    caches = [jnp.transpose(c, (0, 1, 3, 2, 4)) for c in (cache_sb_k, cache_sb_v, cache_moba_k, cache_moba_v)]
    pt_flat = page_table.reshape(-1).astype(jnp.int32)

    n_dist = past + dn + 1
    buckets = _rel_bucket_np(np.arange(n_dist))
    bias_by_dist = rel_bias.astype(F32).T[:, buckets]
    bias_tiles = _toeplitz_bias_tiles(bias_by_dist, n // MOBA_BLOCK)
    rev = rel_bias.astype(F32).T[:, buckets[::-1].copy()]
    bias_past = jnp.stack([rev[:, n_dist - 1 - past - t:n_dist - 1 - t] for t in range(dn)], axis=1)
    bias_past = bias_past.reshape(nh * dn, past)
    dist_own = np.maximum(np.arange(dn)[:, None] - np.arange(PAGE_SIZE)[None, :], 0)
    bias_own = bias_by_dist[:, dist_own].reshape(nh * dn, PAGE_SIZE)

    tm_proj = min(n, IN_PROJ_ROWS)
    tm_p = min(n, ROW_TILE)
    hist0 = jnp.zeros((nb, POOL_HIST, d_pool), F32)

    yp, ys = x_prompt.reshape(nb * n, d_model), x_sample.reshape(db * dn, d_model)
    ts = db * dn
    new_stack = lambda: jnp.zeros((depth, nb, nh, n, HEAD_DIM), F32)
    kv_stacks = (new_stack(), new_stack(), new_stack())
    mk_stack = new_stack()
    pool_p, kv_s, pool_s = [], [], []
    for l in range(depth):
        lidx = jnp.full((1,), l, jnp.int32)

        p = _in_proj(lidx, yp, norm_g3, w_main, nb, n, tm_proj, IN_PROJ_COLS)
        sbk, sbv, kc_raw, mbv = _kv_proj(lidx, yp, norm_g3, w_kv, kv_stacks, depth, nb, n, nh, tm_proj,
                                         (True, True, False, True))
        kv_stacks = (sbk, sbv, mbv)
        u3 = p.reshape(nb, n, -1)[:, :, m_u:m_u + d_pool]
        a = _pool(lidx, p, _pool_halo(u3, hist0, tm_p), w_pool, pool_scale3, nb, n, tm_p, 0, m_u, m_ga, BF16)
        bmix = _sb_prefill(lidx, p, sbk, sbv, nb, n, nh, m_qb, m_gb, SB_TILE)
        cmix, mk_stack = _moba_prefill(lidx, p, kc_raw, mbv, mk_stack, bias_tiles, q_gain, k_gain,
                                       depth, nb, n, nh, m_qc, m_gc)
        yp = _out_proj(lidx, yp, a, bmix, cmix, wa, wb, wc, tm_p)
        pool_p.append(u3[:, n - POOL_HIST:])

        ps = _in_proj(lidx, ys, norm_g3, w_main, 1, ts, ts, IN_PROJ_COLS, g_main, norm_main)
        kvs = _in_proj(lidx, ys, norm_g3, w_kv, 1, ts, ts, d_att, g_kv, norm_kv)
        ps3 = ps.reshape(db, dn, -1)
        kvs5 = jnp.transpose(kvs.reshape(db, dn, 4, nh, HEAD_DIM), (2, 0, 3, 1, 4))
        hist_s = state_pool[l]
        us3 = ps3[:, :, m_u:m_u + d_pool]
        a_s = _pool(lidx, ps, _pool_halo(us3, hist_s, dn), w_pool, pool_scale3, db, dn, dn, past, m_u, m_ga, F32)
        col = lambda c: ps3[:, :, c:c + d_att]
        new_page = lambda t: jnp.pad(kvs5[t], ((0, 0), (0, 0), (0, PAGE_SIZE - dn), (0, 0)))
        b_s = _sb_decode(lidx, pt_flat, col(m_qb), new_page(0), new_page(1), col(m_gb),
                         caches[0], caches[1], db, dn, nh, n_pages)
        scores, km = _moba_scores(lidx, pt_flat, col(m_qc), caches[2], db, dn, nh, n_pages)
        c_s = _moba_values(lidx, pt_flat, col(m_qc), km, scores, bias_past, new_page(2), new_page(3),
                           bias_own, col(m_gc), caches[3], db, dn, nh, n_pages)
        ys = _out_proj(lidx, ys, a_s, b_s.reshape(ts, d_att), c_s.reshape(ts, d_att), wa, wb, wc, ts)
        kv_s.append(kvs5)
        pool_s.append(jnp.concatenate([hist_s, us3], axis=1)[:, -POOL_HIST:])

    rows = lambda t: jnp.transpose(t, (0, 1, 3, 2, 4))
    sbk, sbv, mbv = kv_stacks
    kv_s = jnp.stack(kv_s, axis=1)
    return ((yp.reshape(nb, n, d_model), ys.reshape(db, dn, d_model))
            + (rows(sbk), rows(sbv), rows(mk_stack), rows(mbv), jnp.stack(pool_p))
            + tuple(rows(kv_s[t]) for t in range(4)) + (jnp.stack(pool_s),))
```

```python
import functools
import math

import numpy as np
import jax
import jax.numpy as jnp
from jax import lax
from jax.experimental import pallas as pl
from jax.experimental.pallas import tpu as pltpu

F32 = jnp.float32
BF16 = jnp.bfloat16

HEAD_DIM = 128
POOL_WINDOWS = (2, 4, 8, 16)
POOL_GROUP = 128
POOL_HIST = max(POOL_WINDOWS) - 1
HALO = POOL_HIST + 1
PAGE_SIZE = 128
MOBA_BLOCK = 256
MOBA_TOPK = 3
REL_BUCKETS = 32
REL_MAX_DIST = 128
EPS = 1e-6
LANES = 128
SUBLANES = 8
VMEM_LIMIT_BYTES = 56 * 1024 * 1024
IN_PROJ_ROWS = 1024
IN_PROJ_COLS = 1024
ROW_TILE = 512
SB_TILE = 256
SB_HEADS_PER_STEP = 6
MOBA_HEADS_PER_STEP = 2
SB_PAGES_PER_CHUNK = 4
MOBA_PAGES_PER_STEP = 16
NEG_INF = float("-inf")
NEG_BIG = -1e30
UNDERFLOW_LOG = -104.0
LOG2E = 1.4426950408889634
LN2 = 0.6931471805599453

NT_DIMS = (((1,), (1,)), ((), ()))


def _dot(a, b):
    return jnp.dot(a, b, preferred_element_type=F32)


def _dot_nt(a, b):
    return lax.dot_general(a, b, NT_DIMS, preferred_element_type=F32)


def _split_bf16(x):
    hi = x.astype(BF16)
    lo = (x - hi.astype(F32)).astype(BF16)
    return hi, lo


def _log_sigmoid(z):
    return jnp.minimum(z, 0.0) - jnp.log(1.0 + jnp.exp(-jnp.abs(z)))


def _silu(g):
    return g / (1.0 + jnp.exp(-g))


def _rms_norm(x, gain):
    ms = jnp.mean(x * x, axis=-1, keepdims=True)
    return x * lax.rsqrt(ms + EPS) * gain


def _lanes(x, width):
    return x if width == LANES else jnp.concatenate([x] * (width // LANES), axis=1)


def _params(semantics):
    return pltpu.CompilerParams(dimension_semantics=semantics, vmem_limit_bytes=VMEM_LIMIT_BYTES)


def _in_proj_kernel(l_ref, x_ref, g_ref, w_ref, *rest, norm_heads):
    if norm_heads is None:
        o_ref, h_ref = rest
    else:
        qkg_ref, o_ref, h_ref = rest
    j = pl.program_id(2)

    @pl.when(j == 0)
    def _():
        h_ref[...] = _rms_norm(x_ref[...], g_ref[...]).astype(BF16)

    acc = _dot(h_ref[...], w_ref[...])
    if norm_heads is None:
        o_ref[...] = acc
        return

    heads_per_tile = acc.shape[1] // HEAD_DIM
    for s in range(heads_per_tile):
        sl = slice(s * HEAD_DIM, (s + 1) * HEAD_DIM)
        head = j * heads_per_tile + s
        needs_norm = jnp.logical_and(head >= norm_heads[0], head < norm_heads[1])

        @pl.when(needs_norm)
        def _(sl=sl):
            o_ref[:, sl] = _rms_norm(acc[:, sl], qkg_ref[:, sl])

        @pl.when(jnp.logical_not(needs_norm))
        def _(sl=sl):
            o_ref[:, sl] = acc[:, sl]


def _in_proj(lidx, x2d, norm_g, w_bf, nb, n, tm, tn, qkg=None, norm_cols=None):
    T, d_model = x2d.shape
    cols = w_bf.shape[-1]
    assert T == nb * n and n % tm == 0 and cols % tn == 0 and tn % HEAD_DIM == 0
    nt = n // tm
    norm_heads = None if qkg is None else (norm_cols[0] // HEAD_DIM, norm_cols[1] // HEAD_DIM)
    in_specs = [
        pl.BlockSpec((tm, d_model), lambda b, i, j, l: (b * nt + i, 0)),
        pl.BlockSpec((None, 1, d_model), lambda b, i, j, l: (l[0], 0, 0)),
        pl.BlockSpec((None, d_model, tn), lambda b, i, j, l: (l[0], 0, j)),
    ]
    args = [lidx, x2d, norm_g, w_bf]
    if qkg is not None:
        in_specs.append(pl.BlockSpec((None, 1, tn), lambda b, i, j, l: (l[0], 0, j)))
        args.append(qkg)
    return pl.pallas_call(
        functools.partial(_in_proj_kernel, norm_heads=norm_heads),
        grid_spec=pltpu.PrefetchScalarGridSpec(
            num_scalar_prefetch=1,
            grid=(nb, nt, cols // tn),
            in_specs=in_specs,
            out_specs=pl.BlockSpec((tm, tn), lambda b, i, j, l: (b * nt + i, j)),
            scratch_shapes=[pltpu.VMEM((tm, d_model), BF16)],
        ),
        out_shape=jax.ShapeDtypeStruct((T, cols), F32),
        compiler_params=_params(("parallel", "parallel", "arbitrary")),
        name="in_proj",
    )(*args)


def _kv_proj_kernel(l_ref, x_ref, g_ref, w_ref, *rest, n_stacks, nh):
    outs, h_ref = rest[n_stacks:-1], rest[-1]
    j = pl.program_id(2)

    @pl.when(j == 0)
    def _():
        h_ref[...] = _rms_norm(x_ref[...], g_ref[...]).astype(BF16)

    for kind, o_ref in enumerate(outs):
        @pl.when(j == kind)
        def _(o_ref=o_ref):
            acc = _dot(h_ref[...], w_ref[...])
            for s in range(nh):
                o_ref[s] = acc[:, s * HEAD_DIM:(s + 1) * HEAD_DIM]


def _kv_proj(lidx, x2d, norm_g, w_bf, stacks, depth, nb, n, nh, tm, stacked):
    T, d_model = x2d.shape
    tn = nh * HEAD_DIM
    assert T == nb * n and n % tm == 0 and w_bf.shape[-1] == len(stacked) * tn
    nt = n // tm
    out_specs, out_shapes = [], []
    for is_stacked in stacked:
        if is_stacked:
            out_specs.append(pl.BlockSpec((None, None, nh, tm, HEAD_DIM), lambda b, i, j, l: (l[0], b, 0, i, 0)))
            out_shapes.append(jax.ShapeDtypeStruct((depth, nb, nh, n, HEAD_DIM), F32))
        else:
            out_specs.append(pl.BlockSpec((None, nh, tm, HEAD_DIM), lambda b, i, j, l: (b, 0, i, 0)))
            out_shapes.append(jax.ShapeDtypeStruct((nb, nh, n, HEAD_DIM), F32))
    stacks = list(stacks)
    stacked_outs = [k for k, is_stacked in enumerate(stacked) if is_stacked]
    assert len(stacks) == len(stacked_outs)
    n_fixed = 4
    return pl.pallas_call(
        functools.partial(_kv_proj_kernel, n_stacks=len(stacks), nh=nh),
        grid_spec=pltpu.PrefetchScalarGridSpec(
            num_scalar_prefetch=1,
            grid=(nb, nt, len(stacked)),
            in_specs=[
                pl.BlockSpec((tm, d_model), lambda b, i, j, l: (b * nt + i, 0)),
                pl.BlockSpec((None, 1, d_model), lambda b, i, j, l: (l[0], 0, 0)),
                pl.BlockSpec((None, d_model, tn), lambda b, i, j, l: (l[0], 0, j)),
            ] + [pl.BlockSpec(memory_space=pl.ANY)] * len(stacks),
            out_specs=out_specs,
            scratch_shapes=[pltpu.VMEM((tm, d_model), BF16)],
        ),
        out_shape=out_shapes,
        input_output_aliases={n_fixed + s: stacked_outs[s] for s in range(len(stacks))},
        compiler_params=_params(("parallel", "parallel", "arbitrary")),
        name="kv_proj",
    )(lidx, x2d, norm_g, w_bf, *stacks)


def _pool_kernel(l_ref, u_ref, ga_ref, halo_ref, wp_ref, ps_ref, o_ref, ext_ref, *, tm, pos0):
    i = pl.program_id(1)
    ext_ref[0:HALO, :] = halo_ref[...]
    ext_ref[HALO:, :] = u_ref[...]
    pos = pos0 + i * tm + lax.broadcasted_iota(jnp.int32, (tm, 1), 0)
    for g, w in enumerate(POOL_WINDOWS):
        sl = slice(g * POOL_GROUP, (g + 1) * POOL_GROUP)
        s = ext_ref[HALO:HALO + tm, sl]
        for d in range(1, w):
            s = s + ext_ref[HALO - d:HALO - d + tm, sl]
        cnt = jnp.minimum(w, pos + 1).astype(F32)
        pooled = s / cnt - u_ref[:, sl]
        y = _dot(pooled.astype(BF16), wp_ref[g].astype(BF16)) * ps_ref[:, sl]
        o_ref[:, sl] = (y * _silu(ga_ref[:, sl])).astype(o_ref.dtype)


def _pool(lidx, p2d, halo, w_pool, pool_scale, nb, n, tm, pos0, u_col, g_col, out_dtype):
    d_pool = POOL_GROUP * len(POOL_WINDOWS)
    nt = n // tm
    assert u_col % d_pool == 0 and g_col % d_pool == 0
    uc, gc = u_col // d_pool, g_col // d_pool
    kern = functools.partial(_pool_kernel, tm=tm, pos0=pos0)
    return pl.pallas_call(
        kern,
        grid_spec=pltpu.PrefetchScalarGridSpec(
            num_scalar_prefetch=1,
            grid=(nb, nt),
            in_specs=[
                pl.BlockSpec((tm, d_pool), lambda b, i, l: (b * nt + i, uc)),
                pl.BlockSpec((tm, d_pool), lambda b, i, l: (b * nt + i, gc)),
                pl.BlockSpec((None, None, HALO, d_pool), lambda b, i, l: (b, i, 0, 0)),
                pl.BlockSpec((None, len(POOL_WINDOWS), POOL_GROUP, POOL_GROUP), lambda b, i, l: (l[0], 0, 0, 0)),
                pl.BlockSpec((None, 1, d_pool), lambda b, i, l: (l[0], 0, 0)),
            ],
            out_specs=pl.BlockSpec((tm, d_pool), lambda b, i, l: (b * nt + i, 0)),
            scratch_shapes=[pltpu.VMEM((HALO + tm, d_pool), F32)],
        ),
        out_shape=jax.ShapeDtypeStruct((nb * n, d_pool), out_dtype),
        compiler_params=_params(("parallel", "arbitrary")),
        name="pool",
    )(lidx, p2d, p2d, halo, w_pool, pool_scale)


def _sb_prefill_kernel(l_ref, q_ref, k_ref, v_ref, g_ref, o_ref, acc_ref, cs_ref, *, tq, scale):
    i = pl.program_id(2)
    hps = SB_HEADS_PER_STEP
    qs = [q_ref[:, h * HEAD_DIM:(h + 1) * HEAD_DIM].astype(BF16) for h in range(hps)]
    row = lax.broadcasted_iota(jnp.int32, (tq, tq), 0)
    col = lax.broadcasted_iota(jnp.int32, (tq, tq), 1)
    later = jnp.where(row > col, 1.0, 0.0).astype(BF16)

    def logits(kj):
        off = pl.multiple_of(kj * tq, tq)
        return jnp.concatenate([_dot_nt(qs[h], k_ref[h, pl.ds(off, tq), :].astype(BF16)) for h in range(hps)],
                               axis=0) * scale

    def tiles(z, kj, diagonal):
        off = pl.multiple_of(kj * tq, tq)
        ls = _log_sigmoid(z)
        lk = ls - z
        if diagonal:
            keep = jnp.concatenate([col < row] * hps, axis=0)
            lk = jnp.where(keep, lk, 0.0)
        hi, lo = _split_bf16(lk)
        la = _dot(hi, later) + _dot(lo, later) + _lanes(cs_ref[...], tq)
        a = jnp.exp(ls + la)
        if diagonal:
            a = jnp.where(keep, a, 0.0)
        for h in range(hps):
            v = v_ref[h, pl.ds(off, tq), :].astype(BF16)
            acc_ref[h] += _dot(a[h * tq:(h + 1) * tq].astype(BF16), v)
        cs = cs_ref[...] + jnp.sum(lk, axis=1, keepdims=True)
        cs_ref[...] = cs
        return jnp.max(cs)

    acc_ref[...] = jnp.zeros_like(acc_ref)
    cs_ref[...] = jnp.zeros_like(cs_ref)
    z_next = logits(jnp.maximum(i - 1, 0))
    cs_max = tiles(logits(i), i, True)

    def more(state):
        t, cs_max, _ = state
        return jnp.logical_and(t < i, cs_max > UNDERFLOW_LOG)

    def body(state):
        t, _, z = state
        z_after = logits(jnp.maximum(i - 2 - t, 0))
        return t + 1, tiles(z, i - 1 - t, False), z_after

    lax.while_loop(more, body, (0, cs_max, z_next))
    gate = _silu(g_ref[...])
    for h in range(hps):
        sl = slice(h * HEAD_DIM, (h + 1) * HEAD_DIM)
        o_ref[:, sl] = (acc_ref[h] * gate[:, sl]).astype(o_ref.dtype)


def _sb_prefill(lidx, p2d, k_stack, v_stack, nb, n, nh, q_col, g_col, tq):
    hps = SB_HEADS_PER_STEP
    width = hps * HEAD_DIM
    nq = n // tq
    assert nh % hps == 0 and q_col % width == 0 and g_col % width == 0
    kern = functools.partial(_sb_prefill_kernel, tq=tq, scale=1.0 / math.sqrt(HEAD_DIM))
    qc, gc = q_col // width, g_col // width
    heads = pl.BlockSpec((None, None, hps, n, HEAD_DIM), lambda b, h, i, l: (l[0], b, h, 0, 0))
    return pl.pallas_call(
        kern,
        grid_spec=pltpu.PrefetchScalarGridSpec(
            num_scalar_prefetch=1,
            grid=(nb, nh // hps, nq),
            in_specs=[
                pl.BlockSpec((tq, width), lambda b, h, i, l: (b * nq + i, qc + h)),
                heads, heads,
                pl.BlockSpec((tq, width), lambda b, h, i, l: (b * nq + i, gc + h)),
            ],
            out_specs=pl.BlockSpec((tq, width), lambda b, h, i, l: (b * nq + i, h)),
            scratch_shapes=[pltpu.VMEM((hps, tq, HEAD_DIM), F32), pltpu.VMEM((hps * tq, LANES), F32)],
        ),
        out_shape=jax.ShapeDtypeStruct((nb * n, nh * HEAD_DIM), BF16),
        compiler_params=_params(("parallel", "parallel", "arbitrary")),
        name="sb_prefill",
    )(lidx, p2d, k_stack, v_stack, p2d)


def _top_k_lanes(gate, k):
    lane = lax.broadcasted_iota(jnp.int32, gate.shape, 1).astype(F32)
    sel = jnp.zeros(gate.shape, F32)
    for _ in range(k):
        m = jnp.max(gate, axis=1, keepdims=True)
        cand = jnp.where(gate == m, lane, float(LANES))
        cand = jnp.where(m > NEG_INF, cand, float(LANES))
        idx = jnp.min(cand, axis=1, keepdims=True)
        pick = lane == idx
        sel = jnp.where(pick, 1.0, sel)
        gate = jnp.where(pick, NEG_INF, gate)
    return sel


def _top_k_sublanes(gate_t, k, n_blocks):
    block = lax.broadcasted_iota(jnp.int32, gate_t.shape, 0)
    rank = jnp.zeros(gate_t.shape, F32)
    for b2 in range(n_blocks):
        other = jnp.broadcast_to(gate_t[b2:b2 + 1, :], gate_t.shape)
        wins_tie = jnp.where(block > b2, 1.0, 0.0)
        rank = rank + jnp.where(other > gate_t, 1.0, jnp.where(other == gate_t, wins_tie, 0.0))
    return jnp.where(jnp.logical_and(rank < k, gate_t > NEG_INF), 1.0, 0.0)


def _gate_scores(a, b):
    a_hi, a_lo = _split_bf16(a)
    b_hi, b_lo = _split_bf16(b)
    return _dot_nt(a_hi, b_hi) + _dot_nt(a_hi, b_lo) + _dot_nt(a_lo, b_hi)


def _moba_prefill_kernel(l_ref, q_ref, k_ref, v_ref, g_ref, bias_ref, qg_ref, kg_ref, *rest, n_stacks, nblk, scale):
    o_ref, kn_ref, km_ref, qn_ref, nc_ref, m_ref, l_scr, acc_ref = rest[n_stacks:]
    i = pl.program_id(2)
    hps = MOBA_HEADS_PER_STEP
    blk = MOBA_BLOCK
    n = nblk * blk
    row = lax.broadcasted_iota(jnp.int32, (blk, blk), 0)
    col = lax.broadcasted_iota(jnp.int32, (blk, blk), 1)
    block_id = lax.broadcasted_iota(jnp.int32, (LANES, blk), 0)

    @pl.when(i == 0)
    def _():
        eye = jnp.where(row == col, 1.0, 0.0).astype(BF16)
        gate_rows = -(-nblk // SUBLANES) * SUBLANES
        gate_block = lax.broadcasted_iota(jnp.int32, (gate_rows, n), 0)
        query_block = lax.shift_right_logical(lax.broadcasted_iota(jnp.int32, (gate_rows, n), 1), int(math.log2(blk)))
        km_ref[...] = jnp.zeros_like(km_ref)
        for h in range(hps):
            for b in range(nblk):
                kn = _rms_norm(k_ref[h, b * blk:(b + 1) * blk, :], kg_ref[...])
                kn_ref[h, b * blk:(b + 1) * blk, :] = kn
                km_ref[h, b:b + 1, :] = jnp.sum(kn, axis=0, keepdims=True) * (1.0 / blk)
            qf = _rms_norm(q_ref[:, h * HEAD_DIM:(h + 1) * HEAD_DIM], qg_ref[...])
            qn_ref[h] = qf.astype(BF16)
            gate_t = jnp.where(gate_block < query_block, _gate_scores(km_ref[h, 0:gate_rows, :], qf), NEG_INF)
            mask_t = jnp.where(_top_k_sublanes(gate_t, MOBA_TOPK, nblk) > 0.0, 0.0, NEG_BIG)
            if gate_rows < LANES:
                mask_t = jnp.concatenate([mask_t, jnp.zeros((LANES - gate_rows, n), F32)], axis=0)
            mask_t = mask_t.astype(BF16)
            for t in range(nblk):
                nc_ref[t, h * blk:(h + 1) * blk, :] = _dot_nt(eye, mask_t[:, t * blk:(t + 1) * blk]).astype(BF16)

    row0 = pl.multiple_of(i * blk, blk)
    qs = [qn_ref[h, pl.ds(row0, blk), :] for h in range(hps)]
    not_chosen = nc_ref[i]

    m_ref[...] = jnp.full_like(m_ref, NEG_INF)
    l_scr[...] = jnp.zeros_like(l_scr)
    acc_ref[...] = jnp.zeros_like(acc_ref)

    def raw_scores(j):
        off = pl.multiple_of(j * blk, blk)
        return jnp.concatenate([_dot_nt(qs[h], kn_ref[h, pl.ds(off, blk), :].astype(BF16)) * scale + bias_ref[h, i - j]
                                for h in range(hps)], axis=0)

    s_own = jnp.where(jnp.concatenate([col <= row] * hps, axis=0), raw_scores(i), NEG_INF)

    def masked(raw, j):
        return raw + _dot(not_chosen, jnp.where(block_id == j, 1.0, 0.0).astype(BF16))

    def scores(j):
        return masked(raw_scores(j), j)

    def accumulate(s, j):
        off = pl.multiple_of(j * blk, blk)
        m_prev = m_ref[...]
        m_new = jnp.maximum(m_prev, jnp.max(s, axis=1, keepdims=True))
        alpha = jnp.exp(m_prev - m_new)
        p = jnp.exp(s - _lanes(m_new, blk))
        l_scr[...] = alpha * l_scr[...] + jnp.sum(p, axis=1, keepdims=True)
        for h in range(hps):
            rows = slice(h * blk, (h + 1) * blk)
            v = v_ref[h, pl.ds(off, blk), :].astype(BF16)
            acc_ref[h] = alpha[rows] * acc_ref[h] + _dot(p[rows].astype(BF16), v)
        m_ref[...] = m_new

    s_first = scores(0)
    accumulate(s_own, i)

    def body(j, s):
        s_next = scores(jnp.minimum(j + 1, jnp.maximum(i - 1, 0)))
        accumulate(s, j)
        return s_next

    lax.fori_loop(0, i, body, s_first)
    gate = _silu(g_ref[...])
    inv_l = 1.0 / l_scr[...]
    for h in range(hps):
        sl = slice(h * HEAD_DIM, (h + 1) * HEAD_DIM)
        o_ref[:, sl] = (acc_ref[h] * inv_l[h * blk:(h + 1) * blk] * gate[:, sl]).astype(o_ref.dtype)


def _moba_prefill(lidx, p2d, k_raw, v_stack, k_stack, bias_tiles, q_gain, k_gain, depth, nb, n, nh, q_col, g_col):
    hps = MOBA_HEADS_PER_STEP
    width = hps * HEAD_DIM
    blk = MOBA_BLOCK
    assert n % blk == 0
    nq = n // blk
    assert nq <= LANES
    assert nh % hps == 0 and q_col % width == 0 and g_col % width == 0
    extend = [k_stack]
    kern = functools.partial(_moba_prefill_kernel, n_stacks=len(extend), nblk=nq, scale=1.0 / math.sqrt(HEAD_DIM))
    qc, gc = q_col // width, g_col // width
    gain = pl.BlockSpec((None, 1, HEAD_DIM), lambda b, h, i, l: (l[0], 0, 0))
    layer_heads = pl.BlockSpec((None, None, hps, n, HEAD_DIM), lambda b, h, i, l: (l[0], b, h, 0, 0))
    n_fixed = 8
    return pl.pallas_call(
        kern,
        grid_spec=pltpu.PrefetchScalarGridSpec(
            num_scalar_prefetch=1,
            grid=(nb, nh // hps, nq),
            in_specs=[
                pl.BlockSpec((n, width), lambda b, h, i, l: (b, qc + h)),
                pl.BlockSpec((None, hps, n, HEAD_DIM), lambda b, h, i, l: (b, h, 0, 0)),
                layer_heads,
                pl.BlockSpec((blk, width), lambda b, h, i, l: (b * nq + i, gc + h)),
                pl.BlockSpec((hps, nq, blk, blk), lambda b, h, i, l: (h, 0, 0, 0)),
                gain, gain,
            ] + [pl.BlockSpec(memory_space=pl.ANY)] * len(extend),
            out_specs=[
                pl.BlockSpec((blk, width), lambda b, h, i, l: (b * nq + i, h)),
                layer_heads,
            ],
            scratch_shapes=[pltpu.VMEM((hps, LANES, HEAD_DIM), F32),
                            pltpu.VMEM((hps, n, HEAD_DIM), BF16),
                            pltpu.VMEM((nq, hps * blk, LANES), BF16),
                            pltpu.VMEM((hps * blk, LANES), F32), pltpu.VMEM((hps * blk, LANES), F32),
                            pltpu.VMEM((hps, blk, HEAD_DIM), F32)],
        ),
        out_shape=[jax.ShapeDtypeStruct((nb * n, nh * HEAD_DIM), BF16),
                   jax.ShapeDtypeStruct((depth, nb, nh, n, HEAD_DIM), F32)],
        input_output_aliases={n_fixed: 1},
        compiler_params=_params(("parallel", "parallel", "arbitrary")),
        name="moba_prefill",
    )(lidx, p2d, k_raw, v_stack, p2d, bias_tiles, q_gain, k_gain, *extend)


def _head_queries(q_ref, nh):
    return [q_ref[:, h * HEAD_DIM:(h + 1) * HEAD_DIM] for h in range(nh)]


def _stacked_scores(qs, k_ref):
    return jnp.concatenate([_dot_nt(q, k_ref[h].astype(BF16)) for h, q in enumerate(qs)], axis=0)


def _stacked_values(p, v_ref, nh, dn):
    return jnp.concatenate(
        [_dot(p[h * dn:(h + 1) * dn].astype(BF16), v_ref[h].astype(BF16)) for h in range(nh)], axis=0)


def _gated_heads(acc, g, nh, dn):
    return jnp.concatenate([acc[h * dn:(h + 1) * dn] for h in range(nh)], axis=1) * _silu(g)


def _page_index_map(r, pps, n_pages, reverse):
    def index_map(b, s, l, pt):
        page = s * pps + r
        if reverse:
            page = n_pages - 1 - page
        return (l[0], pt[b * n_pages + page], 0, 0, 0)

    return index_map


def _page_specs(pps, nh, n_pages, reverse):
    return [pl.BlockSpec((None, None, nh, PAGE_SIZE, HEAD_DIM), _page_index_map(r, pps, n_pages, reverse))
            for r in range(pps)]


def _sb_decode_kernel(l_ref, pt_ref, q_ref, kn_ref, vn_ref, g_ref, ck_hbm, cv_hbm, o_ref,
                      kbuf, vbuf, sems, acc_ref, cs_ref, *, nh, dn, n_pages, scale):
    ppc = SB_PAGES_PER_CHUNK
    n_chunks = n_pages // ppc
    b = pl.program_id(0)
    rows = nh * dn

    def page_copies(c, slot):
        copies = []
        for r in range(ppc):
            page = pt_ref[b * n_pages + (n_pages - 1 - (c * ppc + r))]
            copies.append(pltpu.make_async_copy(ck_hbm.at[l_ref[0], page], kbuf.at[slot, r], sems.at[0, slot]))
            copies.append(pltpu.make_async_copy(cv_hbm.at[l_ref[0], page], vbuf.at[slot, r], sems.at[1, slot]))
        return copies

    qs = [q.astype(BF16) for q in _head_queries(q_ref, nh)]
    row = lax.broadcasted_iota(jnp.int32, (PAGE_SIZE, PAGE_SIZE), 0)
    col = lax.broadcasted_iota(jnp.int32, (PAGE_SIZE, PAGE_SIZE), 1)
    later = jnp.where(row > col, 1.0, 0.0).astype(BF16)

    def pages(kp_refs, vp_refs, keep):
        z = jnp.concatenate([_stacked_scores(qs, kp) for kp in kp_refs], axis=0) * scale
        ls = _log_sigmoid(z)
        lk = ls - z
        if keep is not None:
            lk = jnp.where(keep, lk, 0.0)
        hi, lo = _split_bf16(lk)
        within = _dot(hi, later) + _dot(lo, later)
        tot = jnp.sum(lk, axis=1, keepdims=True)
        carry = cs_ref[...]
        carries = []
        for r in range(len(kp_refs)):
            carries.append(carry)
            carry = carry + tot[r * rows:(r + 1) * rows]
        cs_ref[...] = carry
        a = jnp.exp(ls + within + jnp.concatenate(carries, axis=0))
        if keep is not None:
            a = jnp.where(keep, a, 0.0)
        upd = _stacked_values(a[0:rows], vp_refs[0], nh, dn)
        for r in range(1, len(kp_refs)):
            upd = upd + _stacked_values(a[r * rows:(r + 1) * rows], vp_refs[r], nh, dn)
        acc_ref[...] += upd
        return jnp.max(carry)

    for cp in page_copies(0, 0):
        cp.start()
    acc_ref[...] = jnp.zeros_like(acc_ref)
    cs_ref[...] = jnp.zeros_like(cs_ref)
    tok = lax.broadcasted_iota(jnp.int32, (rows, PAGE_SIZE), 0) & (dn - 1)
    key = lax.broadcasted_iota(jnp.int32, (rows, PAGE_SIZE), 1)
    cs_max = pages([kn_ref], [vn_ref], key < tok)

    def more(state):
        c, cs_max = state
        return jnp.logical_and(c < n_chunks, cs_max > UNDERFLOW_LOG)

    def trip(state):
        c, _ = state
        slot = c & 1

        @pl.when(c + 1 < n_chunks)
        def _():
            for cp in page_copies(c + 1, 1 - slot):
                cp.start()

        for cp in page_copies(c, slot):
            cp.wait()
        cs_max = pages([kbuf.at[slot, r] for r in range(ppc)], [vbuf.at[slot, r] for r in range(ppc)], None)
        return c + 1, cs_max

    c_end, _ = lax.while_loop(more, trip, (0, cs_max))

    @pl.when(c_end < n_chunks)
    def _():
        for cp in page_copies(c_end, c_end & 1):
            cp.wait()

    o_ref[...] = _gated_heads(acc_ref[...], g_ref[...], nh, dn).astype(o_ref.dtype)


def _sb_decode(lidx, pt_flat, q, k_new, v_new, g, cache_k, cache_v, nb, dn, nh, n_pages):
    ppc = SB_PAGES_PER_CHUNK
    assert n_pages % ppc == 0 and dn == SUBLANES
    width = nh * HEAD_DIM
    kern = functools.partial(_sb_decode_kernel, nh=nh, dn=dn, n_pages=n_pages, scale=1.0 / math.sqrt(HEAD_DIM))
    seq = lambda rows: pl.BlockSpec((None, rows, width), lambda b, l, pt: (b, 0, 0))
    new = pl.BlockSpec((None, nh, PAGE_SIZE, HEAD_DIM), lambda b, l, pt: (b, 0, 0, 0))
    hbm = pl.BlockSpec(memory_space=pl.ANY)
    chunk = (2, ppc, nh, PAGE_SIZE, HEAD_DIM)
    return pl.pallas_call(
        kern,
        grid_spec=pltpu.PrefetchScalarGridSpec(
            num_scalar_prefetch=2,
            grid=(nb,),
            in_specs=[seq(dn), new, new, seq(dn), hbm, hbm],
            out_specs=seq(dn),
            scratch_shapes=[pltpu.VMEM(chunk, F32), pltpu.VMEM(chunk, F32), pltpu.SemaphoreType.DMA((2, 2)),
                            pltpu.VMEM((nh * dn, HEAD_DIM), F32), pltpu.VMEM((nh * dn, LANES), F32)],
        ),
        out_shape=jax.ShapeDtypeStruct((nb, dn, width), F32),
        compiler_params=_params(("arbitrary",)),
        name="sb_decode",
    )(lidx, pt_flat, q, k_new, v_new, g, cache_k, cache_v)


def _moba_scores_kernel(l_ref, pt_ref, q_ref, *rest, nh, dn):
    pps = MOBA_PAGES_PER_STEP
    k_refs = rest[:pps]
    s_ref, km_ref = rest[pps:]
    s_idx = pl.program_id(1)
    qs = [q.astype(BF16) for q in _head_queries(q_ref, nh)]
    pages_per_block = MOBA_BLOCK // PAGE_SIZE
    blocks_per_step = pps // pages_per_block

    @pl.when(s_idx == 0)
    def _():
        km_ref[...] = jnp.zeros_like(km_ref)

    sub = lax.broadcasted_iota(jnp.int32, (blocks_per_step, HEAD_DIM), 0)
    for h in range(nh):
        means = jnp.zeros((blocks_per_step, HEAD_DIM), F32)
        for c in range(blocks_per_step):
            tot = jnp.zeros((1, HEAD_DIM), F32)
            for r in range(c * pages_per_block, (c + 1) * pages_per_block):
                kp = k_refs[r][h]
                s_ref[h * dn:(h + 1) * dn, r * PAGE_SIZE:(r + 1) * PAGE_SIZE] = _dot_nt(qs[h], kp.astype(BF16))
                tot = tot + jnp.sum(kp, axis=0, keepdims=True)
            means = jnp.where(sub == c, tot * (1.0 / MOBA_BLOCK), means)
        km_ref[h, pl.ds(pl.multiple_of(s_idx * blocks_per_step, blocks_per_step), blocks_per_step), :] = means


def _moba_scores(lidx, pt_flat, q, cache_k, nb, dn, nh, n_pages):
    pps = MOBA_PAGES_PER_STEP
    assert pps // (MOBA_BLOCK // PAGE_SIZE) == SUBLANES and n_pages % pps == 0
    width = nh * HEAD_DIM
    past = n_pages * PAGE_SIZE
    assert past // MOBA_BLOCK <= LANES
    kern = functools.partial(_moba_scores_kernel, nh=nh, dn=dn)
    return pl.pallas_call(
        kern,
        grid_spec=pltpu.PrefetchScalarGridSpec(
            num_scalar_prefetch=2,
            grid=(nb, n_pages // pps),
            in_specs=[pl.BlockSpec((None, dn, width), lambda b, s, l, pt: (b, 0, 0))]
            + _page_specs(pps, nh, n_pages, False),
            out_specs=[
                pl.BlockSpec((None, nh * dn, pps * PAGE_SIZE), lambda b, s, l, pt: (b, 0, s)),
                pl.BlockSpec((None, nh, LANES, HEAD_DIM), lambda b, s, l, pt: (b, 0, 0, 0)),
            ],
        ),
        out_shape=[jax.ShapeDtypeStruct((nb, nh * dn, past), F32),
                   jax.ShapeDtypeStruct((nb, nh, LANES, HEAD_DIM), F32)],
        compiler_params=_params(("parallel", "arbitrary")),
        name="moba_scores",
    )(lidx, pt_flat, q, *([cache_k] * pps))


def _moba_values_kernel(l_ref, pt_ref, q_ref, km_ref, s_ref, bias_ref, kn_ref, vn_ref, bown_ref, g_ref, *rest,
                        nh, dn, n_past_blocks, scale):
    pps = MOBA_PAGES_PER_STEP
    v_refs = rest[:pps]
    o_ref, sel_ref, m_ref, l_scr, acc_ref = rest[pps:]
    s_idx = pl.program_id(1)
    rows = nh * dn
    width = pps * PAGE_SIZE
    blocks_per_step = width // MOBA_BLOCK

    @pl.when(s_idx == 0)
    def _():
        qf = _head_queries(q_ref, nh)
        lane = lax.broadcasted_iota(jnp.int32, (rows, LANES), 1)
        gate = jnp.concatenate([_gate_scores(qf[h], km_ref[h]) for h in range(nh)], axis=0)
        gate = jnp.where(lane < n_past_blocks, gate, NEG_INF)
        sel_ref[...] = _top_k_lanes(gate, MOBA_TOPK)
        tok = lax.broadcasted_iota(jnp.int32, (rows, PAGE_SIZE), 0) & (dn - 1)
        key = lax.broadcasted_iota(jnp.int32, (rows, PAGE_SIZE), 1)
        s_own = _stacked_scores([q.astype(BF16) for q in qf], kn_ref) * scale + bown_ref[...]
        s_own = jnp.where(key <= tok, s_own, NEG_INF)
        m0 = jnp.max(s_own, axis=1, keepdims=True)
        p_own = jnp.exp(s_own - m0)
        m_ref[...] = jnp.broadcast_to(m0, m_ref.shape)
        l_scr[...] = jnp.broadcast_to(jnp.sum(p_own, axis=1, keepdims=True), l_scr.shape)
        acc_ref[...] = _stacked_values(p_own, vn_ref, nh, dn)

    key_in_step = lax.broadcasted_iota(jnp.int32, (LANES, width), 1)
    blk_of_key = lax.shift_right_logical(key_in_step, int(math.log2(MOBA_BLOCK))) + s_idx * blocks_per_step
    indicator = jnp.where(lax.broadcasted_iota(jnp.int32, (LANES, width), 0) == blk_of_key, 1.0, 0.0).astype(BF16)
    chosen = _dot(sel_ref[...].astype(BF16), indicator) > 0.5
    sm = jnp.where(chosen, s_ref[...] * scale + bias_ref[...], NEG_INF)
    m_prev = m_ref[...]
    m_new = jnp.maximum(m_prev, jnp.max(sm, axis=1, keepdims=True))
    alpha = jnp.exp(m_prev - m_new)
    p = jnp.exp(sm - _lanes(m_new, width))
    l_scr[...] = alpha * l_scr[...] + jnp.sum(p, axis=1, keepdims=True)
    upd = _stacked_values(p[:, 0:PAGE_SIZE], v_refs[0], nh, dn)
    for r in range(1, pps):
        upd = upd + _stacked_values(p[:, r * PAGE_SIZE:(r + 1) * PAGE_SIZE], v_refs[r], nh, dn)
    acc_ref[...] = alpha * acc_ref[...] + upd
    m_ref[...] = m_new

    @pl.when(s_idx == pl.num_programs(1) - 1)
    def _():
        o_ref[...] = _gated_heads(acc_ref[...] / l_scr[...], g_ref[...], nh, dn).astype(o_ref.dtype)


def _moba_values(lidx, pt_flat, q, km, scores, bias_past, k_new, v_new, bias_own, g, cache_v,
                 nb, dn, nh, n_pages):
    pps = MOBA_PAGES_PER_STEP
    width = nh * HEAD_DIM
    rows = nh * dn
    past = n_pages * PAGE_SIZE
    n_past_blocks = past // MOBA_BLOCK
    assert n_past_blocks <= LANES and km.shape[2] == LANES and n_pages % pps == 0
    assert (pps * PAGE_SIZE) % MOBA_BLOCK == 0
    kern = functools.partial(_moba_values_kernel, nh=nh, dn=dn, n_past_blocks=n_past_blocks,
                             scale=1.0 / math.sqrt(HEAD_DIM))
    seq = lambda r, c: pl.BlockSpec((None, r, c), lambda b, s, l, pt: (b, 0, 0))
    per_head = lambda r: pl.BlockSpec((None, nh, r, HEAD_DIM), lambda b, s, l, pt: (b, 0, 0, 0))
    return pl.pallas_call(
        kern,
        grid_spec=pltpu.PrefetchScalarGridSpec(
            num_scalar_prefetch=2,
            grid=(nb, n_pages // pps),
            in_specs=[seq(dn, width), per_head(LANES),
                      pl.BlockSpec((None, rows, pps * PAGE_SIZE), lambda b, s, l, pt: (b, 0, s)),
                      pl.BlockSpec((rows, pps * PAGE_SIZE), lambda b, s, l, pt: (0, s)),
                      per_head(PAGE_SIZE), per_head(PAGE_SIZE),
                      pl.BlockSpec((rows, PAGE_SIZE), lambda b, s, l, pt: (0, 0)),
                      seq(dn, width)] + _page_specs(pps, nh, n_pages, False),
            out_specs=seq(dn, width),
            scratch_shapes=[pltpu.VMEM((rows, LANES), F32), pltpu.VMEM((rows, LANES), F32),
                            pltpu.VMEM((rows, LANES), F32), pltpu.VMEM((rows, HEAD_DIM), F32)],
        ),
        out_shape=jax.ShapeDtypeStruct((nb, dn, width), F32),
        compiler_params=_params(("parallel", "arbitrary")),
        name="moba_values",
    )(lidx, pt_flat, q, km, scores, bias_past, k_new, v_new, bias_own, g, *([cache_v] * pps))


def _out_proj_kernel(l_ref, x_ref, a_ref, b_ref, c_ref, wa_ref, wb_ref, wc_ref, o_ref):
    acc = _dot(a_ref[...].astype(BF16), wa_ref[...])
    acc += _dot(b_ref[...].astype(BF16), wb_ref[...])
    acc += _dot(c_ref[...].astype(BF16), wc_ref[...])
    o_ref[...] = x_ref[...] + acc


def _out_proj(lidx, x2d, a, b, c, wa, wb, wc, tm):
    T, d_model = x2d.shape
    assert T % tm == 0
    act = lambda arr: pl.BlockSpec((tm, arr.shape[1]), lambda i, l: (i, 0))
    wgt = lambda arr: pl.BlockSpec((None,) + arr.shape[1:], lambda i, l: (l[0], 0, 0))
    return pl.pallas_call(
        _out_proj_kernel,
        grid_spec=pltpu.PrefetchScalarGridSpec(
            num_scalar_prefetch=1,
            grid=(T // tm,),
            in_specs=[act(x2d), act(a), act(b), act(c), wgt(wa), wgt(wb), wgt(wc)],
            out_specs=pl.BlockSpec((tm, d_model), lambda i, l: (i, 0)),
        ),
        out_shape=jax.ShapeDtypeStruct((T, d_model), F32),
        input_output_aliases={1: 0},
        compiler_params=_params(("parallel",)),
        name="out_proj",
    )(lidx, x2d, a, b, c, wa, wb, wc)


def _rel_bucket_np(n):
    n = np.maximum(n, 0)
    max_exact = REL_BUCKETS // 2
    nf = np.maximum(n, 1).astype(np.float64)
    large = max_exact + (np.log(nf / max_exact) / math.log(REL_MAX_DIST / max_exact)
                         * (REL_BUCKETS - max_exact)).astype(np.int32)
    large = np.minimum(large, REL_BUCKETS - 1)
    return np.where(n < max_exact, n, large).astype(np.int32)


def _toeplitz_kernel(gen_ref, o_ref):
    blk = MOBA_BLOCK
    wide = jnp.broadcast_to(gen_ref[...], (blk, 2 * blk))
    o_ref[...] = pltpu.roll(wide, 0, 1, stride=1, stride_axis=0)[:, :blk]


def _toeplitz_bias_tiles(bias_by_dist, nq):
    blk = MOBA_BLOCK
    nh = bias_by_dist.shape[0]
    m = np.arange(2 * blk)
    d = np.arange(nq)[:, None] * blk
    gen_idx = np.where(m[None, :] <= blk, np.maximum(d - m[None, :], 0), d + 2 * blk - m[None, :])
    gen = bias_by_dist[:, gen_idx][:, :, None, :]
    return pl.pallas_call(
        _toeplitz_kernel,
        grid=(nh, nq),
        in_specs=[pl.BlockSpec((None, None, 1, 2 * blk), lambda h, d: (h, d, 0, 0))],
        out_specs=pl.BlockSpec((None, None, blk, blk), lambda h, d: (h, d, 0, 0)),
        out_shape=jax.ShapeDtypeStruct((nh, nq, blk, blk), F32),
        compiler_params=_params(("parallel", "parallel")),
        name="bias_tiles",
    )(gen)


def _decode_bias_kernel(gen_ref, o_ref):
    dn, past = o_ref.shape
    wide = jnp.broadcast_to(gen_ref[...], (dn, gen_ref.shape[-1]))
    o_ref[...] = pltpu.roll(wide, 0, 1, stride=1, stride_axis=0)[:, :past]


def _decode_bias_rows(bias_t, buckets, past, dn):
    nh = bias_t.shape[0]
    width = past + LANES
    j = np.arange(width)
    dist = np.where(j <= past, past - j, np.minimum(past + width - j, past + dn))
    gen = bias_t[:, buckets[dist]][:, None, :]
    return pl.pallas_call(
        _decode_bias_kernel,
        grid=(nh,),
        in_specs=[pl.BlockSpec((None, 1, width), lambda h: (h, 0, 0))],
        out_specs=pl.BlockSpec((dn, past), lambda h: (h, 0)),
        out_shape=jax.ShapeDtypeStruct((nh * dn, past), F32),
        compiler_params=_params(("parallel",)),
        name="decode_bias",
    )(gen)


def _pool_halo(u3, hist, tm):
    nb, n, d = u3.shape
    nt = n // tm
    first = jnp.pad(hist, ((0, 0), (HALO - POOL_HIST, 0), (0, 0)))[:, None]
    if nt == 1:
        return first
    tails = u3.reshape(nb, nt, tm, d)[:, :-1, tm - HALO:, :]
    return jnp.concatenate([first, tails], axis=1)


def kernel(x_prompt, x_sample, cache_sb_k, cache_sb_v, cache_moba_k, cache_moba_v, state_pool, page_table,
           norm_g, w_in, w_pool, pool_scale, q_norm_g, k_norm_g, w_out, rel_bias):
    nb, n, d_model = x_prompt.shape
    db, dn, _ = x_sample.shape
    depth, n_phys, page_size, nh, hd = cache_sb_k.shape
    n_pages = page_table.shape[1]
    past = n_pages * page_size
    assert page_size == PAGE_SIZE and hd == HEAD_DIM
    d_pool = POOL_GROUP * len(POOL_WINDOWS)
    d_att = nh * HEAD_DIM
    d_proj = w_in.shape[-1]
    assert d_proj == 2 * d_pool + 8 * d_att
    seg = lambda t: slice(2 * d_pool + t * d_att, 2 * d_pool + (t + 1) * d_att)
    w_main = jnp.concatenate([w_in[..., seg(t)] for t in (0, 3, 4, 7)] + [w_in[..., :2 * d_pool]], axis=-1).astype(BF16)
    w_kv = jnp.concatenate([w_in[..., seg(t)] for t in (1, 2, 5, 6)], axis=-1).astype(BF16)
    m_qb, m_gb, m_qc, m_gc = (t * d_att for t in range(4))
    m_u, m_ga = 4 * d_att, 4 * d_att + d_pool
    zeros_att = jnp.zeros((depth, d_att), F32)
    g_main = jnp.concatenate([jnp.zeros((depth, m_qc), F32), jnp.tile(q_norm_g, (1, nh)),
                              jnp.zeros((depth, d_att + 2 * d_pool), F32)], axis=-1)[:, None]
    g_kv = jnp.concatenate([zeros_att, zeros_att, jnp.tile(k_norm_g, (1, nh)), zeros_att], axis=-1)[:, None]
    norm_main = (m_qc, m_qc + d_att)
    norm_kv = (2 * d_att, 3 * d_att)
    q_gain, k_gain = q_norm_g[:, None, :], k_norm_g[:, None, :]

    w_out_bf = w_out.astype(BF16)
    wa, wb, wc = w_out_bf[:, :d_pool], w_out_bf[:, d_pool:d_pool + d_att], w_out_bf[:, d_pool + d_att:]
    norm_g3 = norm_g[:, None, :]
    pool_scale3 = pool_scale[:, None, :]
    caches = [jnp.transpose(c, (0, 1, 3, 2, 4)) for c in (cache_sb_k, cache_sb_v, cache_moba_k, cache_moba_v)]
    pt_flat = page_table.reshape(-1).astype(jnp.int32)

    n_dist = past + dn + 1
    buckets = _rel_bucket_np(np.arange(n_dist))
    bias_by_dist = rel_bias.astype(F32).T[:, buckets]
    bias_tiles = _toeplitz_bias_tiles(bias_by_dist, n // MOBA_BLOCK)
    bias_past = _decode_bias_rows(rel_bias.astype(F32).T, buckets, past, dn)
    dist_own = np.maximum(np.arange(dn)[:, None] - np.arange(PAGE_SIZE)[None, :], 0)
    bias_own = bias_by_dist[:, dist_own].reshape(nh * dn, PAGE_SIZE)

    tm_proj = min(n, IN_PROJ_ROWS)
    tm_p = min(n, ROW_TILE)
    hist0 = jnp.zeros((nb, POOL_HIST, d_pool), F32)

    yp, ys = x_prompt.reshape(nb * n, d_model), x_sample.reshape(db * dn, d_model)
    ts = db * dn
    new_stack = lambda: jnp.zeros((depth, nb, nh, n, HEAD_DIM), F32)
    kv_stacks = (new_stack(), new_stack(), new_stack())
    mk_stack = new_stack()
    pool_p, kv_s, pool_s = [], [], []
    for l in range(depth):
        lidx = jnp.full((1,), l, jnp.int32)

        p = _in_proj(lidx, yp, norm_g3, w_main, nb, n, tm_proj, IN_PROJ_COLS)
        sbk, sbv, kc_raw, mbv = _kv_proj(lidx, yp, norm_g3, w_kv, kv_stacks, depth, nb, n, nh, tm_proj,
                                         (True, True, False, True))
        kv_stacks = (sbk, sbv, mbv)
        u3 = p.reshape(nb, n, -1)[:, :, m_u:m_u + d_pool]
        a = _pool(lidx, p, _pool_halo(u3, hist0, tm_p), w_pool, pool_scale3, nb, n, tm_p, 0, m_u, m_ga, BF16)
        bmix = _sb_prefill(lidx, p, sbk, sbv, nb, n, nh, m_qb, m_gb, SB_TILE)
        cmix, mk_stack = _moba_prefill(lidx, p, kc_raw, mbv, mk_stack, bias_tiles, q_gain, k_gain,
                                       depth, nb, n, nh, m_qc, m_gc)
        yp = _out_proj(lidx, yp, a, bmix, cmix, wa, wb, wc, tm_p)
        pool_p.append(u3[:, n - POOL_HIST:])

        ps = _in_proj(lidx, ys, norm_g3, w_main, 1, ts, ts, IN_PROJ_COLS, g_main, norm_main)
        kvs = _in_proj(lidx, ys, norm_g3, w_kv, 1, ts, ts, d_att, g_kv, norm_kv)
        ps3 = ps.reshape(db, dn, -1)
        kvs5 = jnp.transpose(kvs.reshape(db, dn, 4, nh, HEAD_DIM), (2, 0, 3, 1, 4))
        hist_s = state_pool[l]
        us3 = ps3[:, :, m_u:m_u + d_pool]
        a_s = _pool(lidx, ps, _pool_halo(us3, hist_s, dn), w_pool, pool_scale3, db, dn, dn, past, m_u, m_ga, F32)
        col = lambda c: ps3[:, :, c:c + d_att]
        new_page = lambda t: jnp.pad(kvs5[t], ((0, 0), (0, 0), (0, PAGE_SIZE - dn), (0, 0)))
        b_s = _sb_decode(lidx, pt_flat, col(m_qb), new_page(0), new_page(1), col(m_gb),
                         caches[0], caches[1], db, dn, nh, n_pages)
        scores, km = _moba_scores(lidx, pt_flat, col(m_qc), caches[2], db, dn, nh, n_pages)
        c_s = _moba_values(lidx, pt_flat, col(m_qc), km, scores, bias_past, new_page(2), new_page(3),
                           bias_own, col(m_gc), caches[3], db, dn, nh, n_pages)
        ys = _out_proj(lidx, ys, a_s, b_s.reshape(ts, d_att), c_s.reshape(ts, d_att), wa, wb, wc, ts)
        kv_s.append(kvs5)
        pool_s.append(jnp.concatenate([hist_s, us3], axis=1)[:, -POOL_HIST:])

    rows = lambda t: jnp.transpose(t, (0, 1, 3, 2, 4))
    sbk, sbv, mbv = kv_stacks
    kv_s = jnp.stack(kv_s, axis=1)
    return ((yp.reshape(nb, n, d_model), ys.reshape(db, dn, d_model))
            + (rows(sbk), rows(sbv), rows(mk_stack), rows(mbv), jnp.stack(pool_p))
            + tuple(rows(kv_s[t]) for t in range(4)) + (jnp.stack(pool_s),))
```

```python
import functools
import math

import numpy as np
import jax
import jax.numpy as jnp
from jax import lax
from jax.experimental import pallas as pl
from jax.experimental.pallas import tpu as pltpu

F32 = jnp.float32
BF16 = jnp.bfloat16

HEAD_DIM = 128
POOL_WINDOWS = (2, 4, 8, 16)
POOL_GROUP = 128
POOL_HIST = max(POOL_WINDOWS) - 1
HALO = POOL_HIST + 1
PAGE_SIZE = 128
MOBA_BLOCK = 256
MOBA_TOPK = 3
REL_BUCKETS = 32
REL_MAX_DIST = 128
EPS = 1e-6
LANES = 128
SUBLANES = 8
VMEM_LIMIT_BYTES = 56 * 1024 * 1024
IN_PROJ_ROWS = 1024
IN_PROJ_COLS = 1024
ROW_TILE = 512
OUT_PROJ_K = 256
SB_TILE = 256
SB_HEADS_PER_STEP = 6
MOBA_HEADS_PER_STEP = 2
SB_PAGES_PER_CHUNK = 4
MOBA_PAGES_PER_STEP = 16
NEG_INF = float("-inf")
NEG_BIG = -1e30
UNDERFLOW_LOG = -104.0
LOG2E = 1.4426950408889634
LN2 = 0.6931471805599453

NT_DIMS = (((1,), (1,)), ((), ()))


def _dot(a, b):
    return jnp.dot(a, b, preferred_element_type=F32)


def _dot_nt(a, b):
    return lax.dot_general(a, b, NT_DIMS, preferred_element_type=F32)


def _split_bf16(x):
    hi = x.astype(BF16)
    lo = (x - hi.astype(F32)).astype(BF16)
    return hi, lo


def _log_sigmoid(z):
    return jnp.minimum(z, 0.0) - jnp.log(1.0 + jnp.exp(-jnp.abs(z)))


def _silu(g):
    return g / (1.0 + jnp.exp(-g))


def _rms_norm(x, gain):
    ms = jnp.mean(x * x, axis=-1, keepdims=True)
    return x * lax.rsqrt(ms + EPS) * gain


def _lanes(x, width):
    return x if width == LANES else jnp.concatenate([x] * (width // LANES), axis=1)


def _params(semantics):
    return pltpu.CompilerParams(dimension_semantics=semantics, vmem_limit_bytes=VMEM_LIMIT_BYTES)


def _in_proj_kernel(l_ref, x_ref, g_ref, w_ref, *rest, norm_heads):
    if norm_heads is None:
        o_ref, h_ref = rest
    else:
        qkg_ref, o_ref, h_ref = rest
    j = pl.program_id(2)

    @pl.when(j == 0)
    def _():
        h_ref[...] = _rms_norm(x_ref[...], g_ref[...]).astype(BF16)

    acc = _dot(h_ref[...], w_ref[...])
    if norm_heads is None:
        o_ref[...] = acc
        return

    heads_per_tile = acc.shape[1] // HEAD_DIM
    for s in range(heads_per_tile):
        sl = slice(s * HEAD_DIM, (s + 1) * HEAD_DIM)
        head = j * heads_per_tile + s
        needs_norm = jnp.logical_and(head >= norm_heads[0], head < norm_heads[1])

        @pl.when(needs_norm)
        def _(sl=sl):
            o_ref[:, sl] = _rms_norm(acc[:, sl], qkg_ref[:, sl])

        @pl.when(jnp.logical_not(needs_norm))
        def _(sl=sl):
            o_ref[:, sl] = acc[:, sl]


def _in_proj(lidx, x2d, norm_g, w_bf, nb, n, tm, tn, qkg=None, norm_cols=None):
    T, d_model = x2d.shape
    cols = w_bf.shape[-1]
    assert T == nb * n and n % tm == 0 and cols % tn == 0 and tn % HEAD_DIM == 0
    nt = n // tm
    norm_heads = None if qkg is None else (norm_cols[0] // HEAD_DIM, norm_cols[1] // HEAD_DIM)
    in_specs = [
        pl.BlockSpec((tm, d_model), lambda b, i, j, l: (b * nt + i, 0)),
        pl.BlockSpec((None, 1, d_model), lambda b, i, j, l: (l[0], 0, 0)),
        pl.BlockSpec((None, d_model, tn), lambda b, i, j, l: (l[0], 0, j)),
    ]
    args = [lidx, x2d, norm_g, w_bf]
    if qkg is not None:
        in_specs.append(pl.BlockSpec((None, 1, tn), lambda b, i, j, l: (l[0], 0, j)))
        args.append(qkg)
    return pl.pallas_call(
        functools.partial(_in_proj_kernel, norm_heads=norm_heads),
        grid_spec=pltpu.PrefetchScalarGridSpec(
            num_scalar_prefetch=1,
            grid=(nb, nt, cols // tn),
            in_specs=in_specs,
            out_specs=pl.BlockSpec((tm, tn), lambda b, i, j, l: (b * nt + i, j)),
            scratch_shapes=[pltpu.VMEM((tm, d_model), BF16)],
        ),
        out_shape=jax.ShapeDtypeStruct((T, cols), F32),
        compiler_params=_params(("parallel", "parallel", "arbitrary")),
        name="in_proj",
    )(*args)


def _kv_proj_kernel(l_ref, x_ref, g_ref, w_ref, *rest, n_stacks, nh):
    outs, h_ref = rest[n_stacks:-1], rest[-1]
    j = pl.program_id(2)

    @pl.when(j == 0)
    def _():
        h_ref[...] = _rms_norm(x_ref[...], g_ref[...]).astype(BF16)

    for kind, o_ref in enumerate(outs):
        @pl.when(j == kind)
        def _(o_ref=o_ref):
            acc = _dot(h_ref[...], w_ref[...])
            for s in range(nh):
                o_ref[s] = acc[:, s * HEAD_DIM:(s + 1) * HEAD_DIM]


def _kv_proj(lidx, x2d, norm_g, w_bf, stacks, depth, nb, n, nh, tm, stacked):
    T, d_model = x2d.shape
    tn = nh * HEAD_DIM
    assert T == nb * n and n % tm == 0 and w_bf.shape[-1] == len(stacked) * tn
    nt = n // tm
    out_specs, out_shapes = [], []
    for is_stacked in stacked:
        if is_stacked:
            out_specs.append(pl.BlockSpec((None, None, nh, tm, HEAD_DIM), lambda b, i, j, l: (l[0], b, 0, i, 0)))
            out_shapes.append(jax.ShapeDtypeStruct((depth, nb, nh, n, HEAD_DIM), F32))
        else:
            out_specs.append(pl.BlockSpec((None, nh, tm, HEAD_DIM), lambda b, i, j, l: (b, 0, i, 0)))
            out_shapes.append(jax.ShapeDtypeStruct((nb, nh, n, HEAD_DIM), F32))
    stacks = list(stacks)
    stacked_outs = [k for k, is_stacked in enumerate(stacked) if is_stacked]
    assert len(stacks) == len(stacked_outs)
    n_fixed = 4
    return pl.pallas_call(
        functools.partial(_kv_proj_kernel, n_stacks=len(stacks), nh=nh),
        grid_spec=pltpu.PrefetchScalarGridSpec(
            num_scalar_prefetch=1,
            grid=(nb, nt, len(stacked)),
            in_specs=[
                pl.BlockSpec((tm, d_model), lambda b, i, j, l: (b * nt + i, 0)),
                pl.BlockSpec((None, 1, d_model), lambda b, i, j, l: (l[0], 0, 0)),
                pl.BlockSpec((None, d_model, tn), lambda b, i, j, l: (l[0], 0, j)),
            ] + [pl.BlockSpec(memory_space=pl.ANY)] * len(stacks),
            out_specs=out_specs,
            scratch_shapes=[pltpu.VMEM((tm, d_model), BF16)],
        ),
        out_shape=out_shapes,
        input_output_aliases={n_fixed + s: stacked_outs[s] for s in range(len(stacks))},
        compiler_params=_params(("parallel", "parallel", "arbitrary")),
        name="kv_proj",
    )(lidx, x2d, norm_g, w_bf, *stacks)


def _pool_kernel(l_ref, u_ref, ga_ref, halo_ref, wp_ref, ps_ref, o_ref, ext_ref, *, tm, pos0):
    i = pl.program_id(1)
    ext_ref[0:HALO, :] = halo_ref[...]
    ext_ref[HALO:, :] = u_ref[...]
    pos = pos0 + i * tm + lax.broadcasted_iota(jnp.int32, (tm, 1), 0)
    for g, w in enumerate(POOL_WINDOWS):
        sl = slice(g * POOL_GROUP, (g + 1) * POOL_GROUP)
        s = ext_ref[HALO:HALO + tm, sl]
        for d in range(1, w):
            s = s + ext_ref[HALO - d:HALO - d + tm, sl]
        cnt = jnp.minimum(w, pos + 1).astype(F32)
        pooled = s / cnt - u_ref[:, sl]
        y = _dot(pooled.astype(BF16), wp_ref[g].astype(BF16)) * ps_ref[:, sl]
        o_ref[:, sl] = (y * _silu(ga_ref[:, sl])).astype(o_ref.dtype)


def _pool(lidx, p2d, halo, w_pool, pool_scale, nb, n, tm, pos0, u_col, g_col, out_dtype):
    d_pool = POOL_GROUP * len(POOL_WINDOWS)
    nt = n // tm
    assert u_col % d_pool == 0 and g_col % d_pool == 0
    uc, gc = u_col // d_pool, g_col // d_pool
    kern = functools.partial(_pool_kernel, tm=tm, pos0=pos0)
    return pl.pallas_call(
        kern,
        grid_spec=pltpu.PrefetchScalarGridSpec(
            num_scalar_prefetch=1,
            grid=(nb, nt),
            in_specs=[
                pl.BlockSpec((tm, d_pool), lambda b, i, l: (b * nt + i, uc)),
                pl.BlockSpec((tm, d_pool), lambda b, i, l: (b * nt + i, gc)),
                pl.BlockSpec((None, None, HALO, d_pool), lambda b, i, l: (b, i, 0, 0)),
                pl.BlockSpec((None, len(POOL_WINDOWS), POOL_GROUP, POOL_GROUP), lambda b, i, l: (l[0], 0, 0, 0)),
                pl.BlockSpec((None, 1, d_pool), lambda b, i, l: (l[0], 0, 0)),
            ],
            out_specs=pl.BlockSpec((tm, d_pool), lambda b, i, l: (b * nt + i, 0)),
            scratch_shapes=[pltpu.VMEM((HALO + tm, d_pool), F32)],
        ),
        out_shape=jax.ShapeDtypeStruct((nb * n, d_pool), out_dtype),
        compiler_params=_params(("parallel", "arbitrary")),
        name="pool",
    )(lidx, p2d, p2d, halo, w_pool, pool_scale)


def _sb_prefill_kernel(l_ref, q_ref, k_ref, v_ref, g_ref, o_ref, acc_ref, cs_ref, *, tq, scale):
    i = pl.program_id(2)
    hps = SB_HEADS_PER_STEP
    qs = [q_ref[:, h * HEAD_DIM:(h + 1) * HEAD_DIM].astype(BF16) for h in range(hps)]
    row = lax.broadcasted_iota(jnp.int32, (tq, tq), 0)
    col = lax.broadcasted_iota(jnp.int32, (tq, tq), 1)
    later = jnp.where(row > col, 1.0, 0.0).astype(BF16)

    def logits(kj):
        off = pl.multiple_of(kj * tq, tq)
        return jnp.concatenate([_dot_nt(qs[h], k_ref[h, pl.ds(off, tq), :].astype(BF16)) for h in range(hps)],
                               axis=0) * scale

    def tiles(z, kj, diagonal):
        off = pl.multiple_of(kj * tq, tq)
        ls = _log_sigmoid(z)
        lk = ls - z
        if diagonal:
            keep = jnp.concatenate([col < row] * hps, axis=0)
            lk = jnp.where(keep, lk, 0.0)
        hi, lo = _split_bf16(lk)
        la = _dot(hi, later) + _dot(lo, later) + _lanes(cs_ref[...], tq)
        a = jnp.exp(ls + la)
        if diagonal:
            a = jnp.where(keep, a, 0.0)
        for h in range(hps):
            v = v_ref[h, pl.ds(off, tq), :].astype(BF16)
            acc_ref[h] += _dot(a[h * tq:(h + 1) * tq].astype(BF16), v)
        cs = cs_ref[...] + jnp.sum(lk, axis=1, keepdims=True)
        cs_ref[...] = cs
        return jnp.max(cs)

    acc_ref[...] = jnp.zeros_like(acc_ref)
    cs_ref[...] = jnp.zeros_like(cs_ref)
    z_next = logits(jnp.maximum(i - 1, 0))
    cs_max = tiles(logits(i), i, True)

    def more(state):
        t, cs_max, _ = state
        return jnp.logical_and(t < i, cs_max > UNDERFLOW_LOG)

    def body(state):
        t, _, z = state
        z_after = logits(jnp.maximum(i - 2 - t, 0))
        return t + 1, tiles(z, i - 1 - t, False), z_after

    lax.while_loop(more, body, (0, cs_max, z_next))
    gate = _silu(g_ref[...])
    for h in range(hps):
        sl = slice(h * HEAD_DIM, (h + 1) * HEAD_DIM)
        o_ref[:, sl] = (acc_ref[h] * gate[:, sl]).astype(o_ref.dtype)


def _sb_prefill(lidx, p2d, k_stack, v_stack, nb, n, nh, q_col, g_col, tq):
    hps = SB_HEADS_PER_STEP
    width = hps * HEAD_DIM
    nq = n // tq
    assert nh % hps == 0 and q_col % width == 0 and g_col % width == 0
    kern = functools.partial(_sb_prefill_kernel, tq=tq, scale=1.0 / math.sqrt(HEAD_DIM))
    qc, gc = q_col // width, g_col // width
    heads = pl.BlockSpec((None, None, hps, n, HEAD_DIM), lambda b, h, i, l: (l[0], b, h, 0, 0))
    return pl.pallas_call(
        kern,
        grid_spec=pltpu.PrefetchScalarGridSpec(
            num_scalar_prefetch=1,
            grid=(nb, nh // hps, nq),
            in_specs=[
                pl.BlockSpec((tq, width), lambda b, h, i, l: (b * nq + i, qc + h)),
                heads, heads,
                pl.BlockSpec((tq, width), lambda b, h, i, l: (b * nq + i, gc + h)),
            ],
            out_specs=pl.BlockSpec((tq, width), lambda b, h, i, l: (b * nq + i, h)),
            scratch_shapes=[pltpu.VMEM((hps, tq, HEAD_DIM), F32), pltpu.VMEM((hps * tq, LANES), F32)],
        ),
        out_shape=jax.ShapeDtypeStruct((nb * n, nh * HEAD_DIM), BF16),
        compiler_params=_params(("parallel", "parallel", "arbitrary")),
        name="sb_prefill",
    )(lidx, p2d, k_stack, v_stack, p2d)


def _top_k_lanes(gate, k):
    lane = lax.broadcasted_iota(jnp.int32, gate.shape, 1).astype(F32)
    sel = jnp.zeros(gate.shape, F32)
    for _ in range(k):
        m = jnp.max(gate, axis=1, keepdims=True)
        cand = jnp.where(gate == m, lane, float(LANES))
        cand = jnp.where(m > NEG_INF, cand, float(LANES))
        idx = jnp.min(cand, axis=1, keepdims=True)
        pick = lane == idx
        sel = jnp.where(pick, 1.0, sel)
        gate = jnp.where(pick, NEG_INF, gate)
    return sel


def _top_k_sublanes(gate_t, k, n_blocks):
    block = lax.broadcasted_iota(jnp.int32, gate_t.shape, 0)
    rank = jnp.zeros(gate_t.shape, F32)
    for b2 in range(n_blocks):
        other = jnp.broadcast_to(gate_t[b2:b2 + 1, :], gate_t.shape)
        wins_tie = jnp.where(block > b2, 1.0, 0.0)
        rank = rank + jnp.where(other > gate_t, 1.0, jnp.where(other == gate_t, wins_tie, 0.0))
    return jnp.where(jnp.logical_and(rank < k, gate_t > NEG_INF), 1.0, 0.0)


def _gate_scores(a, b):
    a_hi, a_lo = _split_bf16(a)
    b_hi, b_lo = _split_bf16(b)
    return _dot_nt(a_hi, b_hi) + _dot_nt(a_hi, b_lo) + _dot_nt(a_lo, b_hi)


def _moba_prefill_kernel(l_ref, q_ref, k_ref, v_ref, g_ref, bias_ref, qg_ref, kg_ref, *rest, n_stacks, nblk, scale):
    o_ref, kn_ref, km_ref, qn_ref, nc_ref, m_ref, l_scr, acc_ref = rest[n_stacks:]
    i = pl.program_id(2)
    hps = MOBA_HEADS_PER_STEP
    blk = MOBA_BLOCK
    n = nblk * blk
    row = lax.broadcasted_iota(jnp.int32, (blk, blk), 0)
    col = lax.broadcasted_iota(jnp.int32, (blk, blk), 1)
    block_id = lax.broadcasted_iota(jnp.int32, (LANES, blk), 0)

    @pl.when(i == 0)
    def _():
        eye = jnp.where(row == col, 1.0, 0.0).astype(BF16)
        gate_rows = -(-nblk // SUBLANES) * SUBLANES
        gate_block = lax.broadcasted_iota(jnp.int32, (gate_rows, n), 0)
        query_block = lax.shift_right_logical(lax.broadcasted_iota(jnp.int32, (gate_rows, n), 1), int(math.log2(blk)))
        km_ref[...] = jnp.zeros_like(km_ref)
        for h in range(hps):
            for b in range(nblk):
                kn = _rms_norm(k_ref[h, b * blk:(b + 1) * blk, :], kg_ref[...])
                kn_ref[h, b * blk:(b + 1) * blk, :] = kn
                km_ref[h, b:b + 1, :] = jnp.sum(kn, axis=0, keepdims=True) * (1.0 / blk)
            qf = _rms_norm(q_ref[:, h * HEAD_DIM:(h + 1) * HEAD_DIM], qg_ref[...])
            qn_ref[h] = qf.astype(BF16)
            gate_t = jnp.where(gate_block < query_block, _gate_scores(km_ref[h, 0:gate_rows, :], qf), NEG_INF)
            mask_t = jnp.where(_top_k_sublanes(gate_t, MOBA_TOPK, nblk) > 0.0, 0.0, NEG_BIG)
            if gate_rows < LANES:
                mask_t = jnp.concatenate([mask_t, jnp.zeros((LANES - gate_rows, n), F32)], axis=0)
            mask_t = mask_t.astype(BF16)
            for t in range(nblk):
                nc_ref[t, h * blk:(h + 1) * blk, :] = _dot_nt(eye, mask_t[:, t * blk:(t + 1) * blk]).astype(BF16)

    row0 = pl.multiple_of(i * blk, blk)
    qs = [qn_ref[h, pl.ds(row0, blk), :] for h in range(hps)]
    not_chosen = nc_ref[i]

    m_ref[...] = jnp.full_like(m_ref, NEG_INF)
    l_scr[...] = jnp.zeros_like(l_scr)
    acc_ref[...] = jnp.zeros_like(acc_ref)

    def raw_scores(j):
        off = pl.multiple_of(j * blk, blk)
        return jnp.concatenate([_dot_nt(qs[h], kn_ref[h, pl.ds(off, blk), :].astype(BF16)) * scale + bias_ref[h, i - j]
                                for h in range(hps)], axis=0)

    s_own = jnp.where(jnp.concatenate([col <= row] * hps, axis=0), raw_scores(i), NEG_INF)

    def masked(raw, j):
        return raw + _dot(not_chosen, jnp.where(block_id == j, 1.0, 0.0).astype(BF16))

    def scores(j):
        return masked(raw_scores(j), j)

    def accumulate(s, j):
        off = pl.multiple_of(j * blk, blk)
        m_prev = m_ref[...]
        m_new = jnp.maximum(m_prev, jnp.max(s, axis=1, keepdims=True))
        alpha = jnp.exp(m_prev - m_new)
        p = jnp.exp(s - _lanes(m_new, blk))
        l_scr[...] = alpha * l_scr[...] + jnp.sum(p, axis=1, keepdims=True)
        for h in range(hps):
            rows = slice(h * blk, (h + 1) * blk)
            v = v_ref[h, pl.ds(off, blk), :].astype(BF16)
            acc_ref[h] = alpha[rows] * acc_ref[h] + _dot(p[rows].astype(BF16), v)
        m_ref[...] = m_new

    s_first = scores(0)
    accumulate(s_own, i)

    def body(j, s):
        s_next = scores(jnp.minimum(j + 1, jnp.maximum(i - 1, 0)))
        accumulate(s, j)
        return s_next

    lax.fori_loop(0, i, body, s_first)
    gate = _silu(g_ref[...])
    inv_l = 1.0 / l_scr[...]
    for h in range(hps):
        sl = slice(h * HEAD_DIM, (h + 1) * HEAD_DIM)
        o_ref[:, sl] = (acc_ref[h] * inv_l[h * blk:(h + 1) * blk] * gate[:, sl]).astype(o_ref.dtype)


def _moba_prefill(lidx, p2d, k_raw, v_stack, k_stack, bias_tiles, q_gain, k_gain, depth, nb, n, nh, q_col, g_col):
    hps = MOBA_HEADS_PER_STEP
    width = hps * HEAD_DIM
    blk = MOBA_BLOCK
    assert n % blk == 0
    nq = n // blk
    assert nq <= LANES
    assert nh % hps == 0 and q_col % width == 0 and g_col % width == 0
    extend = [k_stack]
    kern = functools.partial(_moba_prefill_kernel, n_stacks=len(extend), nblk=nq, scale=1.0 / math.sqrt(HEAD_DIM))
    qc, gc = q_col // width, g_col // width
    gain = pl.BlockSpec((None, 1, HEAD_DIM), lambda h, b, i, l: (l[0], 0, 0))
    layer_heads = pl.BlockSpec((None, None, hps, n, HEAD_DIM), lambda h, b, i, l: (l[0], b, h, 0, 0))
    n_fixed = 8
    return pl.pallas_call(
        kern,
        grid_spec=pltpu.PrefetchScalarGridSpec(
            num_scalar_prefetch=1,
            grid=(nh // hps, nb, nq),
            in_specs=[
                pl.BlockSpec((n, width), lambda h, b, i, l: (b, qc + h)),
                pl.BlockSpec((None, hps, n, HEAD_DIM), lambda h, b, i, l: (b, h, 0, 0)),
                layer_heads,
                pl.BlockSpec((blk, width), lambda h, b, i, l: (b * nq + i, gc + h)),
                pl.BlockSpec((hps, nq, blk, blk), lambda h, b, i, l: (h, 0, 0, 0)),
                gain, gain,
            ] + [pl.BlockSpec(memory_space=pl.ANY)] * len(extend),
            out_specs=[
                pl.BlockSpec((blk, width), lambda h, b, i, l: (b * nq + i, h)),
                layer_heads,
            ],
            scratch_shapes=[pltpu.VMEM((hps, LANES, HEAD_DIM), F32),
                            pltpu.VMEM((hps, n, HEAD_DIM), BF16),
                            pltpu.VMEM((nq, hps * blk, LANES), BF16),
                            pltpu.VMEM((hps * blk, LANES), F32), pltpu.VMEM((hps * blk, LANES), F32),
                            pltpu.VMEM((hps, blk, HEAD_DIM), F32)],
        ),
        out_shape=[jax.ShapeDtypeStruct((nb * n, nh * HEAD_DIM), BF16),
                   jax.ShapeDtypeStruct((depth, nb, nh, n, HEAD_DIM), F32)],
        input_output_aliases={n_fixed: 1},
        compiler_params=_params(("parallel", "parallel", "arbitrary")),
        name="moba_prefill",
    )(lidx, p2d, k_raw, v_stack, p2d, bias_tiles, q_gain, k_gain, *extend)


def _head_queries(q_ref, nh):
    return [q_ref[:, h * HEAD_DIM:(h + 1) * HEAD_DIM] for h in range(nh)]


def _stacked_scores(qs, k_ref):
    return jnp.concatenate([_dot_nt(q, k_ref[h].astype(BF16)) for h, q in enumerate(qs)], axis=0)


def _stacked_values(p, v_ref, nh, dn):
    return jnp.concatenate(
        [_dot(p[h * dn:(h + 1) * dn].astype(BF16), v_ref[h].astype(BF16)) for h in range(nh)], axis=0)


def _gated_heads(acc, g, nh, dn):
    return jnp.concatenate([acc[h * dn:(h + 1) * dn] for h in range(nh)], axis=1) * _silu(g)


def _page_index_map(r, pps, n_pages, reverse):
    def index_map(b, s, l, pt):
        page = s * pps + r
        if reverse:
            page = n_pages - 1 - page
        return (l[0], pt[b * n_pages + page], 0, 0, 0)

    return index_map


def _page_specs(pps, nh, n_pages, reverse):
    return [pl.BlockSpec((None, None, nh, PAGE_SIZE, HEAD_DIM), _page_index_map(r, pps, n_pages, reverse))
            for r in range(pps)]


def _sb_decode_kernel(l_ref, pt_ref, q_ref, kn_ref, vn_ref, g_ref, ck_hbm, cv_hbm, o_ref,
                      kbuf, vbuf, sems, acc_ref, cs_ref, *, nh, dn, n_pages, scale):
    ppc = SB_PAGES_PER_CHUNK
    n_chunks = n_pages // ppc
    b = pl.program_id(0)
    rows = nh * dn

    def page_copies(c, slot):
        copies = []
        for r in range(ppc):
            page = pt_ref[b * n_pages + (n_pages - 1 - (c * ppc + r))]
            copies.append(pltpu.make_async_copy(ck_hbm.at[l_ref[0], page], kbuf.at[slot, r], sems.at[0, slot]))
            copies.append(pltpu.make_async_copy(cv_hbm.at[l_ref[0], page], vbuf.at[slot, r], sems.at[1, slot]))
        return copies

    qs = [q.astype(BF16) for q in _head_queries(q_ref, nh)]
    row = lax.broadcasted_iota(jnp.int32, (PAGE_SIZE, PAGE_SIZE), 0)
    col = lax.broadcasted_iota(jnp.int32, (PAGE_SIZE, PAGE_SIZE), 1)
    later = jnp.where(row > col, 1.0, 0.0).astype(BF16)

    def pages(kp_refs, vp_refs, keep):
        z = jnp.concatenate([_stacked_scores(qs, kp) for kp in kp_refs], axis=0) * scale
        ls = _log_sigmoid(z)
        lk = ls - z
        if keep is not None:
            lk = jnp.where(keep, lk, 0.0)
        hi, lo = _split_bf16(lk)
        within = _dot(hi, later) + _dot(lo, later)
        tot = jnp.sum(lk, axis=1, keepdims=True)
        carry = cs_ref[...]
        carries = []
        for r in range(len(kp_refs)):
            carries.append(carry)
            carry = carry + tot[r * rows:(r + 1) * rows]
        cs_ref[...] = carry
        a = jnp.exp(ls + within + jnp.concatenate(carries, axis=0))
        if keep is not None:
            a = jnp.where(keep, a, 0.0)
        upd = _stacked_values(a[0:rows], vp_refs[0], nh, dn)
        for r in range(1, len(kp_refs)):
            upd = upd + _stacked_values(a[r * rows:(r + 1) * rows], vp_refs[r], nh, dn)
        acc_ref[...] += upd
        return jnp.max(carry)

    for cp in page_copies(0, 0):
        cp.start()
    acc_ref[...] = jnp.zeros_like(acc_ref)
    cs_ref[...] = jnp.zeros_like(cs_ref)
    tok = lax.broadcasted_iota(jnp.int32, (rows, PAGE_SIZE), 0) & (dn - 1)
    key = lax.broadcasted_iota(jnp.int32, (rows, PAGE_SIZE), 1)
    cs_max = pages([kn_ref], [vn_ref], key < tok)

    def more(state):
        c, cs_max = state
        return jnp.logical_and(c < n_chunks, cs_max > UNDERFLOW_LOG)

    def trip(state):
        c, _ = state
        slot = c & 1

        @pl.when(c + 1 < n_chunks)
        def _():
            for cp in page_copies(c + 1, 1 - slot):
                cp.start()

        for cp in page_copies(c, slot):
            cp.wait()
        cs_max = pages([kbuf.at[slot, r] for r in range(ppc)], [vbuf.at[slot, r] for r in range(ppc)], None)
        return c + 1, cs_max

    c_end, _ = lax.while_loop(more, trip, (0, cs_max))

    @pl.when(c_end < n_chunks)
    def _():
        for cp in page_copies(c_end, c_end & 1):
            cp.wait()

    o_ref[...] = _gated_heads(acc_ref[...], g_ref[...], nh, dn).astype(o_ref.dtype)


def _sb_decode(lidx, pt_flat, q, k_new, v_new, g, cache_k, cache_v, nb, dn, nh, n_pages):
    ppc = SB_PAGES_PER_CHUNK
    assert n_pages % ppc == 0 and dn == SUBLANES
    width = nh * HEAD_DIM
    kern = functools.partial(_sb_decode_kernel, nh=nh, dn=dn, n_pages=n_pages, scale=1.0 / math.sqrt(HEAD_DIM))
    seq = lambda rows: pl.BlockSpec((None, rows, width), lambda b, l, pt: (b, 0, 0))
    new = pl.BlockSpec((None, nh, PAGE_SIZE, HEAD_DIM), lambda b, l, pt: (b, 0, 0, 0))
    hbm = pl.BlockSpec(memory_space=pl.ANY)
    chunk = (2, ppc, nh, PAGE_SIZE, HEAD_DIM)
    return pl.pallas_call(
        kern,
        grid_spec=pltpu.PrefetchScalarGridSpec(
            num_scalar_prefetch=2,
            grid=(nb,),
            in_specs=[seq(dn), new, new, seq(dn), hbm, hbm],
            out_specs=seq(dn),
            scratch_shapes=[pltpu.VMEM(chunk, F32), pltpu.VMEM(chunk, F32), pltpu.SemaphoreType.DMA((2, 2)),
                            pltpu.VMEM((nh * dn, HEAD_DIM), F32), pltpu.VMEM((nh * dn, LANES), F32)],
        ),
        out_shape=jax.ShapeDtypeStruct((nb, dn, width), F32),
        compiler_params=_params(("arbitrary",)),
        name="sb_decode",
    )(lidx, pt_flat, q, k_new, v_new, g, cache_k, cache_v)


def _moba_scores_kernel(l_ref, pt_ref, q_ref, *rest, nh, dn):
    pps = MOBA_PAGES_PER_STEP
    k_refs = rest[:pps]
    s_ref, km_ref = rest[pps:]
    s_idx = pl.program_id(1)
    qs = [q.astype(BF16) for q in _head_queries(q_ref, nh)]
    pages_per_block = MOBA_BLOCK // PAGE_SIZE
    blocks_per_step = pps // pages_per_block

    @pl.when(s_idx == 0)
    def _():
        km_ref[...] = jnp.zeros_like(km_ref)

    sub = lax.broadcasted_iota(jnp.int32, (blocks_per_step, HEAD_DIM), 0)
    for h in range(nh):
        means = jnp.zeros((blocks_per_step, HEAD_DIM), F32)
        for c in range(blocks_per_step):
            tot = jnp.zeros((1, HEAD_DIM), F32)
            for r in range(c * pages_per_block, (c + 1) * pages_per_block):
                kp = k_refs[r][h]
                s_ref[h * dn:(h + 1) * dn, r * PAGE_SIZE:(r + 1) * PAGE_SIZE] = _dot_nt(qs[h], kp.astype(BF16))
                tot = tot + jnp.sum(kp, axis=0, keepdims=True)
            means = jnp.where(sub == c, tot * (1.0 / MOBA_BLOCK), means)
        km_ref[h, pl.ds(pl.multiple_of(s_idx * blocks_per_step, blocks_per_step), blocks_per_step), :] = means


def _moba_scores(lidx, pt_flat, q, cache_k, nb, dn, nh, n_pages):
    pps = MOBA_PAGES_PER_STEP
    assert pps // (MOBA_BLOCK // PAGE_SIZE) == SUBLANES and n_pages % pps == 0
    width = nh * HEAD_DIM
    past = n_pages * PAGE_SIZE
    assert past // MOBA_BLOCK <= LANES
    kern = functools.partial(_moba_scores_kernel, nh=nh, dn=dn)
    return pl.pallas_call(
        kern,
        grid_spec=pltpu.PrefetchScalarGridSpec(
            num_scalar_prefetch=2,
            grid=(nb, n_pages // pps),
            in_specs=[pl.BlockSpec((None, dn, width), lambda b, s, l, pt: (b, 0, 0))]
            + _page_specs(pps, nh, n_pages, False),
            out_specs=[
                pl.BlockSpec((None, nh * dn, pps * PAGE_SIZE), lambda b, s, l, pt: (b, 0, s)),
                pl.BlockSpec((None, nh, LANES, HEAD_DIM), lambda b, s, l, pt: (b, 0, 0, 0)),
            ],
        ),
        out_shape=[jax.ShapeDtypeStruct((nb, nh * dn, past), F32),
                   jax.ShapeDtypeStruct((nb, nh, LANES, HEAD_DIM), F32)],
        compiler_params=_params(("parallel", "arbitrary")),
        name="moba_scores",
    )(lidx, pt_flat, q, *([cache_k] * pps))


def _moba_values_kernel(l_ref, pt_ref, q_ref, km_ref, s_ref, bias_ref, kn_ref, vn_ref, bown_ref, g_ref, *rest,
                        nh, dn, n_past_blocks, scale):
    pps = MOBA_PAGES_PER_STEP
    v_refs = rest[:pps]
    o_ref, sel_ref, m_ref, l_scr, acc_ref = rest[pps:]
    s_idx = pl.program_id(1)
    rows = nh * dn
    width = pps * PAGE_SIZE
    blocks_per_step = width // MOBA_BLOCK

    @pl.when(s_idx == 0)
    def _():
        qf = _head_queries(q_ref, nh)
        lane = lax.broadcasted_iota(jnp.int32, (rows, LANES), 1)
        gate = jnp.concatenate([_gate_scores(qf[h], km_ref[h]) for h in range(nh)], axis=0)
        gate = jnp.where(lane < n_past_blocks, gate, NEG_INF)
        sel_ref[...] = _top_k_lanes(gate, MOBA_TOPK)
        tok = lax.broadcasted_iota(jnp.int32, (rows, PAGE_SIZE), 0) & (dn - 1)
        key = lax.broadcasted_iota(jnp.int32, (rows, PAGE_SIZE), 1)
        s_own = _stacked_scores([q.astype(BF16) for q in qf], kn_ref) * scale + bown_ref[...]
        s_own = jnp.where(key <= tok, s_own, NEG_INF)
        m0 = jnp.max(s_own, axis=1, keepdims=True)
        p_own = jnp.exp(s_own - m0)
        m_ref[...] = jnp.broadcast_to(m0, m_ref.shape)
        l_scr[...] = jnp.broadcast_to(jnp.sum(p_own, axis=1, keepdims=True), l_scr.shape)
        acc_ref[...] = _stacked_values(p_own, vn_ref, nh, dn)

    key_in_step = lax.broadcasted_iota(jnp.int32, (LANES, width), 1)
    blk_of_key = lax.shift_right_logical(key_in_step, int(math.log2(MOBA_BLOCK))) + s_idx * blocks_per_step
    indicator = jnp.where(lax.broadcasted_iota(jnp.int32, (LANES, width), 0) == blk_of_key, 1.0, 0.0).astype(BF16)
    chosen = _dot(sel_ref[...].astype(BF16), indicator) > 0.5
    sm = jnp.where(chosen, s_ref[...] * scale + bias_ref[...], NEG_INF)
    m_prev = m_ref[...]
    m_new = jnp.maximum(m_prev, jnp.max(sm, axis=1, keepdims=True))
    alpha = jnp.exp(m_prev - m_new)
    p = jnp.exp(sm - _lanes(m_new, width))
    l_scr[...] = alpha * l_scr[...] + jnp.sum(p, axis=1, keepdims=True)
    upd = _stacked_values(p[:, 0:PAGE_SIZE], v_refs[0], nh, dn)
    for r in range(1, pps):
        upd = upd + _stacked_values(p[:, r * PAGE_SIZE:(r + 1) * PAGE_SIZE], v_refs[r], nh, dn)
    acc_ref[...] = alpha * acc_ref[...] + upd
    m_ref[...] = m_new

    @pl.when(s_idx == pl.num_programs(1) - 1)
    def _():
        o_ref[...] = _gated_heads(acc_ref[...] / l_scr[...], g_ref[...], nh, dn).astype(o_ref.dtype)


def _moba_values(lidx, pt_flat, q, km, scores, bias_past, k_new, v_new, bias_own, g, cache_v,
                 nb, dn, nh, n_pages):
    pps = MOBA_PAGES_PER_STEP
    width = nh * HEAD_DIM
    rows = nh * dn
    past = n_pages * PAGE_SIZE
    n_past_blocks = past // MOBA_BLOCK
    assert n_past_blocks <= LANES and km.shape[2] == LANES and n_pages % pps == 0
    assert (pps * PAGE_SIZE) % MOBA_BLOCK == 0
    kern = functools.partial(_moba_values_kernel, nh=nh, dn=dn, n_past_blocks=n_past_blocks,
                             scale=1.0 / math.sqrt(HEAD_DIM))
    seq = lambda r, c: pl.BlockSpec((None, r, c), lambda b, s, l, pt: (b, 0, 0))
    per_head = lambda r: pl.BlockSpec((None, nh, r, HEAD_DIM), lambda b, s, l, pt: (b, 0, 0, 0))
    return pl.pallas_call(
        kern,
        grid_spec=pltpu.PrefetchScalarGridSpec(
            num_scalar_prefetch=2,
            grid=(nb, n_pages // pps),
            in_specs=[seq(dn, width), per_head(LANES),
                      pl.BlockSpec((None, rows, pps * PAGE_SIZE), lambda b, s, l, pt: (b, 0, s)),
                      pl.BlockSpec((rows, pps * PAGE_SIZE), lambda b, s, l, pt: (0, s)),
                      per_head(PAGE_SIZE), per_head(PAGE_SIZE),
                      pl.BlockSpec((rows, PAGE_SIZE), lambda b, s, l, pt: (0, 0)),
                      seq(dn, width)] + _page_specs(pps, nh, n_pages, False),
            out_specs=seq(dn, width),
            scratch_shapes=[pltpu.VMEM((rows, LANES), F32), pltpu.VMEM((rows, LANES), F32),
                            pltpu.VMEM((rows, LANES), F32), pltpu.VMEM((rows, HEAD_DIM), F32)],
        ),
        out_shape=jax.ShapeDtypeStruct((nb, dn, width), F32),
        compiler_params=_params(("parallel", "arbitrary")),
        name="moba_values",
    )(lidx, pt_flat, q, km, scores, bias_past, k_new, v_new, bias_own, g, *([cache_v] * pps))


def _out_proj_kernel(l_ref, x_ref, a_ref, b_ref, c_ref, *rest):
    w_refs, o_ref = rest[:-1], rest[-1]
    acc, slab = None, 0
    for part in (a_ref, b_ref, c_ref):
        for c0 in range(0, part.shape[1], OUT_PROJ_K):
            term = _dot(part[:, c0:c0 + OUT_PROJ_K].astype(BF16), w_refs[slab][...])
            acc = term if acc is None else acc + term
            slab += 1
    o_ref[...] = x_ref[...] + acc


def _out_proj(lidx, x2d, a, b, c, w_bf, tm, in_place):
    T, d_model = x2d.shape
    widths = (a.shape[1], b.shape[1], c.shape[1])
    assert T % tm == 0 and sum(widths) == w_bf.shape[1] and all(w % OUT_PROJ_K == 0 for w in widths)
    act = lambda arr: pl.BlockSpec((tm, arr.shape[1]), lambda i, l: (i, 0))
    slabs = [pl.BlockSpec((None, OUT_PROJ_K, d_model), lambda i, l, k=k: (l[0], k, 0))
             for k in range(w_bf.shape[1] // OUT_PROJ_K)]
    return pl.pallas_call(
        _out_proj_kernel,
        grid_spec=pltpu.PrefetchScalarGridSpec(
            num_scalar_prefetch=1,
            grid=(T // tm,),
            in_specs=[act(x2d), act(a), act(b), act(c)] + slabs,
            out_specs=pl.BlockSpec((tm, d_model), lambda i, l: (i, 0)),
        ),
        out_shape=jax.ShapeDtypeStruct((T, d_model), F32),
        input_output_aliases={1: 0} if in_place else {},
        compiler_params=_params(("parallel",)),
        name="out_proj",
    )(lidx, x2d, a, b, c, *([w_bf] * len(slabs)))


def _rel_bucket_np(n):
    n = np.maximum(n, 0)
    max_exact = REL_BUCKETS // 2
    nf = np.maximum(n, 1).astype(np.float64)
    large = max_exact + (np.log(nf / max_exact) / math.log(REL_MAX_DIST / max_exact)
                         * (REL_BUCKETS - max_exact)).astype(np.int32)
    large = np.minimum(large, REL_BUCKETS - 1)
    return np.where(n < max_exact, n, large).astype(np.int32)


def _toeplitz_kernel(gen_ref, o_ref):
    blk = MOBA_BLOCK
    wide = jnp.broadcast_to(gen_ref[...], (blk, 2 * blk))
    o_ref[...] = pltpu.roll(wide, 0, 1, stride=1, stride_axis=0)[:, :blk]


def _take_static(table_t, idx):
    idx = np.asarray(idx)
    flat = idx.reshape(-1)
    starts = np.concatenate([[0], np.flatnonzero(np.diff(flat)) + 1, [flat.size]])
    rows = table_t.shape[0]
    runs = [jnp.broadcast_to(table_t[:, int(flat[s]):int(flat[s]) + 1], (rows, int(e - s)))
            for s, e in zip(starts[:-1], starts[1:])]
    return jnp.concatenate(runs, axis=1).reshape((rows,) + idx.shape)


def _toeplitz_bias_tiles(bias_t, buckets, nq):
    blk = MOBA_BLOCK
    nh = bias_t.shape[0]
    m = np.arange(2 * blk)
    d = np.arange(nq)[:, None] * blk
    gen_idx = np.where(m[None, :] <= blk, np.maximum(d - m[None, :], 0), d + 2 * blk - m[None, :])
    gen = _take_static(bias_t, buckets[gen_idx])[:, :, None, :]
    return pl.pallas_call(
        _toeplitz_kernel,
        grid=(nh, nq),
        in_specs=[pl.BlockSpec((None, None, 1, 2 * blk), lambda h, d: (h, d, 0, 0))],
        out_specs=pl.BlockSpec((None, None, blk, blk), lambda h, d: (h, d, 0, 0)),
        out_shape=jax.ShapeDtypeStruct((nh, nq, blk, blk), F32),
        compiler_params=_params(("parallel", "parallel")),
        name="bias_tiles",
    )(gen)


def _decode_bias_kernel(gen_ref, o_ref):
    dn, past = o_ref.shape
    wide = jnp.broadcast_to(gen_ref[...], (dn, gen_ref.shape[-1]))
    o_ref[...] = pltpu.roll(wide, 0, 1, stride=1, stride_axis=0)[:, :past]


def _decode_bias_rows(bias_t, buckets, past, dn):
    nh = bias_t.shape[0]
    width = past + LANES
    j = np.arange(width)
    dist = np.where(j <= past, past - j, np.minimum(past + width - j, past + dn))
    gen = _take_static(bias_t, buckets[dist])[:, None, :]
    return pl.pallas_call(
        _decode_bias_kernel,
        grid=(nh,),
        in_specs=[pl.BlockSpec((None, 1, width), lambda h: (h, 0, 0))],
        out_specs=pl.BlockSpec((dn, past), lambda h: (h, 0)),
        out_shape=jax.ShapeDtypeStruct((nh * dn, past), F32),
        compiler_params=_params(("parallel",)),
        name="decode_bias",
    )(gen)


def _pool_halo(u3, hist, tm):
    nb, n, d = u3.shape
    nt = n // tm
    first = jnp.pad(hist, ((0, 0), (HALO - POOL_HIST, 0), (0, 0)))[:, None]
    if nt == 1:
        return first
    tails = u3.reshape(nb, nt, tm, d)[:, :-1, tm - HALO:, :]
    return jnp.concatenate([first, tails], axis=1)


def kernel(x_prompt, x_sample, cache_sb_k, cache_sb_v, cache_moba_k, cache_moba_v, state_pool, page_table,
           norm_g, w_in, w_pool, pool_scale, q_norm_g, k_norm_g, w_out, rel_bias):
    nb, n, d_model = x_prompt.shape
    db, dn, _ = x_sample.shape
    depth, n_phys, page_size, nh, hd = cache_sb_k.shape
    n_pages = page_table.shape[1]
    past = n_pages * page_size
    assert page_size == PAGE_SIZE and hd == HEAD_DIM
    d_pool = POOL_GROUP * len(POOL_WINDOWS)
    d_att = nh * HEAD_DIM
    d_proj = w_in.shape[-1]
    assert d_proj == 2 * d_pool + 8 * d_att
    seg = lambda t: slice(2 * d_pool + t * d_att, 2 * d_pool + (t + 1) * d_att)
    w_main = jnp.concatenate([w_in[..., seg(t)] for t in (0, 3, 4, 7)] + [w_in[..., :2 * d_pool]], axis=-1).astype(BF16)
    w_kv = jnp.concatenate([w_in[..., seg(t)] for t in (1, 2, 5, 6)], axis=-1).astype(BF16)
    m_qb, m_gb, m_qc, m_gc = (t * d_att for t in range(4))
    m_u, m_ga = 4 * d_att, 4 * d_att + d_pool
    zeros_att = jnp.zeros((depth, d_att), F32)
    g_main = jnp.concatenate([jnp.zeros((depth, m_qc), F32), jnp.tile(q_norm_g, (1, nh)),
                              jnp.zeros((depth, d_att + 2 * d_pool), F32)], axis=-1)[:, None]
    g_kv = jnp.concatenate([zeros_att, zeros_att, jnp.tile(k_norm_g, (1, nh)), zeros_att], axis=-1)[:, None]
    norm_main = (m_qc, m_qc + d_att)
    norm_kv = (2 * d_att, 3 * d_att)
    q_gain, k_gain = q_norm_g[:, None, :], k_norm_g[:, None, :]

    w_out_bf = w_out.astype(BF16)
    norm_g3 = norm_g[:, None, :]
    pool_scale3 = pool_scale[:, None, :]
    caches = [jnp.transpose(c, (0, 1, 3, 2, 4)) for c in (cache_sb_k, cache_sb_v, cache_moba_k, cache_moba_v)]
    pt_flat = page_table.reshape(-1).astype(jnp.int32)

    n_dist = past + dn + 1
    buckets = _rel_bucket_np(np.arange(n_dist))
    bias_t = rel_bias.astype(F32).T
    bias_tiles = _toeplitz_bias_tiles(bias_t, buckets, n // MOBA_BLOCK)
    bias_past = _decode_bias_rows(bias_t, buckets, past, dn)
    dist_own = np.maximum(np.arange(dn)[:, None] - np.arange(PAGE_SIZE)[None, :], 0)
    bias_own = _take_static(bias_t, buckets[dist_own]).reshape(nh * dn, PAGE_SIZE)

    tm_proj = min(n, IN_PROJ_ROWS)
    tm_p = min(n, ROW_TILE)
    hist0 = jnp.zeros((nb, POOL_HIST, d_pool), F32)

    yp, ys = x_prompt.reshape(nb * n, d_model), x_sample.reshape(db * dn, d_model)
    ts = db * dn
    new_stack = lambda: jnp.zeros((depth, nb, nh, n, HEAD_DIM), F32)
    kv_stacks = (new_stack(), new_stack(), new_stack())
    mk_stack = new_stack()
    pool_p, kv_s, pool_s = [], [], []
    for l in range(depth):
        lidx = jnp.full((1,), l, jnp.int32)

        p = _in_proj(lidx, yp, norm_g3, w_main, nb, n, tm_proj, IN_PROJ_COLS)
        sbk, sbv, kc_raw, mbv = _kv_proj(lidx, yp, norm_g3, w_kv, kv_stacks, depth, nb, n, nh, tm_proj,
                                         (True, True, False, True))
        kv_stacks = (sbk, sbv, mbv)
        u3 = p.reshape(nb, n, -1)[:, :, m_u:m_u + d_pool]
        a = _pool(lidx, p, _pool_halo(u3, hist0, tm_p), w_pool, pool_scale3, nb, n, tm_p, 0, m_u, m_ga, BF16)
        bmix = _sb_prefill(lidx, p, sbk, sbv, nb, n, nh, m_qb, m_gb, SB_TILE)
        cmix, mk_stack = _moba_prefill(lidx, p, kc_raw, mbv, mk_stack, bias_tiles, q_gain, k_gain,
                                       depth, nb, n, nh, m_qc, m_gc)
        yp = _out_proj(lidx, yp, a, bmix, cmix, w_out_bf, tm_p, in_place=l > 0)
        pool_p.append(u3[:, n - POOL_HIST:])

        ps = _in_proj(lidx, ys, norm_g3, w_main, 1, ts, ts, IN_PROJ_COLS, g_main, norm_main)
        kvs = _in_proj(lidx, ys, norm_g3, w_kv, 1, ts, ts, d_att, g_kv, norm_kv)
        ps3 = ps.reshape(db, dn, -1)
        kvs5 = jnp.transpose(kvs.reshape(db, dn, 4, nh, HEAD_DIM), (2, 0, 3, 1, 4))
        hist_s = state_pool[l]
        us3 = ps3[:, :, m_u:m_u + d_pool]
        a_s = _pool(lidx, ps, _pool_halo(us3, hist_s, dn), w_pool, pool_scale3, db, dn, dn, past, m_u, m_ga, F32)
        col = lambda c: ps3[:, :, c:c + d_att]
        new_page = lambda t: jnp.pad(kvs5[t], ((0, 0), (0, 0), (0, PAGE_SIZE - dn), (0, 0)))
        b_s = _sb_decode(lidx, pt_flat, col(m_qb), new_page(0), new_page(1), col(m_gb),
                         caches[0], caches[1], db, dn, nh, n_pages)
        scores, km = _moba_scores(lidx, pt_flat, col(m_qc), caches[2], db, dn, nh, n_pages)
        c_s = _moba_values(lidx, pt_flat, col(m_qc), km, scores, bias_past, new_page(2), new_page(3),
                           bias_own, col(m_gc), caches[3], db, dn, nh, n_pages)
        ys = _out_proj(lidx, ys, a_s, b_s.reshape(ts, d_att), c_s.reshape(ts, d_att), w_out_bf, ts, in_place=l > 0)
        kv_s.append(kvs5)
        pool_s.append(jnp.concatenate([hist_s, us3], axis=1)[:, -POOL_HIST:])

    rows = lambda t: jnp.transpose(t, (0, 1, 3, 2, 4))
    sbk, sbv, mbv = kv_stacks
    kv_s = jnp.stack(kv_s, axis=1)
    return ((yp.reshape(nb, n, d_model), ys.reshape(db, dn, d_model))
            + (rows(sbk), rows(sbv), rows(mk_stack), rows(mbv), jnp.stack(pool_p))
            + tuple(rows(kv_s[t]) for t in range(4)) + (jnp.stack(pool_s),))
```

```python
import functools
import math

import numpy as np
import jax
import jax.numpy as jnp
from jax import lax
from jax.experimental import pallas as pl
from jax.experimental.pallas import tpu as pltpu

F32 = jnp.float32
BF16 = jnp.bfloat16

HEAD_DIM = 128
POOL_WINDOWS = (2, 4, 8, 16)
POOL_GROUP = 128
POOL_HIST = max(POOL_WINDOWS) - 1
HALO = POOL_HIST + 1
PAGE_SIZE = 128
MOBA_BLOCK = 256
MOBA_TOPK = 3
REL_BUCKETS = 32
REL_MAX_DIST = 128
EPS = 1e-6
LANES = 128
SUBLANES = 8
VMEM_LIMIT_BYTES = 56 * 1024 * 1024
IN_PROJ_ROWS = 1024
IN_PROJ_COLS = 1024
ROW_TILE = 512
COL_SLAB = 256
OUT_PROJ_K = 256
SB_TILE = 256
SB_HEADS_PER_STEP = 6
MOBA_HEADS_PER_STEP = 2
SB_PAGES_PER_CHUNK = 4
MOBA_PAGES_PER_STEP = 16
NEG_INF = float("-inf")
NEG_BIG = -1e30
UNDERFLOW_LOG = -104.0
LOG2E = 1.4426950408889634
LN2 = 0.6931471805599453

NT_DIMS = (((1,), (1,)), ((), ()))


def _dot(a, b):
    return jnp.dot(a, b, preferred_element_type=F32)


def _dot_nt(a, b):
    return lax.dot_general(a, b, NT_DIMS, preferred_element_type=F32)


def _split_bf16(x):
    hi = x.astype(BF16)
    lo = (x - hi.astype(F32)).astype(BF16)
    return hi, lo


def _log_sigmoid(z):
    return jnp.minimum(z, 0.0) - jnp.log(1.0 + jnp.exp(-jnp.abs(z)))


def _silu(g):
    return g / (1.0 + jnp.exp(-g))


def _rms_norm(x, gain):
    ms = jnp.mean(x * x, axis=-1, keepdims=True)
    return x * lax.rsqrt(ms + EPS) * gain


def _lanes(x, width):
    return x if width == LANES else jnp.concatenate([x] * (width // LANES), axis=1)


def _params(semantics):
    return pltpu.CompilerParams(dimension_semantics=semantics, vmem_limit_bytes=VMEM_LIMIT_BYTES)


def _weight_slab_specs(n_slabs, d_model):
    return [pl.BlockSpec((None, d_model, COL_SLAB), lambda b, i, j, l, cm, k=k: (l[0], 0, cm[j * n_slabs + k]))
            for k in range(n_slabs)]


def _in_proj_kernel(l_ref, cm_ref, x_ref, g_ref, *rest, n_slabs, norm_heads):
    w_refs, rest = rest[:n_slabs], rest[n_slabs:]
    if norm_heads is None:
        o_ref, h_ref = rest
    else:
        qkg_ref, o_ref, h_ref = rest
    j = pl.program_id(2)

    @pl.when(j == 0)
    def _():
        h_ref[...] = _rms_norm(x_ref[...], g_ref[...]).astype(BF16)

    heads_per_slab = COL_SLAB // HEAD_DIM
    for k, w_ref in enumerate(w_refs):
        acc = _dot(h_ref[...], w_ref[...])
        if norm_heads is None:
            o_ref[:, k * COL_SLAB:(k + 1) * COL_SLAB] = acc
            continue
        for s in range(heads_per_slab):
            sl = slice(s * HEAD_DIM, (s + 1) * HEAD_DIM)
            osl = slice(k * COL_SLAB + s * HEAD_DIM, k * COL_SLAB + (s + 1) * HEAD_DIM)
            head = (j * n_slabs + k) * heads_per_slab + s
            needs_norm = jnp.logical_and(head >= norm_heads[0], head < norm_heads[1])

            @pl.when(needs_norm)
            def _(acc=acc, sl=sl, osl=osl):
                o_ref[:, osl] = _rms_norm(acc[:, sl], qkg_ref[:, osl])

            @pl.when(jnp.logical_not(needs_norm))
            def _(acc=acc, sl=sl, osl=osl):
                o_ref[:, osl] = acc[:, sl]


def _in_proj(lidx, col_map, x2d, norm_g, w_bf, nb, n, tm, tn, qkg=None, norm_cols=None):
    T, d_model = x2d.shape
    cols = col_map.shape[0] * COL_SLAB
    assert T == nb * n and n % tm == 0 and cols % tn == 0 and tn % COL_SLAB == 0
    nt = n // tm
    n_slabs = tn // COL_SLAB
    norm_heads = None if qkg is None else (norm_cols[0] // HEAD_DIM, norm_cols[1] // HEAD_DIM)
    in_specs = [
        pl.BlockSpec((tm, d_model), lambda b, i, j, l, cm: (b * nt + i, 0)),
        pl.BlockSpec((None, 1, d_model), lambda b, i, j, l, cm: (l[0], 0, 0)),
    ] + _weight_slab_specs(n_slabs, d_model)
    args = [lidx, col_map, x2d, norm_g] + [w_bf] * n_slabs
    if qkg is not None:
        in_specs.append(pl.BlockSpec((None, 1, tn), lambda b, i, j, l, cm: (l[0], 0, j)))
        args.append(qkg)
    return pl.pallas_call(
        functools.partial(_in_proj_kernel, n_slabs=n_slabs, norm_heads=norm_heads),
        grid_spec=pltpu.PrefetchScalarGridSpec(
            num_scalar_prefetch=2,
            grid=(nb, nt, cols // tn),
            in_specs=in_specs,
            out_specs=pl.BlockSpec((tm, tn), lambda b, i, j, l, cm: (b * nt + i, j)),
            scratch_shapes=[pltpu.VMEM((tm, d_model), BF16)],
        ),
        out_shape=jax.ShapeDtypeStruct((T, cols), F32),
        compiler_params=_params(("parallel", "parallel", "arbitrary")),
        name="in_proj",
    )(*args)


def _kv_proj_kernel(l_ref, cm_ref, x_ref, g_ref, *rest, n_slabs, n_stacks):
    w_refs = rest[:n_slabs]
    outs, h_ref = rest[n_slabs + n_stacks:-1], rest[-1]
    j = pl.program_id(2)

    @pl.when(j == 0)
    def _():
        h_ref[...] = _rms_norm(x_ref[...], g_ref[...]).astype(BF16)

    heads_per_slab = COL_SLAB // HEAD_DIM
    for kind, o_ref in enumerate(outs):
        @pl.when(j == kind)
        def _(o_ref=o_ref):
            for k, w_ref in enumerate(w_refs):
                acc = _dot(h_ref[...], w_ref[...])
                for s in range(heads_per_slab):
                    o_ref[k * heads_per_slab + s] = acc[:, s * HEAD_DIM:(s + 1) * HEAD_DIM]


def _kv_proj(lidx, col_map, x2d, norm_g, w_bf, stacks, depth, nb, n, nh, tm, stacked):
    T, d_model = x2d.shape
    tn = nh * HEAD_DIM
    assert T == nb * n and n % tm == 0 and col_map.shape[0] * COL_SLAB == len(stacked) * tn and tn % COL_SLAB == 0
    nt = n // tm
    n_slabs = tn // COL_SLAB
    out_specs, out_shapes = [], []
    for is_stacked in stacked:
        if is_stacked:
            out_specs.append(pl.BlockSpec((None, None, nh, tm, HEAD_DIM), lambda b, i, j, l, cm: (l[0], b, 0, i, 0)))
            out_shapes.append(jax.ShapeDtypeStruct((depth, nb, nh, n, HEAD_DIM), F32))
        else:
            out_specs.append(pl.BlockSpec((None, nh, tm, HEAD_DIM), lambda b, i, j, l, cm: (b, 0, i, 0)))
            out_shapes.append(jax.ShapeDtypeStruct((nb, nh, n, HEAD_DIM), F32))
    stacks = list(stacks)
    stacked_outs = [k for k, is_stacked in enumerate(stacked) if is_stacked]
    assert len(stacks) == len(stacked_outs)
    n_fixed = 4 + n_slabs
    return pl.pallas_call(
        functools.partial(_kv_proj_kernel, n_slabs=n_slabs, n_stacks=len(stacks)),
        grid_spec=pltpu.PrefetchScalarGridSpec(
            num_scalar_prefetch=2,
            grid=(nb, nt, len(stacked)),
            in_specs=[
                pl.BlockSpec((tm, d_model), lambda b, i, j, l, cm: (b * nt + i, 0)),
                pl.BlockSpec((None, 1, d_model), lambda b, i, j, l, cm: (l[0], 0, 0)),
            ] + _weight_slab_specs(n_slabs, d_model) + [pl.BlockSpec(memory_space=pl.ANY)] * len(stacks),
            out_specs=out_specs,
            scratch_shapes=[pltpu.VMEM((tm, d_model), BF16)],
        ),
        out_shape=out_shapes,
        input_output_aliases={n_fixed + s: stacked_outs[s] for s in range(len(stacks))},
        compiler_params=_params(("parallel", "parallel", "arbitrary")),
        name="kv_proj",
    )(lidx, col_map, x2d, norm_g, *([w_bf] * n_slabs), *stacks)


def _pool_kernel(l_ref, u_ref, ga_ref, halo_ref, wp_ref, ps_ref, o_ref, ext_ref, *, tm, pos0):
    i = pl.program_id(1)
    ext_ref[0:HALO, :] = halo_ref[...]
    ext_ref[HALO:, :] = u_ref[...]
    pos = pos0 + i * tm + lax.broadcasted_iota(jnp.int32, (tm, 1), 0)
    for g, w in enumerate(POOL_WINDOWS):
        sl = slice(g * POOL_GROUP, (g + 1) * POOL_GROUP)
        s = ext_ref[HALO:HALO + tm, sl]
        for d in range(1, w):
            s = s + ext_ref[HALO - d:HALO - d + tm, sl]
        cnt = jnp.minimum(w, pos + 1).astype(F32)
        pooled = s / cnt - u_ref[:, sl]
        y = _dot(pooled.astype(BF16), wp_ref[g].astype(BF16)) * ps_ref[:, sl]
        o_ref[:, sl] = (y * _silu(ga_ref[:, sl])).astype(o_ref.dtype)


def _pool(lidx, p2d, halo, w_pool, pool_scale, nb, n, tm, pos0, u_col, g_col, out_dtype):
    d_pool = POOL_GROUP * len(POOL_WINDOWS)
    nt = n // tm
    assert u_col % d_pool == 0 and g_col % d_pool == 0
    uc, gc = u_col // d_pool, g_col // d_pool
    kern = functools.partial(_pool_kernel, tm=tm, pos0=pos0)
    return pl.pallas_call(
        kern,
        grid_spec=pltpu.PrefetchScalarGridSpec(
            num_scalar_prefetch=1,
            grid=(nb, nt),
            in_specs=[
                pl.BlockSpec((tm, d_pool), lambda b, i, l: (b * nt + i, uc)),
                pl.BlockSpec((tm, d_pool), lambda b, i, l: (b * nt + i, gc)),
                pl.BlockSpec((None, None, HALO, d_pool), lambda b, i, l: (b, i, 0, 0)),
                pl.BlockSpec((None, len(POOL_WINDOWS), POOL_GROUP, POOL_GROUP), lambda b, i, l: (l[0], 0, 0, 0)),
                pl.BlockSpec((None, 1, d_pool), lambda b, i, l: (l[0], 0, 0)),
            ],
            out_specs=pl.BlockSpec((tm, d_pool), lambda b, i, l: (b * nt + i, 0)),
            scratch_shapes=[pltpu.VMEM((HALO + tm, d_pool), F32)],
        ),
        out_shape=jax.ShapeDtypeStruct((nb * n, d_pool), out_dtype),
        compiler_params=_params(("parallel", "arbitrary")),
        name="pool",
    )(lidx, p2d, p2d, halo, w_pool, pool_scale)


def _sb_prefill_kernel(l_ref, q_ref, k_ref, v_ref, g_ref, o_ref, acc_ref, cs_ref, *, tq, scale):
    i = pl.program_id(2)
    hps = SB_HEADS_PER_STEP
    qs = [q_ref[:, h * HEAD_DIM:(h + 1) * HEAD_DIM].astype(BF16) for h in range(hps)]
    row = lax.broadcasted_iota(jnp.int32, (tq, tq), 0)
    col = lax.broadcasted_iota(jnp.int32, (tq, tq), 1)
    later = jnp.where(row > col, 1.0, 0.0).astype(BF16)

    def logits(kj):
        off = pl.multiple_of(kj * tq, tq)
        return jnp.concatenate([_dot_nt(qs[h], k_ref[h, pl.ds(off, tq), :].astype(BF16)) for h in range(hps)],
                               axis=0) * scale

    def tiles(z, kj, diagonal):
        off = pl.multiple_of(kj * tq, tq)
        ls = _log_sigmoid(z)
        lk = ls - z
        if diagonal:
            keep = jnp.concatenate([col < row] * hps, axis=0)
            lk = jnp.where(keep, lk, 0.0)
        hi, lo = _split_bf16(lk)
        la = _dot(hi, later) + _dot(lo, later) + _lanes(cs_ref[...], tq)
        a = jnp.exp(ls + la)
        if diagonal:
            a = jnp.where(keep, a, 0.0)
        for h in range(hps):
            v = v_ref[h, pl.ds(off, tq), :].astype(BF16)
            acc_ref[h] += _dot(a[h * tq:(h + 1) * tq].astype(BF16), v)
        cs = cs_ref[...] + jnp.sum(lk, axis=1, keepdims=True)
        cs_ref[...] = cs
        return jnp.max(cs)

    acc_ref[...] = jnp.zeros_like(acc_ref)
    cs_ref[...] = jnp.zeros_like(cs_ref)
    z_next = logits(jnp.maximum(i - 1, 0))
    cs_max = tiles(logits(i), i, True)

    def more(state):
        t, cs_max, _ = state
        return jnp.logical_and(t < i, cs_max > UNDERFLOW_LOG)

    def body(state):
        t, _, z = state
        z_after = logits(jnp.maximum(i - 2 - t, 0))
        return t + 1, tiles(z, i - 1 - t, False), z_after

    lax.while_loop(more, body, (0, cs_max, z_next))
    gate = _silu(g_ref[...])
    for h in range(hps):
        sl = slice(h * HEAD_DIM, (h + 1) * HEAD_DIM)
        o_ref[:, sl] = (acc_ref[h] * gate[:, sl]).astype(o_ref.dtype)


def _sb_prefill(lidx, p2d, k_stack, v_stack, nb, n, nh, q_col, g_col, tq):
    hps = SB_HEADS_PER_STEP
    width = hps * HEAD_DIM
    nq = n // tq
    assert nh % hps == 0 and q_col % width == 0 and g_col % width == 0
    kern = functools.partial(_sb_prefill_kernel, tq=tq, scale=1.0 / math.sqrt(HEAD_DIM))
    qc, gc = q_col // width, g_col // width
    heads = pl.BlockSpec((None, None, hps, n, HEAD_DIM), lambda b, h, i, l: (l[0], b, h, 0, 0))
    return pl.pallas_call(
        kern,
        grid_spec=pltpu.PrefetchScalarGridSpec(
            num_scalar_prefetch=1,
            grid=(nb, nh // hps, nq),
            in_specs=[
                pl.BlockSpec((tq, width), lambda b, h, i, l: (b * nq + i, qc + h)),
                heads, heads,
                pl.BlockSpec((tq, width), lambda b, h, i, l: (b * nq + i, gc + h)),
            ],
            out_specs=pl.BlockSpec((tq, width), lambda b, h, i, l: (b * nq + i, h)),
            scratch_shapes=[pltpu.VMEM((hps, tq, HEAD_DIM), F32), pltpu.VMEM((hps * tq, LANES), F32)],
        ),
        out_shape=jax.ShapeDtypeStruct((nb * n, nh * HEAD_DIM), BF16),
        compiler_params=_params(("parallel", "parallel", "arbitrary")),
        name="sb_prefill",
    )(lidx, p2d, k_stack, v_stack, p2d)


def _top_k_lanes(gate, k):
    lane = lax.broadcasted_iota(jnp.int32, gate.shape, 1).astype(F32)
    sel = jnp.zeros(gate.shape, F32)
    for _ in range(k):
        m = jnp.max(gate, axis=1, keepdims=True)
        cand = jnp.where(gate == m, lane, float(LANES))
        cand = jnp.where(m > NEG_INF, cand, float(LANES))
        idx = jnp.min(cand, axis=1, keepdims=True)
        pick = lane == idx
        sel = jnp.where(pick, 1.0, sel)
        gate = jnp.where(pick, NEG_INF, gate)
    return sel


def _top_k_sublanes(gate_t, k, n_blocks):
    block = lax.broadcasted_iota(jnp.int32, gate_t.shape, 0)
    rank = jnp.zeros(gate_t.shape, F32)
    for b2 in range(n_blocks):
        other = jnp.broadcast_to(gate_t[b2:b2 + 1, :], gate_t.shape)
        wins_tie = jnp.where(block > b2, 1.0, 0.0)
        rank = rank + jnp.where(other > gate_t, 1.0, jnp.where(other == gate_t, wins_tie, 0.0))
    return jnp.where(jnp.logical_and(rank < k, gate_t > NEG_INF), 1.0, 0.0)


def _gate_scores(a, b):
    a_hi, a_lo = _split_bf16(a)
    b_hi, b_lo = _split_bf16(b)
    return _dot_nt(a_hi, b_hi) + _dot_nt(a_hi, b_lo) + _dot_nt(a_lo, b_hi)


def _moba_prefill_kernel(l_ref, q_ref, k_ref, v_ref, g_ref, bias_ref, qg_ref, kg_ref, *rest, n_stacks, nblk, scale):
    o_ref, kn_ref, km_ref, qn_ref, nc_ref, m_ref, l_scr, acc_ref = rest[n_stacks:]
    i = pl.program_id(2)
    hps = MOBA_HEADS_PER_STEP
    blk = MOBA_BLOCK
    n = nblk * blk
    row = lax.broadcasted_iota(jnp.int32, (blk, blk), 0)
    col = lax.broadcasted_iota(jnp.int32, (blk, blk), 1)
    block_id = lax.broadcasted_iota(jnp.int32, (LANES, blk), 0)

    @pl.when(i == 0)
    def _():
        eye = jnp.where(row == col, 1.0, 0.0).astype(BF16)
        gate_rows = -(-nblk // SUBLANES) * SUBLANES
        gate_block = lax.broadcasted_iota(jnp.int32, (gate_rows, n), 0)
        query_block = lax.shift_right_logical(lax.broadcasted_iota(jnp.int32, (gate_rows, n), 1), int(math.log2(blk)))
        km_ref[...] = jnp.zeros_like(km_ref)
        for h in range(hps):
            for b in range(nblk):
                kn = _rms_norm(k_ref[h, b * blk:(b + 1) * blk, :], kg_ref[...])
                kn_ref[h, b * blk:(b + 1) * blk, :] = kn
                km_ref[h, b:b + 1, :] = jnp.sum(kn, axis=0, keepdims=True) * (1.0 / blk)
            qf = _rms_norm(q_ref[:, h * HEAD_DIM:(h + 1) * HEAD_DIM], qg_ref[...])
            qn_ref[h] = qf.astype(BF16)
            gate_t = jnp.where(gate_block < query_block, _gate_scores(km_ref[h, 0:gate_rows, :], qf), NEG_INF)
            mask_t = jnp.where(_top_k_sublanes(gate_t, MOBA_TOPK, nblk) > 0.0, 0.0, NEG_BIG)
            if gate_rows < LANES:
                mask_t = jnp.concatenate([mask_t, jnp.zeros((LANES - gate_rows, n), F32)], axis=0)
            mask_t = mask_t.astype(BF16)
            for t in range(nblk):
                nc_ref[t, h * blk:(h + 1) * blk, :] = _dot_nt(eye, mask_t[:, t * blk:(t + 1) * blk]).astype(BF16)

    row0 = pl.multiple_of(i * blk, blk)
    qs = [qn_ref[h, pl.ds(row0, blk), :] for h in range(hps)]
    not_chosen = nc_ref[i]

    m_ref[...] = jnp.full_like(m_ref, NEG_INF)
    l_scr[...] = jnp.zeros_like(l_scr)
    acc_ref[...] = jnp.zeros_like(acc_ref)

    def raw_scores(j):
        off = pl.multiple_of(j * blk, blk)
        return jnp.concatenate([_dot_nt(qs[h], kn_ref[h, pl.ds(off, blk), :].astype(BF16)) * scale + bias_ref[h, i - j]
                                for h in range(hps)], axis=0)

    s_own = jnp.where(jnp.concatenate([col <= row] * hps, axis=0), raw_scores(i), NEG_INF)

    def masked(raw, j):
        return raw + _dot(not_chosen, jnp.where(block_id == j, 1.0, 0.0).astype(BF16))

    def scores(j):
        return masked(raw_scores(j), j)

    def accumulate(s, j):
        off = pl.multiple_of(j * blk, blk)
        m_prev = m_ref[...]
        m_new = jnp.maximum(m_prev, jnp.max(s, axis=1, keepdims=True))
        alpha = jnp.exp(m_prev - m_new)
        p = jnp.exp(s - _lanes(m_new, blk))
        l_scr[...] = alpha * l_scr[...] + jnp.sum(p, axis=1, keepdims=True)
        for h in range(hps):
            rows = slice(h * blk, (h + 1) * blk)
            v = v_ref[h, pl.ds(off, blk), :].astype(BF16)
            acc_ref[h] = alpha[rows] * acc_ref[h] + _dot(p[rows].astype(BF16), v)
        m_ref[...] = m_new

    s_first = scores(0)
    accumulate(s_own, i)

    def body(j, s):
        s_next = scores(jnp.minimum(j + 1, jnp.maximum(i - 1, 0)))
        accumulate(s, j)
        return s_next

    lax.fori_loop(0, i, body, s_first)
    gate = _silu(g_ref[...])
    inv_l = 1.0 / l_scr[...]
    for h in range(hps):
        sl = slice(h * HEAD_DIM, (h + 1) * HEAD_DIM)
        o_ref[:, sl] = (acc_ref[h] * inv_l[h * blk:(h + 1) * blk] * gate[:, sl]).astype(o_ref.dtype)


def _moba_prefill(lidx, p2d, k_raw, v_stack, k_stack, bias_tiles, q_gain, k_gain, depth, nb, n, nh, q_col, g_col):
    hps = MOBA_HEADS_PER_STEP
    width = hps * HEAD_DIM
    blk = MOBA_BLOCK
    assert n % blk == 0
    nq = n // blk
    assert nq <= LANES
    assert nh % hps == 0 and q_col % width == 0 and g_col % width == 0
    extend = [k_stack]
    kern = functools.partial(_moba_prefill_kernel, n_stacks=len(extend), nblk=nq, scale=1.0 / math.sqrt(HEAD_DIM))
    qc, gc = q_col // width, g_col // width
    gain = pl.BlockSpec((None, 1, HEAD_DIM), lambda h, b, i, l: (l[0], 0, 0))
    layer_heads = pl.BlockSpec((None, None, hps, n, HEAD_DIM), lambda h, b, i, l: (l[0], b, h, 0, 0))
    n_fixed = 8
    return pl.pallas_call(
        kern,
        grid_spec=pltpu.PrefetchScalarGridSpec(
            num_scalar_prefetch=1,
            grid=(nh // hps, nb, nq),
            in_specs=[
                pl.BlockSpec((n, width), lambda h, b, i, l: (b, qc + h)),
                pl.BlockSpec((None, hps, n, HEAD_DIM), lambda h, b, i, l: (b, h, 0, 0)),
                layer_heads,
                pl.BlockSpec((blk, width), lambda h, b, i, l: (b * nq + i, gc + h)),
                pl.BlockSpec((hps, nq, blk, blk), lambda h, b, i, l: (h, 0, 0, 0)),
                gain, gain,
            ] + [pl.BlockSpec(memory_space=pl.ANY)] * len(extend),
            out_specs=[
                pl.BlockSpec((blk, width), lambda h, b, i, l: (b * nq + i, h)),
                layer_heads,
            ],
            scratch_shapes=[pltpu.VMEM((hps, LANES, HEAD_DIM), F32),
                            pltpu.VMEM((hps, n, HEAD_DIM), BF16),
                            pltpu.VMEM((nq, hps * blk, LANES), BF16),
                            pltpu.VMEM((hps * blk, LANES), F32), pltpu.VMEM((hps * blk, LANES), F32),
                            pltpu.VMEM((hps, blk, HEAD_DIM), F32)],
        ),
        out_shape=[jax.ShapeDtypeStruct((nb * n, nh * HEAD_DIM), BF16),
                   jax.ShapeDtypeStruct((depth, nb, nh, n, HEAD_DIM), F32)],
        input_output_aliases={n_fixed: 1},
        compiler_params=_params(("parallel", "parallel", "arbitrary")),
        name="moba_prefill",
    )(lidx, p2d, k_raw, v_stack, p2d, bias_tiles, q_gain, k_gain, *extend)


def _head_queries(q_ref, nh):
    return [q_ref[:, h * HEAD_DIM:(h + 1) * HEAD_DIM] for h in range(nh)]


def _stacked_scores(qs, k_ref):
    return jnp.concatenate([_dot_nt(q, k_ref[h].astype(BF16)) for h, q in enumerate(qs)], axis=0)


def _stacked_values(p, v_ref, nh, dn):
    return jnp.concatenate(
        [_dot(p[h * dn:(h + 1) * dn].astype(BF16), v_ref[h].astype(BF16)) for h in range(nh)], axis=0)


def _gated_heads(acc, g, nh, dn):
    return jnp.concatenate([acc[h * dn:(h + 1) * dn] for h in range(nh)], axis=1) * _silu(g)


def _page_index_map(r, pps, n_pages, reverse):
    def index_map(b, s, l, pt):
        page = s * pps + r
        if reverse:
            page = n_pages - 1 - page
        return (l[0], pt[b * n_pages + page], 0, 0, 0)

    return index_map


def _page_specs(pps, nh, n_pages, reverse):
    return [pl.BlockSpec((None, None, nh, PAGE_SIZE, HEAD_DIM), _page_index_map(r, pps, n_pages, reverse))
            for r in range(pps)]


def _sb_decode_kernel(l_ref, pt_ref, q_ref, kn_ref, vn_ref, g_ref, ck_hbm, cv_hbm, o_ref,
                      kbuf, vbuf, sems, acc_ref, cs_ref, *, nh, dn, n_pages, scale):
    ppc = SB_PAGES_PER_CHUNK
    n_chunks = n_pages // ppc
    b = pl.program_id(0)
    rows = nh * dn

    def page_copies(c, slot):
        copies = []
        for r in range(ppc):
            page = pt_ref[b * n_pages + (n_pages - 1 - (c * ppc + r))]
            copies.append(pltpu.make_async_copy(ck_hbm.at[l_ref[0], page], kbuf.at[slot, r], sems.at[0, slot]))
            copies.append(pltpu.make_async_copy(cv_hbm.at[l_ref[0], page], vbuf.at[slot, r], sems.at[1, slot]))
        return copies

    qs = [q.astype(BF16) for q in _head_queries(q_ref, nh)]
    row = lax.broadcasted_iota(jnp.int32, (PAGE_SIZE, PAGE_SIZE), 0)
    col = lax.broadcasted_iota(jnp.int32, (PAGE_SIZE, PAGE_SIZE), 1)
    later = jnp.where(row > col, 1.0, 0.0).astype(BF16)

    def pages(kp_refs, vp_refs, keep):
        z = jnp.concatenate([_stacked_scores(qs, kp) for kp in kp_refs], axis=0) * scale
        ls = _log_sigmoid(z)
        lk = ls - z
        if keep is not None:
            lk = jnp.where(keep, lk, 0.0)
        hi, lo = _split_bf16(lk)
        within = _dot(hi, later) + _dot(lo, later)
        tot = jnp.sum(lk, axis=1, keepdims=True)
        carry = cs_ref[...]
        carries = []
        for r in range(len(kp_refs)):
            carries.append(carry)
            carry = carry + tot[r * rows:(r + 1) * rows]
        cs_ref[...] = carry
        a = jnp.exp(ls + within + jnp.concatenate(carries, axis=0))
        if keep is not None:
            a = jnp.where(keep, a, 0.0)
        upd = _stacked_values(a[0:rows], vp_refs[0], nh, dn)
        for r in range(1, len(kp_refs)):
            upd = upd + _stacked_values(a[r * rows:(r + 1) * rows], vp_refs[r], nh, dn)
        acc_ref[...] += upd
        return jnp.max(carry)

    for cp in page_copies(0, 0):
        cp.start()
    acc_ref[...] = jnp.zeros_like(acc_ref)
    cs_ref[...] = jnp.zeros_like(cs_ref)
    tok = lax.broadcasted_iota(jnp.int32, (rows, PAGE_SIZE), 0) & (dn - 1)
    key = lax.broadcasted_iota(jnp.int32, (rows, PAGE_SIZE), 1)
    cs_max = pages([kn_ref], [vn_ref], key < tok)

    def more(state):
        c, cs_max = state
        return jnp.logical_and(c < n_chunks, cs_max > UNDERFLOW_LOG)

    def trip(state):
        c, _ = state
        slot = c & 1

        @pl.when(c + 1 < n_chunks)
        def _():
            for cp in page_copies(c + 1, 1 - slot):
                cp.start()

        for cp in page_copies(c, slot):
            cp.wait()
        cs_max = pages([kbuf.at[slot, r] for r in range(ppc)], [vbuf.at[slot, r] for r in range(ppc)], None)
        return c + 1, cs_max

    c_end, _ = lax.while_loop(more, trip, (0, cs_max))

    @pl.when(c_end < n_chunks)
    def _():
        for cp in page_copies(c_end, c_end & 1):
            cp.wait()

    o_ref[...] = _gated_heads(acc_ref[...], g_ref[...], nh, dn).astype(o_ref.dtype)


def _sb_decode(lidx, pt_flat, q, k_new, v_new, g, cache_k, cache_v, nb, dn, nh, n_pages):
    ppc = SB_PAGES_PER_CHUNK
    assert n_pages % ppc == 0 and dn == SUBLANES
    width = nh * HEAD_DIM
    kern = functools.partial(_sb_decode_kernel, nh=nh, dn=dn, n_pages=n_pages, scale=1.0 / math.sqrt(HEAD_DIM))
    seq = lambda rows: pl.BlockSpec((None, rows, width), lambda b, l, pt: (b, 0, 0))
    new = pl.BlockSpec((None, nh, PAGE_SIZE, HEAD_DIM), lambda b, l, pt: (b, 0, 0, 0))
    hbm = pl.BlockSpec(memory_space=pl.ANY)
    chunk = (2, ppc, nh, PAGE_SIZE, HEAD_DIM)
    return pl.pallas_call(
        kern,
        grid_spec=pltpu.PrefetchScalarGridSpec(
            num_scalar_prefetch=2,
            grid=(nb,),
            in_specs=[seq(dn), new, new, seq(dn), hbm, hbm],
            out_specs=seq(dn),
            scratch_shapes=[pltpu.VMEM(chunk, F32), pltpu.VMEM(chunk, F32), pltpu.SemaphoreType.DMA((2, 2)),
                            pltpu.VMEM((nh * dn, HEAD_DIM), F32), pltpu.VMEM((nh * dn, LANES), F32)],
        ),
        out_shape=jax.ShapeDtypeStruct((nb, dn, width), F32),
        compiler_params=_params(("arbitrary",)),
        name="sb_decode",
    )(lidx, pt_flat, q, k_new, v_new, g, cache_k, cache_v)


def _moba_scores_kernel(l_ref, pt_ref, q_ref, *rest, nh, dn):
    pps = MOBA_PAGES_PER_STEP
    k_refs = rest[:pps]
    s_ref, km_ref = rest[pps:]
    s_idx = pl.program_id(1)
    qs = [q.astype(BF16) for q in _head_queries(q_ref, nh)]
    pages_per_block = MOBA_BLOCK // PAGE_SIZE
    blocks_per_step = pps // pages_per_block

    @pl.when(s_idx == 0)
    def _():
        km_ref[...] = jnp.zeros_like(km_ref)

    sub = lax.broadcasted_iota(jnp.int32, (blocks_per_step, HEAD_DIM), 0)
    for h in range(nh):
        means = jnp.zeros((blocks_per_step, HEAD_DIM), F32)
        for c in range(blocks_per_step):
            tot = jnp.zeros((1, HEAD_DIM), F32)
            for r in range(c * pages_per_block, (c + 1) * pages_per_block):
                kp = k_refs[r][h]
                s_ref[h * dn:(h + 1) * dn, r * PAGE_SIZE:(r + 1) * PAGE_SIZE] = _dot_nt(qs[h], kp.astype(BF16))
                tot = tot + jnp.sum(kp, axis=0, keepdims=True)
            means = jnp.where(sub == c, tot * (1.0 / MOBA_BLOCK), means)
        km_ref[h, pl.ds(pl.multiple_of(s_idx * blocks_per_step, blocks_per_step), blocks_per_step), :] = means


def _moba_scores(lidx, pt_flat, q, cache_k, nb, dn, nh, n_pages):
    pps = MOBA_PAGES_PER_STEP
    assert pps // (MOBA_BLOCK // PAGE_SIZE) == SUBLANES and n_pages % pps == 0
    width = nh * HEAD_DIM
    past = n_pages * PAGE_SIZE
    assert past // MOBA_BLOCK <= LANES
    kern = functools.partial(_moba_scores_kernel, nh=nh, dn=dn)
    return pl.pallas_call(
        kern,
        grid_spec=pltpu.PrefetchScalarGridSpec(
            num_scalar_prefetch=2,
            grid=(nb, n_pages // pps),
            in_specs=[pl.BlockSpec((None, dn, width), lambda b, s, l, pt: (b, 0, 0))]
            + _page_specs(pps, nh, n_pages, False),
            out_specs=[
                pl.BlockSpec((None, nh * dn, pps * PAGE_SIZE), lambda b, s, l, pt: (b, 0, s)),
                pl.BlockSpec((None, nh, LANES, HEAD_DIM), lambda b, s, l, pt: (b, 0, 0, 0)),
            ],
        ),
        out_shape=[jax.ShapeDtypeStruct((nb, nh * dn, past), F32),
                   jax.ShapeDtypeStruct((nb, nh, LANES, HEAD_DIM), F32)],
        compiler_params=_params(("parallel", "arbitrary")),
        name="moba_scores",
    )(lidx, pt_flat, q, *([cache_k] * pps))


def _moba_values_kernel(l_ref, pt_ref, q_ref, km_ref, s_ref, bias_ref, kn_ref, vn_ref, bown_ref, g_ref, *rest,
                        nh, dn, n_past_blocks, scale):
    pps = MOBA_PAGES_PER_STEP
    v_refs = rest[:pps]
    o_ref, sel_ref, m_ref, l_scr, acc_ref = rest[pps:]
    s_idx = pl.program_id(1)
    rows = nh * dn
    width = pps * PAGE_SIZE
    blocks_per_step = width // MOBA_BLOCK

    @pl.when(s_idx == 0)
    def _():
        qf = _head_queries(q_ref, nh)
        lane = lax.broadcasted_iota(jnp.int32, (rows, LANES), 1)
        gate = jnp.concatenate([_gate_scores(qf[h], km_ref[h]) for h in range(nh)], axis=0)
        gate = jnp.where(lane < n_past_blocks, gate, NEG_INF)
        sel_ref[...] = _top_k_lanes(gate, MOBA_TOPK)
        tok = lax.broadcasted_iota(jnp.int32, (rows, PAGE_SIZE), 0) & (dn - 1)
        key = lax.broadcasted_iota(jnp.int32, (rows, PAGE_SIZE), 1)
        s_own = _stacked_scores([q.astype(BF16) for q in qf], kn_ref) * scale + bown_ref[...]
        s_own = jnp.where(key <= tok, s_own, NEG_INF)
        m0 = jnp.max(s_own, axis=1, keepdims=True)
        p_own = jnp.exp(s_own - m0)
        m_ref[...] = jnp.broadcast_to(m0, m_ref.shape)
        l_scr[...] = jnp.broadcast_to(jnp.sum(p_own, axis=1, keepdims=True), l_scr.shape)
        acc_ref[...] = _stacked_values(p_own, vn_ref, nh, dn)

    key_in_step = lax.broadcasted_iota(jnp.int32, (LANES, width), 1)
    blk_of_key = lax.shift_right_logical(key_in_step, int(math.log2(MOBA_BLOCK))) + s_idx * blocks_per_step
    indicator = jnp.where(lax.broadcasted_iota(jnp.int32, (LANES, width), 0) == blk_of_key, 1.0, 0.0).astype(BF16)
    chosen = _dot(sel_ref[...].astype(BF16), indicator) > 0.5
    sm = jnp.where(chosen, s_ref[...] * scale + bias_ref[...], NEG_INF)
    m_prev = m_ref[...]
    m_new = jnp.maximum(m_prev, jnp.max(sm, axis=1, keepdims=True))
    alpha = jnp.exp(m_prev - m_new)
    p = jnp.exp(sm - _lanes(m_new, width))
    l_scr[...] = alpha * l_scr[...] + jnp.sum(p, axis=1, keepdims=True)
    upd = _stacked_values(p[:, 0:PAGE_SIZE], v_refs[0], nh, dn)
    for r in range(1, pps):
        upd = upd + _stacked_values(p[:, r * PAGE_SIZE:(r + 1) * PAGE_SIZE], v_refs[r], nh, dn)
    acc_ref[...] = alpha * acc_ref[...] + upd
    m_ref[...] = m_new

    @pl.when(s_idx == pl.num_programs(1) - 1)
    def _():
        o_ref[...] = _gated_heads(acc_ref[...] / l_scr[...], g_ref[...], nh, dn).astype(o_ref.dtype)


def _moba_values(lidx, pt_flat, q, km, scores, bias_past, k_new, v_new, bias_own, g, cache_v,
                 nb, dn, nh, n_pages):
    pps = MOBA_PAGES_PER_STEP
    width = nh * HEAD_DIM
    rows = nh * dn
    past = n_pages * PAGE_SIZE
    n_past_blocks = past // MOBA_BLOCK
    assert n_past_blocks <= LANES and km.shape[2] == LANES and n_pages % pps == 0
    assert (pps * PAGE_SIZE) % MOBA_BLOCK == 0
    kern = functools.partial(_moba_values_kernel, nh=nh, dn=dn, n_past_blocks=n_past_blocks,
                             scale=1.0 / math.sqrt(HEAD_DIM))
    seq = lambda r, c: pl.BlockSpec((None, r, c), lambda b, s, l, pt: (b, 0, 0))
    per_head = lambda r: pl.BlockSpec((None, nh, r, HEAD_DIM), lambda b, s, l, pt: (b, 0, 0, 0))
    return pl.pallas_call(
        kern,
        grid_spec=pltpu.PrefetchScalarGridSpec(
            num_scalar_prefetch=2,
            grid=(nb, n_pages // pps),
            in_specs=[seq(dn, width), per_head(LANES),
                      pl.BlockSpec((None, rows, pps * PAGE_SIZE), lambda b, s, l, pt: (b, 0, s)),
                      pl.BlockSpec((rows, pps * PAGE_SIZE), lambda b, s, l, pt: (0, s)),
                      per_head(PAGE_SIZE), per_head(PAGE_SIZE),
                      pl.BlockSpec((rows, PAGE_SIZE), lambda b, s, l, pt: (0, 0)),
                      seq(dn, width)] + _page_specs(pps, nh, n_pages, False),
            out_specs=seq(dn, width),
            scratch_shapes=[pltpu.VMEM((rows, LANES), F32), pltpu.VMEM((rows, LANES), F32),
                            pltpu.VMEM((rows, LANES), F32), pltpu.VMEM((rows, HEAD_DIM), F32)],
        ),
        out_shape=jax.ShapeDtypeStruct((nb, dn, width), F32),
        compiler_params=_params(("parallel", "arbitrary")),
        name="moba_values",
    )(lidx, pt_flat, q, km, scores, bias_past, k_new, v_new, bias_own, g, *([cache_v] * pps))


def _out_proj_kernel(l_ref, x_ref, a_ref, b_ref, c_ref, *rest):
    w_refs, o_ref = rest[:-1], rest[-1]
    acc, slab = None, 0
    for part in (a_ref, b_ref, c_ref):
        for c0 in range(0, part.shape[1], OUT_PROJ_K):
            term = _dot(part[:, c0:c0 + OUT_PROJ_K].astype(BF16), w_refs[slab][...])
            acc = term if acc is None else acc + term
            slab += 1
    o_ref[...] = x_ref[...] + acc


def _out_proj(lidx, x2d, a, b, c, w_bf, tm, in_place):
    T, d_model = x2d.shape
    widths = (a.shape[1], b.shape[1], c.shape[1])
    assert T % tm == 0 and sum(widths) == w_bf.shape[1] and all(w % OUT_PROJ_K == 0 for w in widths)
    act = lambda arr: pl.BlockSpec((tm, arr.shape[1]), lambda i, l: (i, 0))
    slabs = [pl.BlockSpec((None, OUT_PROJ_K, d_model), lambda i, l, k=k: (l[0], k, 0))
             for k in range(w_bf.shape[1] // OUT_PROJ_K)]
    return pl.pallas_call(
        _out_proj_kernel,
        grid_spec=pltpu.PrefetchScalarGridSpec(
            num_scalar_prefetch=1,
            grid=(T // tm,),
            in_specs=[act(x2d), act(a), act(b), act(c)] + slabs,
            out_specs=pl.BlockSpec((tm, d_model), lambda i, l: (i, 0)),
        ),
        out_shape=jax.ShapeDtypeStruct((T, d_model), F32),
        input_output_aliases={1: 0} if in_place else {},
        compiler_params=_params(("parallel",)),
        name="out_proj",
    )(lidx, x2d, a, b, c, *([w_bf] * len(slabs)))


def _rel_bucket_np(n):
    n = np.maximum(n, 0)
    max_exact = REL_BUCKETS // 2
    nf = np.maximum(n, 1).astype(np.float64)
    large = max_exact + (np.log(nf / max_exact) / math.log(REL_MAX_DIST / max_exact)
                         * (REL_BUCKETS - max_exact)).astype(np.int32)
    large = np.minimum(large, REL_BUCKETS - 1)
    return np.where(n < max_exact, n, large).astype(np.int32)


def _toeplitz_kernel(gen_ref, o_ref):
    blk = MOBA_BLOCK
    wide = jnp.broadcast_to(gen_ref[...], (blk, 2 * blk))
    o_ref[...] = pltpu.roll(wide, 0, 1, stride=1, stride_axis=0)[:, :blk]


def _take_static(table_t, idx):
    idx = np.asarray(idx)
    flat = idx.reshape(-1)
    starts = np.concatenate([[0], np.flatnonzero(np.diff(flat)) + 1, [flat.size]])
    rows = table_t.shape[0]
    runs = [jnp.broadcast_to(table_t[:, int(flat[s]):int(flat[s]) + 1], (rows, int(e - s)))
            for s, e in zip(starts[:-1], starts[1:])]
    return jnp.concatenate(runs, axis=1).reshape((rows,) + idx.shape)


def _toeplitz_bias_tiles(bias_t, buckets, nq):
    blk = MOBA_BLOCK
    nh = bias_t.shape[0]
    m = np.arange(2 * blk)
    d = np.arange(nq)[:, None] * blk
    gen_idx = np.where(m[None, :] <= blk, np.maximum(d - m[None, :], 0), d + 2 * blk - m[None, :])
    gen = _take_static(bias_t, buckets[gen_idx])[:, :, None, :]
    return pl.pallas_call(
        _toeplitz_kernel,
        grid=(nh, nq),
        in_specs=[pl.BlockSpec((None, None, 1, 2 * blk), lambda h, d: (h, d, 0, 0))],
        out_specs=pl.BlockSpec((None, None, blk, blk), lambda h, d: (h, d, 0, 0)),
        out_shape=jax.ShapeDtypeStruct((nh, nq, blk, blk), F32),
        compiler_params=_params(("parallel", "parallel")),
        name="bias_tiles",
    )(gen)


def _decode_bias_kernel(gen_ref, o_ref):
    dn, past = o_ref.shape
    wide = jnp.broadcast_to(gen_ref[...], (dn, gen_ref.shape[-1]))
    o_ref[...] = pltpu.roll(wide, 0, 1, stride=1, stride_axis=0)[:, :past]


def _decode_bias_rows(bias_t, buckets, past, dn):
    nh = bias_t.shape[0]
    width = past + LANES
    j = np.arange(width)
    dist = np.where(j <= past, past - j, np.minimum(past + width - j, past + dn))
    gen = _take_static(bias_t, buckets[dist])[:, None, :]
    return pl.pallas_call(
        _decode_bias_kernel,
        grid=(nh,),
        in_specs=[pl.BlockSpec((None, 1, width), lambda h: (h, 0, 0))],
        out_specs=pl.BlockSpec((dn, past), lambda h: (h, 0)),
        out_shape=jax.ShapeDtypeStruct((nh * dn, past), F32),
        compiler_params=_params(("parallel",)),
        name="decode_bias",
    )(gen)


def _pool_halo(u3, hist, tm):
    nb, n, d = u3.shape
    nt = n // tm
    first = jnp.pad(hist, ((0, 0), (HALO - POOL_HIST, 0), (0, 0)))[:, None]
    if nt == 1:
        return first
    tails = u3.reshape(nb, nt, tm, d)[:, :-1, tm - HALO:, :]
    return jnp.concatenate([first, tails], axis=1)


def kernel(x_prompt, x_sample, cache_sb_k, cache_sb_v, cache_moba_k, cache_moba_v, state_pool, page_table,
           norm_g, w_in, w_pool, pool_scale, q_norm_g, k_norm_g, w_out, rel_bias):
    nb, n, d_model = x_prompt.shape
    db, dn, _ = x_sample.shape
    depth, n_phys, page_size, nh, hd = cache_sb_k.shape
    n_pages = page_table.shape[1]
    past = n_pages * page_size
    assert page_size == PAGE_SIZE and hd == HEAD_DIM
    d_pool = POOL_GROUP * len(POOL_WINDOWS)
    d_att = nh * HEAD_DIM
    d_proj = w_in.shape[-1]
    assert d_proj == 2 * d_pool + 8 * d_att
    assert (2 * d_pool) % COL_SLAB == 0 and d_att % COL_SLAB == 0
    w_in_bf = w_in.astype(BF16)
    slabs = lambda col0, width: list(range(col0 // COL_SLAB, (col0 + width) // COL_SLAB))
    seg = lambda t: slabs(2 * d_pool + t * d_att, d_att)
    cm_main = jnp.asarray(seg(0) + seg(3) + seg(4) + seg(7) + slabs(0, 2 * d_pool), jnp.int32)
    cm_kv = jnp.asarray(seg(1) + seg(2) + seg(5) + seg(6), jnp.int32)
    m_qb, m_gb, m_qc, m_gc = (t * d_att for t in range(4))
    m_u, m_ga = 4 * d_att, 4 * d_att + d_pool
    zeros_att = jnp.zeros((depth, d_att), F32)
    g_main = jnp.concatenate([jnp.zeros((depth, m_qc), F32), jnp.tile(q_norm_g, (1, nh)),
                              jnp.zeros((depth, d_att + 2 * d_pool), F32)], axis=-1)[:, None]
    g_kv = jnp.concatenate([zeros_att, zeros_att, jnp.tile(k_norm_g, (1, nh)), zeros_att], axis=-1)[:, None]
    norm_main = (m_qc, m_qc + d_att)
    norm_kv = (2 * d_att, 3 * d_att)
    q_gain, k_gain = q_norm_g[:, None, :], k_norm_g[:, None, :]

    w_out_bf = w_out.astype(BF16)
    norm_g3 = norm_g[:, None, :]
    pool_scale3 = pool_scale[:, None, :]
    caches = [jnp.transpose(c, (0, 1, 3, 2, 4)) for c in (cache_sb_k, cache_sb_v, cache_moba_k, cache_moba_v)]
    pt_flat = page_table.reshape(-1).astype(jnp.int32)

    n_dist = past + dn + 1
    buckets = _rel_bucket_np(np.arange(n_dist))
    bias_t = rel_bias.astype(F32).T
    bias_tiles = _toeplitz_bias_tiles(bias_t, buckets, n // MOBA_BLOCK)
    bias_past = _decode_bias_rows(bias_t, buckets, past, dn)
    dist_own = np.maximum(np.arange(dn)[:, None] - np.arange(PAGE_SIZE)[None, :], 0)
    bias_own = _take_static(bias_t, buckets[dist_own]).reshape(nh * dn, PAGE_SIZE)

    tm_proj = min(n, IN_PROJ_ROWS)
    tm_p = min(n, ROW_TILE)
    hist0 = jnp.zeros((nb, POOL_HIST, d_pool), F32)

    yp, ys = x_prompt.reshape(nb * n, d_model), x_sample.reshape(db * dn, d_model)
    ts = db * dn
    new_stack = lambda: jnp.zeros((depth, nb, nh, n, HEAD_DIM), F32)
    kv_stacks = (new_stack(), new_stack(), new_stack())
    mk_stack = new_stack()
    pool_p, kv_s, pool_s = [], [], []
    for l in range(depth):
        lidx = jnp.full((1,), l, jnp.int32)

        p = _in_proj(lidx, cm_main, yp, norm_g3, w_in_bf, nb, n, tm_proj, IN_PROJ_COLS)
        sbk, sbv, kc_raw, mbv = _kv_proj(lidx, cm_kv, yp, norm_g3, w_in_bf, kv_stacks, depth, nb, n, nh, tm_proj,
                                         (True, True, False, True))
        kv_stacks = (sbk, sbv, mbv)
        u3 = p.reshape(nb, n, -1)[:, :, m_u:m_u + d_pool]
        a = _pool(lidx, p, _pool_halo(u3, hist0, tm_p), w_pool, pool_scale3, nb, n, tm_p, 0, m_u, m_ga, BF16)
        bmix = _sb_prefill(lidx, p, sbk, sbv, nb, n, nh, m_qb, m_gb, SB_TILE)
        cmix, mk_stack = _moba_prefill(lidx, p, kc_raw, mbv, mk_stack, bias_tiles, q_gain, k_gain,
                                       depth, nb, n, nh, m_qc, m_gc)
        yp = _out_proj(lidx, yp, a, bmix, cmix, w_out_bf, tm_p, in_place=l > 0)
        pool_p.append(u3[:, n - POOL_HIST:])

        ps = _in_proj(lidx, cm_main, ys, norm_g3, w_in_bf, 1, ts, ts, IN_PROJ_COLS, g_main, norm_main)
        kvs = _in_proj(lidx, cm_kv, ys, norm_g3, w_in_bf, 1, ts, ts, d_att, g_kv, norm_kv)
        ps3 = ps.reshape(db, dn, -1)
        kvs5 = jnp.transpose(kvs.reshape(db, dn, 4, nh, HEAD_DIM), (2, 0, 3, 1, 4))
        hist_s = state_pool[l]
        us3 = ps3[:, :, m_u:m_u + d_pool]
        a_s = _pool(lidx, ps, _pool_halo(us3, hist_s, dn), w_pool, pool_scale3, db, dn, dn, past, m_u, m_ga, F32)
        col = lambda c: ps3[:, :, c:c + d_att]
        new_page = lambda t: jnp.pad(kvs5[t], ((0, 0), (0, 0), (0, PAGE_SIZE - dn), (0, 0)))
        b_s = _sb_decode(lidx, pt_flat, col(m_qb), new_page(0), new_page(1), col(m_gb),
                         caches[0], caches[1], db, dn, nh, n_pages)
        scores, km = _moba_scores(lidx, pt_flat, col(m_qc), caches[2], db, dn, nh, n_pages)
        c_s = _moba_values(lidx, pt_flat, col(m_qc), km, scores, bias_past, new_page(2), new_page(3),
                           bias_own, col(m_gc), caches[3], db, dn, nh, n_pages)
        ys = _out_proj(lidx, ys, a_s, b_s.reshape(ts, d_att), c_s.reshape(ts, d_att), w_out_bf, ts, in_place=l > 0)
        kv_s.append(kvs5)
        pool_s.append(jnp.concatenate([hist_s, us3], axis=1)[:, -POOL_HIST:])

    rows = lambda t: jnp.transpose(t, (0, 1, 3, 2, 4))
    sbk, sbv, mbv = kv_stacks
    kv_s = jnp.stack(kv_s, axis=1)
    return ((yp.reshape(nb, n, d_model), ys.reshape(db, dn, d_model))
            + (rows(sbk), rows(sbv), rows(mk_stack), rows(mbv), jnp.stack(pool_p))
            + tuple(rows(kv_s[t]) for t in range(4)) + (jnp.stack(pool_s),))
```

```python
import functools
import math

import numpy as np
import jax
import jax.numpy as jnp
from jax import lax
from jax.experimental import pallas as pl
from jax.experimental.pallas import tpu as pltpu

F32 = jnp.float32
BF16 = jnp.bfloat16

HEAD_DIM = 128
POOL_WINDOWS = (2, 4, 8, 16)
POOL_GROUP = 128
POOL_HIST = max(POOL_WINDOWS) - 1
HALO = POOL_HIST + 1
PAGE_SIZE = 128
MOBA_BLOCK = 256
MOBA_TOPK = 3
REL_BUCKETS = 32
REL_MAX_DIST = 128
EPS = 1e-6
LANES = 128
SUBLANES = 8
VMEM_LIMIT_BYTES = 56 * 1024 * 1024
IN_PROJ_ROWS = 1024
IN_PROJ_COLS = 1024
ROW_TILE = 512
COL_SLAB = 256
OUT_PROJ_K = 256
SB_TILE = 256
SB_HEADS_PER_STEP = 6
MOBA_HEADS_PER_STEP = 2
SB_PAGES_PER_CHUNK = 4
MOBA_PAGES_PER_STEP = 32
NEG_INF = float("-inf")
NEG_BIG = -1e30
UNDERFLOW_LOG = -104.0
LOG2E = 1.4426950408889634
LN2 = 0.6931471805599453

NT_DIMS = (((1,), (1,)), ((), ()))


def _dot(a, b):
    return jnp.dot(a, b, preferred_element_type=F32)


def _dot_nt(a, b):
    return lax.dot_general(a, b, NT_DIMS, preferred_element_type=F32)


def _split_bf16(x):
    hi = x.astype(BF16)
    lo = (x - hi.astype(F32)).astype(BF16)
    return hi, lo


def _log_sigmoid(z):
    return jnp.minimum(z, 0.0) - jnp.log(1.0 + jnp.exp(-jnp.abs(z)))


def _silu(g):
    return g / (1.0 + jnp.exp(-g))


def _rms_norm(x, gain):
    ms = jnp.mean(x * x, axis=-1, keepdims=True)
    return x * lax.rsqrt(ms + EPS) * gain


def _lanes(x, width):
    return x if width == LANES else jnp.concatenate([x] * (width // LANES), axis=1)


def _params(semantics):
    return pltpu.CompilerParams(dimension_semantics=semantics, vmem_limit_bytes=VMEM_LIMIT_BYTES)


def _weight_slab_specs(n_slabs, d_model):
    return [pl.BlockSpec((None, d_model, COL_SLAB), lambda b, i, j, l, cm, k=k: (l[0], 0, cm[j * n_slabs + k]))
            for k in range(n_slabs)]


def _in_proj_kernel(l_ref, cm_ref, x_ref, g_ref, *rest, n_slabs, norm_heads):
    w_refs, rest = rest[:n_slabs], rest[n_slabs:]
    if norm_heads is None:
        o_ref, h_ref = rest
    else:
        qkg_ref, o_ref, h_ref = rest
    j = pl.program_id(2)

    @pl.when(j == 0)
    def _():
        h_ref[...] = _rms_norm(x_ref[...], g_ref[...]).astype(BF16)

    heads_per_slab = COL_SLAB // HEAD_DIM
    for k, w_ref in enumerate(w_refs):
        acc = _dot(h_ref[...], w_ref[...])
        if norm_heads is None:
            o_ref[:, k * COL_SLAB:(k + 1) * COL_SLAB] = acc
            continue
        for s in range(heads_per_slab):
            sl = slice(s * HEAD_DIM, (s + 1) * HEAD_DIM)
            osl = slice(k * COL_SLAB + s * HEAD_DIM, k * COL_SLAB + (s + 1) * HEAD_DIM)
            head = (j * n_slabs + k) * heads_per_slab + s
            needs_norm = jnp.logical_and(head >= norm_heads[0], head < norm_heads[1])

            @pl.when(needs_norm)
            def _(acc=acc, sl=sl, osl=osl):
                o_ref[:, osl] = _rms_norm(acc[:, sl], qkg_ref[:, osl])

            @pl.when(jnp.logical_not(needs_norm))
            def _(acc=acc, sl=sl, osl=osl):
                o_ref[:, osl] = acc[:, sl]


def _in_proj(lidx, col_map, x2d, norm_g, w_bf, nb, n, tm, tn, qkg=None, norm_cols=None):
    T, d_model = x2d.shape
    cols = col_map.shape[0] * COL_SLAB
    assert T == nb * n and n % tm == 0 and cols % tn == 0 and tn % COL_SLAB == 0
    nt = n // tm
    n_slabs = tn // COL_SLAB
    norm_heads = None if qkg is None else (norm_cols[0] // HEAD_DIM, norm_cols[1] // HEAD_DIM)
    in_specs = [
        pl.BlockSpec((tm, d_model), lambda b, i, j, l, cm: (b * nt + i, 0)),
        pl.BlockSpec((None, 1, d_model), lambda b, i, j, l, cm: (l[0], 0, 0)),
    ] + _weight_slab_specs(n_slabs, d_model)
    args = [lidx, col_map, x2d, norm_g] + [w_bf] * n_slabs
    if qkg is not None:
        in_specs.append(pl.BlockSpec((None, 1, tn), lambda b, i, j, l, cm: (l[0], 0, j)))
        args.append(qkg)
    return pl.pallas_call(
        functools.partial(_in_proj_kernel, n_slabs=n_slabs, norm_heads=norm_heads),
        grid_spec=pltpu.PrefetchScalarGridSpec(
            num_scalar_prefetch=2,
            grid=(nb, nt, cols // tn),
            in_specs=in_specs,
            out_specs=pl.BlockSpec((tm, tn), lambda b, i, j, l, cm: (b * nt + i, j)),
            scratch_shapes=[pltpu.VMEM((tm, d_model), BF16)],
        ),
        out_shape=jax.ShapeDtypeStruct((T, cols), F32),
        compiler_params=_params(("parallel", "parallel", "arbitrary")),
        name="in_proj",
    )(*args)


def _kv_proj_kernel(l_ref, cm_ref, x_ref, g_ref, *rest, n_slabs, n_stacks):
    w_refs = rest[:n_slabs]
    outs, h_ref = rest[n_slabs + n_stacks:-1], rest[-1]
    j = pl.program_id(2)

    @pl.when(j == 0)
    def _():
        h_ref[...] = _rms_norm(x_ref[...], g_ref[...]).astype(BF16)

    heads_per_slab = COL_SLAB // HEAD_DIM
    for kind, o_ref in enumerate(outs):
        @pl.when(j == kind)
        def _(o_ref=o_ref):
            for k, w_ref in enumerate(w_refs):
                acc = _dot(h_ref[...], w_ref[...])
                for s in range(heads_per_slab):
                    o_ref[k * heads_per_slab + s] = acc[:, s * HEAD_DIM:(s + 1) * HEAD_DIM]


def _kv_proj(lidx, col_map, x2d, norm_g, w_bf, stacks, depth, nb, n, nh, tm, stacked):
    T, d_model = x2d.shape
    tn = nh * HEAD_DIM
    assert T == nb * n and n % tm == 0 and col_map.shape[0] * COL_SLAB == len(stacked) * tn and tn % COL_SLAB == 0
    nt = n // tm
    n_slabs = tn // COL_SLAB
    out_specs, out_shapes = [], []
    for is_stacked in stacked:
        if is_stacked:
            out_specs.append(pl.BlockSpec((None, None, nh, tm, HEAD_DIM), lambda b, i, j, l, cm: (l[0], b, 0, i, 0)))
            out_shapes.append(jax.ShapeDtypeStruct((depth, nb, nh, n, HEAD_DIM), F32))
        else:
            out_specs.append(pl.BlockSpec((None, nh, tm, HEAD_DIM), lambda b, i, j, l, cm: (b, 0, i, 0)))
            out_shapes.append(jax.ShapeDtypeStruct((nb, nh, n, HEAD_DIM), F32))
    stacks = list(stacks)
    stacked_outs = [k for k, is_stacked in enumerate(stacked) if is_stacked]
    assert len(stacks) == len(stacked_outs)
    n_fixed = 4 + n_slabs
    return pl.pallas_call(
        functools.partial(_kv_proj_kernel, n_slabs=n_slabs, n_stacks=len(stacks)),
        grid_spec=pltpu.PrefetchScalarGridSpec(
            num_scalar_prefetch=2,
            grid=(nb, nt, len(stacked)),
            in_specs=[
                pl.BlockSpec((tm, d_model), lambda b, i, j, l, cm: (b * nt + i, 0)),
                pl.BlockSpec((None, 1, d_model), lambda b, i, j, l, cm: (l[0], 0, 0)),
            ] + _weight_slab_specs(n_slabs, d_model) + [pl.BlockSpec(memory_space=pl.ANY)] * len(stacks),
            out_specs=out_specs,
            scratch_shapes=[pltpu.VMEM((tm, d_model), BF16)],
        ),
        out_shape=out_shapes,
        input_output_aliases={n_fixed + s: stacked_outs[s] for s in range(len(stacks))},
        compiler_params=_params(("parallel", "parallel", "arbitrary")),
        name="kv_proj",
    )(lidx, col_map, x2d, norm_g, *([w_bf] * n_slabs), *stacks)


def _pool_kernel(l_ref, u_ref, ga_ref, halo_ref, wp_ref, ps_ref, o_ref, ext_ref, *, tm, pos0):
    i = pl.program_id(1)
    ext_ref[0:HALO, :] = halo_ref[...]
    ext_ref[HALO:, :] = u_ref[...]
    pos = pos0 + i * tm + lax.broadcasted_iota(jnp.int32, (tm, 1), 0)
    for g, w in enumerate(POOL_WINDOWS):
        sl = slice(g * POOL_GROUP, (g + 1) * POOL_GROUP)
        s = ext_ref[HALO:HALO + tm, sl]
        for d in range(1, w):
            s = s + ext_ref[HALO - d:HALO - d + tm, sl]
        cnt = jnp.minimum(w, pos + 1).astype(F32)
        pooled = s / cnt - u_ref[:, sl]
        y = _dot(pooled.astype(BF16), wp_ref[g].astype(BF16)) * ps_ref[:, sl]
        o_ref[:, sl] = (y * _silu(ga_ref[:, sl])).astype(o_ref.dtype)


def _pool(lidx, p2d, halo, w_pool, pool_scale, nb, n, tm, pos0, u_col, g_col, out_dtype):
    d_pool = POOL_GROUP * len(POOL_WINDOWS)
    nt = n // tm
    assert u_col % d_pool == 0 and g_col % d_pool == 0
    uc, gc = u_col // d_pool, g_col // d_pool
    kern = functools.partial(_pool_kernel, tm=tm, pos0=pos0)
    return pl.pallas_call(
        kern,
        grid_spec=pltpu.PrefetchScalarGridSpec(
            num_scalar_prefetch=1,
            grid=(nb, nt),
            in_specs=[
                pl.BlockSpec((tm, d_pool), lambda b, i, l: (b * nt + i, uc)),
                pl.BlockSpec((tm, d_pool), lambda b, i, l: (b * nt + i, gc)),
                pl.BlockSpec((None, None, HALO, d_pool), lambda b, i, l: (b, i, 0, 0)),
                pl.BlockSpec((None, len(POOL_WINDOWS), POOL_GROUP, POOL_GROUP), lambda b, i, l: (l[0], 0, 0, 0)),
                pl.BlockSpec((None, 1, d_pool), lambda b, i, l: (l[0], 0, 0)),
            ],
            out_specs=pl.BlockSpec((tm, d_pool), lambda b, i, l: (b * nt + i, 0)),
            scratch_shapes=[pltpu.VMEM((HALO + tm, d_pool), F32)],
        ),
        out_shape=jax.ShapeDtypeStruct((nb * n, d_pool), out_dtype),
        compiler_params=_params(("parallel", "arbitrary")),
        name="pool",
    )(lidx, p2d, p2d, halo, w_pool, pool_scale)


def _sb_prefill_kernel(l_ref, q_ref, k_ref, v_ref, g_ref, o_ref, acc_ref, cs_ref, *, tq, scale):
    i = pl.program_id(2)
    hps = SB_HEADS_PER_STEP
    qs = [q_ref[:, h * HEAD_DIM:(h + 1) * HEAD_DIM].astype(BF16) for h in range(hps)]
    row = lax.broadcasted_iota(jnp.int32, (tq, tq), 0)
    col = lax.broadcasted_iota(jnp.int32, (tq, tq), 1)
    later = jnp.where(row > col, 1.0, 0.0).astype(BF16)

    def logits(kj):
        off = pl.multiple_of(kj * tq, tq)
        return jnp.concatenate([_dot_nt(qs[h], k_ref[h, pl.ds(off, tq), :].astype(BF16)) for h in range(hps)],
                               axis=0) * scale

    def tiles(z, kj, diagonal):
        off = pl.multiple_of(kj * tq, tq)
        ls = _log_sigmoid(z)
        lk = ls - z
        if diagonal:
            keep = jnp.concatenate([col < row] * hps, axis=0)
            lk = jnp.where(keep, lk, 0.0)
        hi, lo = _split_bf16(lk)
        la = _dot(hi, later) + _dot(lo, later) + _lanes(cs_ref[...], tq)
        a = jnp.exp(ls + la)
        if diagonal:
            a = jnp.where(keep, a, 0.0)
        for h in range(hps):
            v = v_ref[h, pl.ds(off, tq), :].astype(BF16)
            acc_ref[h] += _dot(a[h * tq:(h + 1) * tq].astype(BF16), v)
        cs = cs_ref[...] + jnp.sum(lk, axis=1, keepdims=True)
        cs_ref[...] = cs
        return jnp.max(cs)

    acc_ref[...] = jnp.zeros_like(acc_ref)
    cs_ref[...] = jnp.zeros_like(cs_ref)
    z_next = logits(jnp.maximum(i - 1, 0))
    cs_max = tiles(logits(i), i, True)

    def more(state):
        t, cs_max, _ = state
        return jnp.logical_and(t < i, cs_max > UNDERFLOW_LOG)

    def body(state):
        t, _, z = state
        z_after = logits(jnp.maximum(i - 2 - t, 0))
        return t + 1, tiles(z, i - 1 - t, False), z_after

    lax.while_loop(more, body, (0, cs_max, z_next))
    gate = _silu(g_ref[...])
    for h in range(hps):
        sl = slice(h * HEAD_DIM, (h + 1) * HEAD_DIM)
        o_ref[:, sl] = (acc_ref[h] * gate[:, sl]).astype(o_ref.dtype)


def _sb_prefill(lidx, p2d, k_stack, v_stack, nb, n, nh, q_col, g_col, tq):
    hps = SB_HEADS_PER_STEP
    width = hps * HEAD_DIM
    nq = n // tq
    assert nh % hps == 0 and q_col % width == 0 and g_col % width == 0
    kern = functools.partial(_sb_prefill_kernel, tq=tq, scale=1.0 / math.sqrt(HEAD_DIM))
    qc, gc = q_col // width, g_col // width
    heads = pl.BlockSpec((None, None, hps, n, HEAD_DIM), lambda b, h, i, l: (l[0], b, h, 0, 0))
    return pl.pallas_call(
        kern,
        grid_spec=pltpu.PrefetchScalarGridSpec(
            num_scalar_prefetch=1,
            grid=(nb, nh // hps, nq),
            in_specs=[
                pl.BlockSpec((tq, width), lambda b, h, i, l: (b * nq + i, qc + h)),
                heads, heads,
                pl.BlockSpec((tq, width), lambda b, h, i, l: (b * nq + i, gc + h)),
            ],
            out_specs=pl.BlockSpec((tq, width), lambda b, h, i, l: (b * nq + i, h)),
            scratch_shapes=[pltpu.VMEM((hps, tq, HEAD_DIM), F32), pltpu.VMEM((hps * tq, LANES), F32)],
        ),
        out_shape=jax.ShapeDtypeStruct((nb * n, nh * HEAD_DIM), BF16),
        compiler_params=_params(("parallel", "parallel", "arbitrary")),
        name="sb_prefill",
    )(lidx, p2d, k_stack, v_stack, p2d)


def _top_k_lanes(gate, k):
    lane = lax.broadcasted_iota(jnp.int32, gate.shape, 1).astype(F32)
    sel = jnp.zeros(gate.shape, F32)
    for _ in range(k):
        m = jnp.max(gate, axis=1, keepdims=True)
        cand = jnp.where(gate == m, lane, float(LANES))
        cand = jnp.where(m > NEG_INF, cand, float(LANES))
        idx = jnp.min(cand, axis=1, keepdims=True)
        pick = lane == idx
        sel = jnp.where(pick, 1.0, sel)
        gate = jnp.where(pick, NEG_INF, gate)
    return sel


def _top_k_sublanes(gate_t, k, n_blocks):
    block = lax.broadcasted_iota(jnp.int32, gate_t.shape, 0)
    rank = jnp.zeros(gate_t.shape, F32)
    for b2 in range(n_blocks):
        other = jnp.broadcast_to(gate_t[b2:b2 + 1, :], gate_t.shape)
        wins_tie = jnp.where(block > b2, 1.0, 0.0)
        rank = rank + jnp.where(other > gate_t, 1.0, jnp.where(other == gate_t, wins_tie, 0.0))
    return jnp.where(jnp.logical_and(rank < k, gate_t > NEG_INF), 1.0, 0.0)


def _gate_scores(a, b):
    a_hi, a_lo = _split_bf16(a)
    b_hi, b_lo = _split_bf16(b)
    return _dot_nt(a_hi, b_hi) + _dot_nt(a_hi, b_lo) + _dot_nt(a_lo, b_hi)


def _moba_prefill_kernel(l_ref, q_ref, k_ref, v_ref, g_ref, bias_ref, qg_ref, kg_ref, *rest, n_stacks, nblk, scale):
    o_ref, kn_ref, km_ref, qn_ref, nc_ref, m_ref, l_scr, acc_ref = rest[n_stacks:]
    i = pl.program_id(2)
    hps = MOBA_HEADS_PER_STEP
    blk = MOBA_BLOCK
    n = nblk * blk
    row = lax.broadcasted_iota(jnp.int32, (blk, blk), 0)
    col = lax.broadcasted_iota(jnp.int32, (blk, blk), 1)
    block_id = lax.broadcasted_iota(jnp.int32, (LANES, blk), 0)

    @pl.when(i == 0)
    def _():
        eye = jnp.where(row == col, 1.0, 0.0).astype(BF16)
        gate_rows = -(-nblk // SUBLANES) * SUBLANES
        gate_block = lax.broadcasted_iota(jnp.int32, (gate_rows, n), 0)
        query_block = lax.shift_right_logical(lax.broadcasted_iota(jnp.int32, (gate_rows, n), 1), int(math.log2(blk)))
        km_ref[...] = jnp.zeros_like(km_ref)
        for h in range(hps):
            for b in range(nblk):
                kn = _rms_norm(k_ref[h, b * blk:(b + 1) * blk, :], kg_ref[...])
                kn_ref[h, b * blk:(b + 1) * blk, :] = kn
                km_ref[h, b:b + 1, :] = jnp.sum(kn, axis=0, keepdims=True) * (1.0 / blk)
            qf = _rms_norm(q_ref[:, h * HEAD_DIM:(h + 1) * HEAD_DIM], qg_ref[...])
            qn_ref[h] = qf.astype(BF16)
            gate_t = jnp.where(gate_block < query_block, _gate_scores(km_ref[h, 0:gate_rows, :], qf), NEG_INF)
            mask_t = jnp.where(_top_k_sublanes(gate_t, MOBA_TOPK, nblk) > 0.0, 0.0, NEG_BIG)
            if gate_rows < LANES:
                mask_t = jnp.concatenate([mask_t, jnp.zeros((LANES - gate_rows, n), F32)], axis=0)
            mask_t = mask_t.astype(BF16)
            for t in range(nblk):
                nc_ref[t, h * blk:(h + 1) * blk, :] = _dot_nt(eye, mask_t[:, t * blk:(t + 1) * blk]).astype(BF16)

    row0 = pl.multiple_of(i * blk, blk)
    qs = [qn_ref[h, pl.ds(row0, blk), :] for h in range(hps)]
    not_chosen = nc_ref[i]

    m_ref[...] = jnp.full_like(m_ref, NEG_INF)
    l_scr[...] = jnp.zeros_like(l_scr)
    acc_ref[...] = jnp.zeros_like(acc_ref)

    def raw_scores(j):
        off = pl.multiple_of(j * blk, blk)
        return jnp.concatenate([_dot_nt(qs[h], kn_ref[h, pl.ds(off, blk), :].astype(BF16)) * scale + bias_ref[h, i - j]
                                for h in range(hps)], axis=0)

    s_own = jnp.where(jnp.concatenate([col <= row] * hps, axis=0), raw_scores(i), NEG_INF)

    def masked(raw, j):
        return raw + _dot(not_chosen, jnp.where(block_id == j, 1.0, 0.0).astype(BF16))

    def scores(j):
        return masked(raw_scores(j), j)

    def accumulate(s, j):
        off = pl.multiple_of(j * blk, blk)
        m_prev = m_ref[...]
        m_new = jnp.maximum(m_prev, jnp.max(s, axis=1, keepdims=True))
        alpha = jnp.exp(m_prev - m_new)
        p = jnp.exp(s - _lanes(m_new, blk))
        l_scr[...] = alpha * l_scr[...] + jnp.sum(p, axis=1, keepdims=True)
        for h in range(hps):
            rows = slice(h * blk, (h + 1) * blk)
            v = v_ref[h, pl.ds(off, blk), :].astype(BF16)
            acc_ref[h] = alpha[rows] * acc_ref[h] + _dot(p[rows].astype(BF16), v)
        m_ref[...] = m_new

    s_first = scores(0)
    accumulate(s_own, i)

    def body(j, s):
        s_next = scores(jnp.minimum(j + 1, jnp.maximum(i - 1, 0)))
        accumulate(s, j)
        return s_next

    lax.fori_loop(0, i, body, s_first)
    gate = _silu(g_ref[...])
    inv_l = 1.0 / l_scr[...]
    for h in range(hps):
        sl = slice(h * HEAD_DIM, (h + 1) * HEAD_DIM)
        o_ref[:, sl] = (acc_ref[h] * inv_l[h * blk:(h + 1) * blk] * gate[:, sl]).astype(o_ref.dtype)


def _moba_prefill(lidx, p2d, k_raw, v_stack, k_stack, bias_tiles, q_gain, k_gain, depth, nb, n, nh, q_col, g_col):
    hps = MOBA_HEADS_PER_STEP
    width = hps * HEAD_DIM
    blk = MOBA_BLOCK
    assert n % blk == 0
    nq = n // blk
    assert nq <= LANES
    assert nh % hps == 0 and q_col % width == 0 and g_col % width == 0
    extend = [k_stack]
    kern = functools.partial(_moba_prefill_kernel, n_stacks=len(extend), nblk=nq, scale=1.0 / math.sqrt(HEAD_DIM))
    qc, gc = q_col // width, g_col // width
    gain = pl.BlockSpec((None, 1, HEAD_DIM), lambda h, b, i, l: (l[0], 0, 0))
    layer_heads = pl.BlockSpec((None, None, hps, n, HEAD_DIM), lambda h, b, i, l: (l[0], b, h, 0, 0))
    n_fixed = 8
    return pl.pallas_call(
        kern,
        grid_spec=pltpu.PrefetchScalarGridSpec(
            num_scalar_prefetch=1,
            grid=(nh // hps, nb, nq),
            in_specs=[
                pl.BlockSpec((n, width), lambda h, b, i, l: (b, qc + h)),
                pl.BlockSpec((None, hps, n, HEAD_DIM), lambda h, b, i, l: (b, h, 0, 0)),
                layer_heads,
                pl.BlockSpec((blk, width), lambda h, b, i, l: (b * nq + i, gc + h)),
                pl.BlockSpec((hps, nq, blk, blk), lambda h, b, i, l: (h, 0, 0, 0)),
                gain, gain,
            ] + [pl.BlockSpec(memory_space=pl.ANY)] * len(extend),
            out_specs=[
                pl.BlockSpec((blk, width), lambda h, b, i, l: (b * nq + i, h)),
                layer_heads,
            ],
            scratch_shapes=[pltpu.VMEM((hps, LANES, HEAD_DIM), F32),
                            pltpu.VMEM((hps, n, HEAD_DIM), BF16),
                            pltpu.VMEM((nq, hps * blk, LANES), BF16),
                            pltpu.VMEM((hps * blk, LANES), F32), pltpu.VMEM((hps * blk, LANES), F32),
                            pltpu.VMEM((hps, blk, HEAD_DIM), F32)],
        ),
        out_shape=[jax.ShapeDtypeStruct((nb * n, nh * HEAD_DIM), BF16),
                   jax.ShapeDtypeStruct((depth, nb, nh, n, HEAD_DIM), F32)],
        input_output_aliases={n_fixed: 1},
        compiler_params=_params(("parallel", "parallel", "arbitrary")),
        name="moba_prefill",
    )(lidx, p2d, k_raw, v_stack, p2d, bias_tiles, q_gain, k_gain, *extend)


def _head_queries(q_ref, nh):
    return [q_ref[:, h * HEAD_DIM:(h + 1) * HEAD_DIM] for h in range(nh)]


def _stacked_scores(qs, k_ref):
    return jnp.concatenate([_dot_nt(q, k_ref[h].astype(BF16)) for h, q in enumerate(qs)], axis=0)


def _stacked_values(p, v_ref, nh, dn):
    return jnp.concatenate(
        [_dot(p[h * dn:(h + 1) * dn].astype(BF16), v_ref[h].astype(BF16)) for h in range(nh)], axis=0)


def _gated_heads(acc, g, nh, dn):
    return jnp.concatenate([acc[h * dn:(h + 1) * dn] for h in range(nh)], axis=1) * _silu(g)


def _page_index_map(r, pps, n_pages, reverse):
    def index_map(b, s, l, pt):
        page = s * pps + r
        if reverse:
            page = n_pages - 1 - page
        return (l[0], pt[b * n_pages + page], 0, 0, 0)

    return index_map


def _page_specs(pps, nh, n_pages, reverse):
    return [pl.BlockSpec((None, None, nh, PAGE_SIZE, HEAD_DIM), _page_index_map(r, pps, n_pages, reverse))
            for r in range(pps)]


def _sb_decode_kernel(l_ref, pt_ref, q_ref, kn_ref, vn_ref, g_ref, ck_hbm, cv_hbm, o_ref,
                      kbuf, vbuf, sems, acc_ref, cs_ref, *, nh, dn, n_pages, scale):
    ppc = SB_PAGES_PER_CHUNK
    n_chunks = n_pages // ppc
    b = pl.program_id(0)
    rows = nh * dn

    def page_copies(c, slot):
        copies = []
        for r in range(ppc):
            page = pt_ref[b * n_pages + (n_pages - 1 - (c * ppc + r))]
            copies.append(pltpu.make_async_copy(ck_hbm.at[l_ref[0], page], kbuf.at[slot, r], sems.at[0, slot]))
            copies.append(pltpu.make_async_copy(cv_hbm.at[l_ref[0], page], vbuf.at[slot, r], sems.at[1, slot]))
        return copies

    qs = [q.astype(BF16) for q in _head_queries(q_ref, nh)]
    row = lax.broadcasted_iota(jnp.int32, (PAGE_SIZE, PAGE_SIZE), 0)
    col = lax.broadcasted_iota(jnp.int32, (PAGE_SIZE, PAGE_SIZE), 1)
    later = jnp.where(row > col, 1.0, 0.0).astype(BF16)

    def pages(kp_refs, vp_refs, keep):
        z = jnp.concatenate([_stacked_scores(qs, kp) for kp in kp_refs], axis=0) * scale
        ls = _log_sigmoid(z)
        lk = ls - z
        if keep is not None:
            lk = jnp.where(keep, lk, 0.0)
        hi, lo = _split_bf16(lk)
        within = _dot(hi, later) + _dot(lo, later)
        tot = jnp.sum(lk, axis=1, keepdims=True)
        carry = cs_ref[...]
        carries = []
        for r in range(len(kp_refs)):
            carries.append(carry)
            carry = carry + tot[r * rows:(r + 1) * rows]
        cs_ref[...] = carry
        a = jnp.exp(ls + within + jnp.concatenate(carries, axis=0))
        if keep is not None:
            a = jnp.where(keep, a, 0.0)
        upd = _stacked_values(a[0:rows], vp_refs[0], nh, dn)
        for r in range(1, len(kp_refs)):
            upd = upd + _stacked_values(a[r * rows:(r + 1) * rows], vp_refs[r], nh, dn)
        acc_ref[...] += upd
        return jnp.max(carry)

    for cp in page_copies(0, 0):
        cp.start()
    acc_ref[...] = jnp.zeros_like(acc_ref)
    cs_ref[...] = jnp.zeros_like(cs_ref)
    tok = lax.broadcasted_iota(jnp.int32, (rows, PAGE_SIZE), 0) & (dn - 1)
    key = lax.broadcasted_iota(jnp.int32, (rows, PAGE_SIZE), 1)
    cs_max = pages([kn_ref], [vn_ref], key < tok)

    def more(state):
        c, cs_max = state
        return jnp.logical_and(c < n_chunks, cs_max > UNDERFLOW_LOG)

    def trip(state):
        c, _ = state
        slot = c & 1

        @pl.when(c + 1 < n_chunks)
        def _():
            for cp in page_copies(c + 1, 1 - slot):
                cp.start()

        for cp in page_copies(c, slot):
            cp.wait()
        cs_max = pages([kbuf.at[slot, r] for r in range(ppc)], [vbuf.at[slot, r] for r in range(ppc)], None)
        return c + 1, cs_max

    c_end, _ = lax.while_loop(more, trip, (0, cs_max))

    @pl.when(c_end < n_chunks)
    def _():
        for cp in page_copies(c_end, c_end & 1):
            cp.wait()

    o_ref[...] = _gated_heads(acc_ref[...], g_ref[...], nh, dn).astype(o_ref.dtype)


def _sb_decode(lidx, pt_flat, q, k_new, v_new, g, cache_k, cache_v, nb, dn, nh, n_pages):
    ppc = SB_PAGES_PER_CHUNK
    assert n_pages % ppc == 0 and dn == SUBLANES
    width = nh * HEAD_DIM
    kern = functools.partial(_sb_decode_kernel, nh=nh, dn=dn, n_pages=n_pages, scale=1.0 / math.sqrt(HEAD_DIM))
    seq = lambda rows: pl.BlockSpec((None, rows, width), lambda b, l, pt: (b, 0, 0))
    new = pl.BlockSpec((None, nh, PAGE_SIZE, HEAD_DIM), lambda b, l, pt: (b, 0, 0, 0))
    hbm = pl.BlockSpec(memory_space=pl.ANY)
    chunk = (2, ppc, nh, PAGE_SIZE, HEAD_DIM)
    return pl.pallas_call(
        kern,
        grid_spec=pltpu.PrefetchScalarGridSpec(
            num_scalar_prefetch=2,
            grid=(nb,),
            in_specs=[seq(dn), new, new, seq(dn), hbm, hbm],
            out_specs=seq(dn),
            scratch_shapes=[pltpu.VMEM(chunk, F32), pltpu.VMEM(chunk, F32), pltpu.SemaphoreType.DMA((2, 2)),
                            pltpu.VMEM((nh * dn, HEAD_DIM), F32), pltpu.VMEM((nh * dn, LANES), F32)],
        ),
        out_shape=jax.ShapeDtypeStruct((nb, dn, width), F32),
        compiler_params=_params(("arbitrary",)),
        name="sb_decode",
    )(lidx, pt_flat, q, k_new, v_new, g, cache_k, cache_v)


def _moba_scores_kernel(l_ref, pt_ref, q_ref, *rest, nh, dn):
    pps = MOBA_PAGES_PER_STEP
    k_refs = rest[:pps]
    s_ref, km_ref = rest[pps:]
    s_idx = pl.program_id(1)
    qs = [q.astype(BF16) for q in _head_queries(q_ref, nh)]
    pages_per_block = MOBA_BLOCK // PAGE_SIZE
    blocks_per_step = pps // pages_per_block

    @pl.when(s_idx == 0)
    def _():
        km_ref[...] = jnp.zeros_like(km_ref)

    sub = lax.broadcasted_iota(jnp.int32, (blocks_per_step, HEAD_DIM), 0)
    for h in range(nh):
        means = jnp.zeros((blocks_per_step, HEAD_DIM), F32)
        for c in range(blocks_per_step):
            tot = jnp.zeros((1, HEAD_DIM), F32)
            for r in range(c * pages_per_block, (c + 1) * pages_per_block):
                kp = k_refs[r][h]
                s_ref[h * dn:(h + 1) * dn, r * PAGE_SIZE:(r + 1) * PAGE_SIZE] = _dot_nt(qs[h], kp.astype(BF16))
                tot = tot + jnp.sum(kp, axis=0, keepdims=True)
            means = jnp.where(sub == c, tot * (1.0 / MOBA_BLOCK), means)
        km_ref[h, pl.ds(pl.multiple_of(s_idx * blocks_per_step, blocks_per_step), blocks_per_step), :] = means


def _moba_scores(lidx, pt_flat, q, cache_k, nb, dn, nh, n_pages):
    pps = MOBA_PAGES_PER_STEP
    assert (pps // (MOBA_BLOCK // PAGE_SIZE)) % SUBLANES == 0 and n_pages % pps == 0
    width = nh * HEAD_DIM
    past = n_pages * PAGE_SIZE
    assert past // MOBA_BLOCK <= LANES
    kern = functools.partial(_moba_scores_kernel, nh=nh, dn=dn)
    return pl.pallas_call(
        kern,
        grid_spec=pltpu.PrefetchScalarGridSpec(
            num_scalar_prefetch=2,
            grid=(nb, n_pages // pps),
            in_specs=[pl.BlockSpec((None, dn, width), lambda b, s, l, pt: (b, 0, 0))]
            + _page_specs(pps, nh, n_pages, False),
            out_specs=[
                pl.BlockSpec((None, nh * dn, pps * PAGE_SIZE), lambda b, s, l, pt: (b, 0, s)),
                pl.BlockSpec((None, nh, LANES, HEAD_DIM), lambda b, s, l, pt: (b, 0, 0, 0)),
            ],
        ),
        out_shape=[jax.ShapeDtypeStruct((nb, nh * dn, past), F32),
                   jax.ShapeDtypeStruct((nb, nh, LANES, HEAD_DIM), F32)],
        compiler_params=_params(("parallel", "arbitrary")),
        name="moba_scores",
    )(lidx, pt_flat, q, *([cache_k] * pps))


def _moba_values_kernel(l_ref, pt_ref, q_ref, km_ref, s_ref, bias_ref, kn_ref, vn_ref, bown_ref, g_ref, *rest,
                        nh, dn, n_past_blocks, scale):
    pps = MOBA_PAGES_PER_STEP
    v_refs = rest[:pps]
    o_ref, sel_ref, m_ref, l_scr, acc_ref = rest[pps:]
    s_idx = pl.program_id(1)
    rows = nh * dn
    width = pps * PAGE_SIZE
    blocks_per_step = width // MOBA_BLOCK

    @pl.when(s_idx == 0)
    def _():
        qf = _head_queries(q_ref, nh)
        lane = lax.broadcasted_iota(jnp.int32, (rows, LANES), 1)
        gate = jnp.concatenate([_gate_scores(qf[h], km_ref[h]) for h in range(nh)], axis=0)
        gate = jnp.where(lane < n_past_blocks, gate, NEG_INF)
        sel_ref[...] = _top_k_lanes(gate, MOBA_TOPK)
        tok = lax.broadcasted_iota(jnp.int32, (rows, PAGE_SIZE), 0) & (dn - 1)
        key = lax.broadcasted_iota(jnp.int32, (rows, PAGE_SIZE), 1)
        s_own = _stacked_scores([q.astype(BF16) for q in qf], kn_ref) * scale + bown_ref[...]
        s_own = jnp.where(key <= tok, s_own, NEG_INF)
        m0 = jnp.max(s_own, axis=1, keepdims=True)
        p_own = jnp.exp(s_own - m0)
        m_ref[...] = jnp.broadcast_to(m0, m_ref.shape)
        l_scr[...] = jnp.broadcast_to(jnp.sum(p_own, axis=1, keepdims=True), l_scr.shape)
        acc_ref[...] = _stacked_values(p_own, vn_ref, nh, dn)

    key_in_step = lax.broadcasted_iota(jnp.int32, (LANES, width), 1)
    blk_of_key = lax.shift_right_logical(key_in_step, int(math.log2(MOBA_BLOCK))) + s_idx * blocks_per_step
    indicator = jnp.where(lax.broadcasted_iota(jnp.int32, (LANES, width), 0) == blk_of_key, 1.0, 0.0).astype(BF16)
    chosen = _dot(sel_ref[...].astype(BF16), indicator) > 0.5
    sm = jnp.where(chosen, s_ref[...] * scale + bias_ref[...], NEG_INF)
    m_prev = m_ref[...]
    m_new = jnp.maximum(m_prev, jnp.max(sm, axis=1, keepdims=True))
    alpha = jnp.exp(m_prev - m_new)
    p = jnp.exp(sm - _lanes(m_new, width))
    l_scr[...] = alpha * l_scr[...] + jnp.sum(p, axis=1, keepdims=True)
    upd = _stacked_values(p[:, 0:PAGE_SIZE], v_refs[0], nh, dn)
    for r in range(1, pps):
        upd = upd + _stacked_values(p[:, r * PAGE_SIZE:(r + 1) * PAGE_SIZE], v_refs[r], nh, dn)
    acc_ref[...] = alpha * acc_ref[...] + upd
    m_ref[...] = m_new

    @pl.when(s_idx == pl.num_programs(1) - 1)
    def _():
        o_ref[...] = _gated_heads(acc_ref[...] / l_scr[...], g_ref[...], nh, dn).astype(o_ref.dtype)


def _moba_values(lidx, pt_flat, q, km, scores, bias_past, k_new, v_new, bias_own, g, cache_v,
                 nb, dn, nh, n_pages):
    pps = MOBA_PAGES_PER_STEP
    width = nh * HEAD_DIM
    rows = nh * dn
    past = n_pages * PAGE_SIZE
    n_past_blocks = past // MOBA_BLOCK
    assert n_past_blocks <= LANES and km.shape[2] == LANES and n_pages % pps == 0
    assert (pps * PAGE_SIZE) % MOBA_BLOCK == 0
    kern = functools.partial(_moba_values_kernel, nh=nh, dn=dn, n_past_blocks=n_past_blocks,
                             scale=1.0 / math.sqrt(HEAD_DIM))
    seq = lambda r, c: pl.BlockSpec((None, r, c), lambda b, s, l, pt: (b, 0, 0))
    per_head = lambda r: pl.BlockSpec((None, nh, r, HEAD_DIM), lambda b, s, l, pt: (b, 0, 0, 0))
    return pl.pallas_call(
        kern,
        grid_spec=pltpu.PrefetchScalarGridSpec(
            num_scalar_prefetch=2,
            grid=(nb, n_pages // pps),
            in_specs=[seq(dn, width), per_head(LANES),
                      pl.BlockSpec((None, rows, pps * PAGE_SIZE), lambda b, s, l, pt: (b, 0, s)),
                      pl.BlockSpec((rows, pps * PAGE_SIZE), lambda b, s, l, pt: (0, s)),
                      per_head(PAGE_SIZE), per_head(PAGE_SIZE),
                      pl.BlockSpec((rows, PAGE_SIZE), lambda b, s, l, pt: (0, 0)),
                      seq(dn, width)] + _page_specs(pps, nh, n_pages, False),
            out_specs=seq(dn, width),
            scratch_shapes=[pltpu.VMEM((rows, LANES), F32), pltpu.VMEM((rows, LANES), F32),
                            pltpu.VMEM((rows, LANES), F32), pltpu.VMEM((rows, HEAD_DIM), F32)],
        ),
        out_shape=jax.ShapeDtypeStruct((nb, dn, width), F32),
        compiler_params=_params(("parallel", "arbitrary")),
        name="moba_values",
    )(lidx, pt_flat, q, km, scores, bias_past, k_new, v_new, bias_own, g, *([cache_v] * pps))


def _out_proj_kernel(l_ref, x_ref, a_ref, b_ref, c_ref, *rest):
    w_refs, o_ref = rest[:-1], rest[-1]
    acc, slab = None, 0
    for part in (a_ref, b_ref, c_ref):
        for c0 in range(0, part.shape[1], OUT_PROJ_K):
            term = _dot(part[:, c0:c0 + OUT_PROJ_K].astype(BF16), w_refs[slab][...])
            acc = term if acc is None else acc + term
            slab += 1
    o_ref[...] = x_ref[...] + acc


def _out_proj(lidx, x2d, a, b, c, w_bf, tm, in_place):
    T, d_model = x2d.shape
    widths = (a.shape[1], b.shape[1], c.shape[1])
    assert T % tm == 0 and sum(widths) == w_bf.shape[1] and all(w % OUT_PROJ_K == 0 for w in widths)
    act = lambda arr: pl.BlockSpec((tm, arr.shape[1]), lambda i, l: (i, 0))
    slabs = [pl.BlockSpec((None, OUT_PROJ_K, d_model), lambda i, l, k=k: (l[0], k, 0))
             for k in range(w_bf.shape[1] // OUT_PROJ_K)]
    return pl.pallas_call(
        _out_proj_kernel,
        grid_spec=pltpu.PrefetchScalarGridSpec(
            num_scalar_prefetch=1,
            grid=(T // tm,),
            in_specs=[act(x2d), act(a), act(b), act(c)] + slabs,
            out_specs=pl.BlockSpec((tm, d_model), lambda i, l: (i, 0)),
        ),
        out_shape=jax.ShapeDtypeStruct((T, d_model), F32),
        input_output_aliases={1: 0} if in_place else {},
        compiler_params=_params(("parallel",)),
        name="out_proj",
    )(lidx, x2d, a, b, c, *([w_bf] * len(slabs)))


def _rel_bucket_np(n):
    n = np.maximum(n, 0)
    max_exact = REL_BUCKETS // 2
    nf = np.maximum(n, 1).astype(np.float64)
    large = max_exact + (np.log(nf / max_exact) / math.log(REL_MAX_DIST / max_exact)
                         * (REL_BUCKETS - max_exact)).astype(np.int32)
    large = np.minimum(large, REL_BUCKETS - 1)
    return np.where(n < max_exact, n, large).astype(np.int32)


def _toeplitz_kernel(gen_ref, o_ref):
    blk = MOBA_BLOCK
    wide = jnp.broadcast_to(gen_ref[...], (blk, 2 * blk))
    o_ref[...] = pltpu.roll(wide, 0, 1, stride=1, stride_axis=0)[:, :blk]


def _take_static(table_t, idx):
    idx = np.asarray(idx)
    flat = idx.reshape(-1)
    starts = np.concatenate([[0], np.flatnonzero(np.diff(flat)) + 1, [flat.size]])
    rows = table_t.shape[0]
    runs = [jnp.broadcast_to(table_t[:, int(flat[s]):int(flat[s]) + 1], (rows, int(e - s)))
            for s, e in zip(starts[:-1], starts[1:])]
    return jnp.concatenate(runs, axis=1).reshape((rows,) + idx.shape)


def _toeplitz_bias_tiles(bias_t, buckets, nq):
    blk = MOBA_BLOCK
    nh = bias_t.shape[0]
    m = np.arange(2 * blk)
    d = np.arange(nq)[:, None] * blk
    gen_idx = np.where(m[None, :] <= blk, np.maximum(d - m[None, :], 0), d + 2 * blk - m[None, :])
    gen = _take_static(bias_t, buckets[gen_idx])[:, :, None, :]
    return pl.pallas_call(
        _toeplitz_kernel,
        grid=(nh, nq),
        in_specs=[pl.BlockSpec((None, None, 1, 2 * blk), lambda h, d: (h, d, 0, 0))],
        out_specs=pl.BlockSpec((None, None, blk, blk), lambda h, d: (h, d, 0, 0)),
        out_shape=jax.ShapeDtypeStruct((nh, nq, blk, blk), F32),
        compiler_params=_params(("parallel", "parallel")),
        name="bias_tiles",
    )(gen)


def _decode_bias_kernel(gen_ref, o_ref):
    dn, past = o_ref.shape
    wide = jnp.broadcast_to(gen_ref[...], (dn, gen_ref.shape[-1]))
    o_ref[...] = pltpu.roll(wide, 0, 1, stride=1, stride_axis=0)[:, :past]


def _decode_bias_rows(bias_t, buckets, past, dn):
    nh = bias_t.shape[0]
    width = past + LANES
    j = np.arange(width)
    dist = np.where(j <= past, past - j, np.minimum(past + width - j, past + dn))
    gen = _take_static(bias_t, buckets[dist])[:, None, :]
    return pl.pallas_call(
        _decode_bias_kernel,
        grid=(nh,),
        in_specs=[pl.BlockSpec((None, 1, width), lambda h: (h, 0, 0))],
        out_specs=pl.BlockSpec((dn, past), lambda h: (h, 0)),
        out_shape=jax.ShapeDtypeStruct((nh * dn, past), F32),
        compiler_params=_params(("parallel",)),
        name="decode_bias",
    )(gen)


def _pool_halo(u3, hist, tm):
    nb, n, d = u3.shape
    nt = n // tm
    first = jnp.pad(hist, ((0, 0), (HALO - POOL_HIST, 0), (0, 0)))[:, None]
    if nt == 1:
        return first
    tails = u3.reshape(nb, nt, tm, d)[:, :-1, tm - HALO:, :]
    return jnp.concatenate([first, tails], axis=1)


def kernel(x_prompt, x_sample, cache_sb_k, cache_sb_v, cache_moba_k, cache_moba_v, state_pool, page_table,
           norm_g, w_in, w_pool, pool_scale, q_norm_g, k_norm_g, w_out, rel_bias):
    nb, n, d_model = x_prompt.shape
    db, dn, _ = x_sample.shape
    depth, n_phys, page_size, nh, hd = cache_sb_k.shape
    n_pages = page_table.shape[1]
    past = n_pages * page_size
    assert page_size == PAGE_SIZE and hd == HEAD_DIM
    d_pool = POOL_GROUP * len(POOL_WINDOWS)
    d_att = nh * HEAD_DIM
    d_proj = w_in.shape[-1]
    assert d_proj == 2 * d_pool + 8 * d_att
    assert (2 * d_pool) % COL_SLAB == 0 and d_att % COL_SLAB == 0
    w_in_bf = w_in.astype(BF16)
    slabs = lambda col0, width: list(range(col0 // COL_SLAB, (col0 + width) // COL_SLAB))
    seg = lambda t: slabs(2 * d_pool + t * d_att, d_att)
    cm_main = jnp.asarray(seg(0) + seg(3) + seg(4) + seg(7) + slabs(0, 2 * d_pool), jnp.int32)
    cm_kv = jnp.asarray(seg(1) + seg(2) + seg(5) + seg(6), jnp.int32)
    m_qb, m_gb, m_qc, m_gc = (t * d_att for t in range(4))
    m_u, m_ga = 4 * d_att, 4 * d_att + d_pool
    zeros_att = jnp.zeros((depth, d_att), F32)
    g_main = jnp.concatenate([jnp.zeros((depth, m_qc), F32), jnp.tile(q_norm_g, (1, nh)),
                              jnp.zeros((depth, d_att + 2 * d_pool), F32)], axis=-1)[:, None]
    g_kv = jnp.concatenate([zeros_att, zeros_att, jnp.tile(k_norm_g, (1, nh)), zeros_att], axis=-1)[:, None]
    norm_main = (m_qc, m_qc + d_att)
    norm_kv = (2 * d_att, 3 * d_att)
    q_gain, k_gain = q_norm_g[:, None, :], k_norm_g[:, None, :]

    w_out_bf = w_out.astype(BF16)
    norm_g3 = norm_g[:, None, :]
    pool_scale3 = pool_scale[:, None, :]
    caches = [jnp.transpose(c, (0, 1, 3, 2, 4)) for c in (cache_sb_k, cache_sb_v, cache_moba_k, cache_moba_v)]
    pt_flat = page_table.reshape(-1).astype(jnp.int32)

    n_dist = past + dn + 1
    buckets = _rel_bucket_np(np.arange(n_dist))
    bias_t = rel_bias.astype(F32).T
    bias_tiles = _toeplitz_bias_tiles(bias_t, buckets, n // MOBA_BLOCK)
    bias_past = _decode_bias_rows(bias_t, buckets, past, dn)
    dist_own = np.maximum(np.arange(dn)[:, None] - np.arange(PAGE_SIZE)[None, :], 0)
    bias_own = _take_static(bias_t, buckets[dist_own]).reshape(nh * dn, PAGE_SIZE)

    tm_proj = min(n, IN_PROJ_ROWS)
    tm_p = min(n, ROW_TILE)
    hist0 = jnp.zeros((nb, POOL_HIST, d_pool), F32)

    yp, ys = x_prompt.reshape(nb * n, d_model), x_sample.reshape(db * dn, d_model)
    ts = db * dn
    new_stack = lambda: jnp.zeros((depth, nb, nh, n, HEAD_DIM), F32)
    kv_stacks = (new_stack(), new_stack(), new_stack())
    mk_stack = new_stack()
    pool_p, kv_s, pool_s = [], [], []
    for l in range(depth):
        lidx = jnp.full((1,), l, jnp.int32)

        p = _in_proj(lidx, cm_main, yp, norm_g3, w_in_bf, nb, n, tm_proj, IN_PROJ_COLS)
        sbk, sbv, kc_raw, mbv = _kv_proj(lidx, cm_kv, yp, norm_g3, w_in_bf, kv_stacks, depth, nb, n, nh, tm_proj,
                                         (True, True, False, True))
        kv_stacks = (sbk, sbv, mbv)
        u3 = p.reshape(nb, n, -1)[:, :, m_u:m_u + d_pool]
        a = _pool(lidx, p, _pool_halo(u3, hist0, tm_p), w_pool, pool_scale3, nb, n, tm_p, 0, m_u, m_ga, BF16)
        bmix = _sb_prefill(lidx, p, sbk, sbv, nb, n, nh, m_qb, m_gb, SB_TILE)
        cmix, mk_stack = _moba_prefill(lidx, p, kc_raw, mbv, mk_stack, bias_tiles, q_gain, k_gain,
                                       depth, nb, n, nh, m_qc, m_gc)
        yp = _out_proj(lidx, yp, a, bmix, cmix, w_out_bf, tm_p, in_place=l > 0)
        pool_p.append(u3[:, n - POOL_HIST:])

        ps = _in_proj(lidx, cm_main, ys, norm_g3, w_in_bf, 1, ts, ts, IN_PROJ_COLS, g_main, norm_main)
        kvs = _in_proj(lidx, cm_kv, ys, norm_g3, w_in_bf, 1, ts, ts, d_att, g_kv, norm_kv)
        ps3 = ps.reshape(db, dn, -1)
        kvs5 = jnp.transpose(kvs.reshape(db, dn, 4, nh, HEAD_DIM), (2, 0, 3, 1, 4))
        hist_s = state_pool[l]
        us3 = ps3[:, :, m_u:m_u + d_pool]
        a_s = _pool(lidx, ps, _pool_halo(us3, hist_s, dn), w_pool, pool_scale3, db, dn, dn, past, m_u, m_ga, F32)
        col = lambda c: ps3[:, :, c:c + d_att]
        new_page = lambda t: jnp.pad(kvs5[t], ((0, 0), (0, 0), (0, PAGE_SIZE - dn), (0, 0)))
        b_s = _sb_decode(lidx, pt_flat, col(m_qb), new_page(0), new_page(1), col(m_gb),
                         caches[0], caches[1], db, dn, nh, n_pages)
        scores, km = _moba_scores(lidx, pt_flat, col(m_qc), caches[2], db, dn, nh, n_pages)
        c_s = _moba_values(lidx, pt_flat, col(m_qc), km, scores, bias_past, new_page(2), new_page(3),
                           bias_own, col(m_gc), caches[3], db, dn, nh, n_pages)
        ys = _out_proj(lidx, ys, a_s, b_s.reshape(ts, d_att), c_s.reshape(ts, d_att), w_out_bf, ts, in_place=l > 0)
        kv_s.append(kvs5)
        pool_s.append(jnp.concatenate([hist_s, us3], axis=1)[:, -POOL_HIST:])

    rows = lambda t: jnp.transpose(t, (0, 1, 3, 2, 4))
    sbk, sbv, mbv = kv_stacks
    kv_s = jnp.stack(kv_s, axis=1)
    return ((yp.reshape(nb, n, d_model), ys.reshape(db, dn, d_model))
            + (rows(sbk), rows(sbv), rows(mk_stack), rows(mbv), jnp.stack(pool_p))
            + tuple(rows(kv_s[t]) for t in range(4)) + (jnp.stack(pool_s),))
```
